```python
import math
import jax, jax.numpy as jnp
from jax import lax
import numpy as np


D_MODEL = 1024
BATCH = 4
SEQ = 4096
DEPTH = 1
DEC_BATCH = 128
DEC_SEQ = 1
PAST_LEN = 8192
PAGE_SIZE = 128

HEAD_DIM = 64
MIX_WIDTH = D_MODEL
N_HEADS_A = MIX_WIDTH // 2 // HEAD_DIM
N_HEADS_B = MIX_WIDTH // 2 // HEAD_DIM
N_KV_B = N_HEADS_B // 4
GQA_GROUP = N_HEADS_B // N_KV_B
A_PATTERNS = ((128, 1), (512, 4), (2048, 16))
A_WINDOW_MAX = 2048
B_WINDOW = 128
BLK = 128
D_FF = 2816
NUM_BUCKETS = 32
MAX_DISTANCE = 2048
N_HEADS_TOTAL = N_HEADS_A + N_HEADS_B
ALPHA = (2 * DEPTH) ** 0.25
BETA = (8 * DEPTH) ** -0.25
LN_EPS = 1e-5
NEG = -1e30
SCALE = HEAD_DIM ** -0.5
PROJ_COLS = 3 * N_HEADS_A * HEAD_DIM + N_HEADS_B * HEAD_DIM + 2 * N_KV_B * HEAD_DIM

kernel_name = "hymba_dilated_swa_sink_macaron_deepnorm_step"


def t5_bucket(n):
    max_exact = NUM_BUCKETS // 2
    nf = jnp.maximum(n, 1).astype(jnp.float32)
    large = max_exact + (jnp.log(nf / max_exact) / math.log(MAX_DISTANCE / max_exact)
                         * (NUM_BUCKETS - max_exact)).astype(jnp.int32)
    return jnp.where(n < max_exact, n, jnp.minimum(large, NUM_BUCKETS - 1))


def layer_norm(x, g, b):
    xf = x.astype(jnp.float32)
    mu = jnp.mean(xf, -1, keepdims=True)
    var = jnp.mean(jnp.square(xf - mu), -1, keepdims=True)
    return ((xf - mu) * lax.rsqrt(var + LN_EPS) * g + b).astype(x.dtype)


def swiglu(h, wg, wu, wd):
    return (jax.nn.silu(h @ wg) * (h @ wu)) @ wd


def softmax_parts(s, sink):
    m = jnp.max(s, -1, keepdims=True)
    if sink is not None:
        m = jnp.maximum(m, sink)
    p = jnp.exp(s - m)
    den = jnp.sum(p, -1, keepdims=True)
    if sink is not None:
        den = den + jnp.exp(sink - m)
    return p, den, m


def banded_attn(q, k, v, bias_heads, step, win_units, sink):
    N, L, Hk, G, hd = q.shape
    nb = L // BLK
    qb = q.reshape(N, nb, BLK, Hk, G, hd)
    kb = k.reshape(N, nb, BLK, Hk, hd)
    vb = v.reshape(N, nb, BLK, Hk, hd)
    pad = ((0, 0), (1, 0), (0, 0), (0, 0), (0, 0))
    kk = jnp.concatenate([jnp.pad(kb, pad)[:, :-1], kb], axis=2)
    vv = jnp.concatenate([jnp.pad(vb, pad)[:, :-1], vb], axis=2)
    dist = jnp.arange(BLK)[:, None] + BLK - jnp.arange(2 * BLK)[None, :]
    valid = (dist >= 0) & (dist <= win_units)
    bias = bias_heads[t5_bucket(jnp.maximum(dist, 0) * step)].astype(jnp.float32)
    bias = jnp.where(valid[..., None], bias, NEG)
    bias = bias.transpose(2, 0, 1).reshape(Hk, G, BLK, 2 * BLK)
    first = (jnp.arange(nb)[:, None] == 0) & (jnp.arange(2 * BLK)[None, :] < BLK)
    s = jnp.einsum('nbqhgd,nbkhd->nbhgqk', qb, kk).astype(jnp.float32) * SCALE + bias
    s = jnp.where(first[None, :, None, None, None, :], NEG, s)
    sink_b = None if sink is None else sink.astype(jnp.float32).reshape(1, 1, Hk, G, 1, 1)
    p, den, m = softmax_parts(s, sink_b)
    o = jnp.einsum('nbhgqk,nbkhd->nbqhgd', p, vv.astype(jnp.float32))
    o = o / den[..., 0].transpose(0, 1, 4, 2, 3)[..., None]
    lse = (m + jnp.log(den))[..., 0].transpose(0, 1, 4, 2, 3)
    return o.reshape(N, L, Hk, G, hd), lse.reshape(N, L, Hk, G)


def gathered_attn(q, kfull, vfull, base, bias_heads, step, win_units, sink):
    N, T, Hk, G, hd = q.shape
    i = jnp.arange(win_units + 1)
    idx = base + jnp.arange(T)[:, None] - step * i[None, :]
    valid = idx >= 0
    idxc = jnp.maximum(idx, 0)
    kg = kfull[:, idxc]
    vg = vfull[:, idxc]
    bias = bias_heads[t5_bucket(i * step)].astype(jnp.float32).T.reshape(Hk, G, win_units + 1)
    s = jnp.einsum('nthgd,ntkhd->nthgk', q, kg).astype(jnp.float32) * SCALE + bias
    s = jnp.where(valid[None, :, None, None, :], s, NEG)
    sink_b = None if sink is None else sink.astype(jnp.float32).reshape(1, 1, Hk, G, 1)
    p, den, m = softmax_parts(s, sink_b)
    o = jnp.einsum('nthgk,ntkhd->nthgd', p, vg.astype(jnp.float32)) / den
    return o, (m + jnp.log(den))[..., 0]


def split_proj(h, w_in):
    na = N_HEADS_A * HEAD_DIM
    nq = N_HEADS_B * HEAD_DIM
    nkv = N_KV_B * HEAD_DIM
    pre = h.shape[:-1]
    proj = h @ w_in
    qa, ka, va, qb, kb, vb = jnp.split(proj, [na, 2 * na, 3 * na, 3 * na + nq, 3 * na + nq + nkv], axis=-1)
    return (qa.reshape(*pre, N_HEADS_A, HEAD_DIM), ka.reshape(*pre, N_HEADS_A, HEAD_DIM),
            va.reshape(*pre, N_HEADS_A, HEAD_DIM), qb.reshape(*pre, N_KV_B, GQA_GROUP, HEAD_DIM),
            kb.reshape(*pre, N_KV_B, HEAD_DIM), vb.reshape(*pre, N_KV_B, HEAD_DIM))


def merge_branches(outs, lses):
    w = jax.nn.softmax(jnp.stack(lses), axis=0)
    return jnp.einsum('rbsh,rbshd->bshd', w, jnp.stack(outs))


def mixer_prompt(h, w_in, w_out, sinks, rel_bias):
    Bn, S, _ = h.shape
    qa, ka, va, qb, kb, vb = split_proj(h, w_in)
    outs, lses = [], []
    for window, dil in A_PATTERNS:
        L = S // dil
        Lp = -(-L // BLK) * BLK

        def to_res(t):
            t = t.reshape(Bn, L, dil, N_HEADS_A, HEAD_DIM).transpose(0, 2, 1, 3, 4)
            t = t.reshape(Bn * dil, L, N_HEADS_A, HEAD_DIM)
            return jnp.pad(t, ((0, 0), (0, Lp - L), (0, 0), (0, 0)))

        o, lse = banded_attn(to_res(qa)[:, :, :, None, :], to_res(ka), to_res(va),
                             rel_bias[:, :N_HEADS_A], dil, window // dil, None)
        o = o[:, :L, :, 0].reshape(Bn, dil, L, N_HEADS_A, HEAD_DIM).transpose(0, 2, 1, 3, 4)
        lse = lse[:, :L, :, 0].reshape(Bn, dil, L, N_HEADS_A).transpose(0, 2, 1, 3)
        outs.append(o.reshape(Bn, S, N_HEADS_A, HEAD_DIM))
        lses.append(lse.reshape(Bn, S, N_HEADS_A))
    oa = merge_branches(outs, lses)
    ob, _ = banded_attn(qb, kb, vb, rel_bias[:, N_HEADS_A:], 1, B_WINDOW, sinks)
    mixed = jnp.concatenate([oa.reshape(Bn, S, -1), ob.reshape(Bn, S, -1)], -1).astype(h.dtype)
    ra = min(A_WINDOW_MAX, S)
    rb = min(B_WINDOW, S)
    return mixed @ w_out, ka[:, S - ra:], va[:, S - ra:], kb[:, S - rb:], vb[:, S - rb:]


def mixer_sample(h, ca_k, ca_v, cb_k, cb_v, w_in, w_out, sinks, rel_bias):
    Bn, T, _ = h.shape
    qa, ka, va, qb, kb, vb = split_proj(h, w_in)
    kfa = jnp.concatenate([ca_k, ka.astype(ca_k.dtype)], 1)
    vfa = jnp.concatenate([ca_v, va.astype(ca_v.dtype)], 1)
    outs, lses = [], []
    for window, dil in A_PATTERNS:
        o, lse = gathered_attn(qa[:, :, :, None, :], kfa, vfa, ca_k.shape[1],
                               rel_bias[:, :N_HEADS_A], dil, window // dil, None)
        outs.append(o[:, :, :, 0])
        lses.append(lse[:, :, :, 0])
    oa = merge_branches(outs, lses)
    kfb = jnp.concatenate([cb_k, kb.astype(cb_k.dtype)], 1)
    vfb = jnp.concatenate([cb_v, vb.astype(cb_v.dtype)], 1)
    ob, _ = gathered_attn(qb, kfb, vfb, cb_k.shape[1], rel_bias[:, N_HEADS_A:], 1, B_WINDOW, sinks)
    mixed = jnp.concatenate([oa.reshape(Bn, T, -1), ob.reshape(Bn, T, -1)], -1).astype(h.dtype)
    return mixed @ w_out, ka, va, kb, vb


def trunk(x, c, mixer, w_ada, b_ada, ffn1_wg, ffn1_wu, ffn1_wd, ffn2_wg, ffn2_wu, ffn2_wd, ln_g, ln_b):
    ak, av, bk, bv = [], [], [], []
    for l in range(DEPTH):
        mod = (jax.nn.silu(c) @ w_ada[l] + b_ada[l]).reshape(c.shape[0], 9, 1, D_MODEL)
        h = x * (1 + mod[:, 1]) + mod[:, 0]
        x = layer_norm(ALPHA * x + 0.5 * mod[:, 2] * swiglu(h, ffn1_wg[l], ffn1_wu[l], ffn1_wd[l]),
                       ln_g[l, 0], ln_b[l, 0])
        h = x * (1 + mod[:, 4]) + mod[:, 3]
        out, k_a, v_a, k_b, v_b = mixer(l, h)
        x = layer_norm(ALPHA * x + mod[:, 5] * out, ln_g[l, 1], ln_b[l, 1])
        h = x * (1 + mod[:, 7]) + mod[:, 6]
        x = layer_norm(ALPHA * x + 0.5 * mod[:, 8] * swiglu(h, ffn2_wg[l], ffn2_wu[l], ffn2_wd[l]),
                       ln_g[l, 2], ln_b[l, 2])
        ak.append(k_a); av.append(v_a); bk.append(k_b); bv.append(v_b)
    return x, jnp.stack(ak), jnp.stack(av), jnp.stack(bk), jnp.stack(bv)


def setup_inputs(seed: int = 0) -> dict:
    key = jax.random.key(seed)
    ks = jax.random.split(key, 24)

    def nrm(k, shape, s):
        return jax.random.normal(k, shape, jnp.float32) * s

    a_rows = min(A_WINDOW_MAX, PAST_LEN)
    b_rows = min(B_WINDOW, PAST_LEN)
    na = N_HEADS_A * HEAD_DIM
    nq = N_HEADS_B * HEAD_DIM
    nkv = N_KV_B * HEAD_DIM
    col_scale = jnp.concatenate([jnp.ones((2 * na,), jnp.float32), jnp.full((na,), BETA, jnp.float32),
                                 jnp.ones((nq + nkv,), jnp.float32), jnp.full((nkv,), BETA, jnp.float32)])
    return {
        "x_prompt": nrm(ks[0], (BATCH, SEQ, D_MODEL), 1.0),
        "x_sample": nrm(ks[1], (DEC_BATCH, DEC_SEQ, D_MODEL), 1.0),
        "cache_a_k": nrm(ks[2], (DEPTH, DEC_BATCH, a_rows, N_HEADS_A, HEAD_DIM), 1.0),
        "cache_a_v": nrm(ks[3], (DEPTH, DEC_BATCH, a_rows, N_HEADS_A, HEAD_DIM), BETA),
        "cache_b_k": nrm(ks[4], (DEPTH, DEC_BATCH, b_rows, N_KV_B, HEAD_DIM), 1.0),
        "cache_b_v": nrm(ks[5], (DEPTH, DEC_BATCH, b_rows, N_KV_B, HEAD_DIM), BETA),
        "c_prompt": nrm(ks[6], (BATCH, D_MODEL), 1.0),
        "c_sample": nrm(ks[7], (DEC_BATCH, D_MODEL), 1.0),
        "rel_bias": nrm(ks[8], (NUM_BUCKETS, N_HEADS_TOTAL), 0.5),
        "w_ada": nrm(ks[9], (DEPTH, D_MODEL, 9 * D_MODEL), 0.5 * D_MODEL ** -0.5),
        "b_ada": nrm(ks[10], (DEPTH, 9 * D_MODEL), 0.02),
        "ffn1_wg": nrm(ks[11], (DEPTH, D_MODEL, D_FF), D_MODEL ** -0.5),
        "ffn1_wu": nrm(ks[12], (DEPTH, D_MODEL, D_FF), D_MODEL ** -0.5),
        "ffn1_wd": nrm(ks[13], (DEPTH, D_FF, D_MODEL), BETA * D_FF ** -0.5),
        "w_in": nrm(ks[14], (DEPTH, D_MODEL, PROJ_COLS), D_MODEL ** -0.5) * col_scale,
        "w_out": nrm(ks[15], (DEPTH, MIX_WIDTH, D_MODEL), BETA * MIX_WIDTH ** -0.5),
        "sinks": nrm(ks[16], (DEPTH, N_HEADS_B), 0.5),
        "ffn2_wg": nrm(ks[17], (DEPTH, D_MODEL, D_FF), D_MODEL ** -0.5),
        "ffn2_wu": nrm(ks[18], (DEPTH, D_MODEL, D_FF), D_MODEL ** -0.5),
        "ffn2_wd": nrm(ks[19], (DEPTH, D_FF, D_MODEL), BETA * D_FF ** -0.5),
        "ln_g": 1.0 + nrm(ks[20], (DEPTH, 3, D_MODEL), 0.05),
        "ln_b": nrm(ks[21], (DEPTH, 3, D_MODEL), 0.05),
    }


def reference(x_prompt, x_sample, cache_a_k, cache_a_v, cache_b_k, cache_b_v, c_prompt, c_sample,
              rel_bias, w_ada, b_ada, ffn1_wg, ffn1_wu, ffn1_wd, w_in, w_out, sinks,
              ffn2_wg, ffn2_wu, ffn2_wd, ln_g, ln_b):
    def mix_p(l, h):
        return mixer_prompt(h, w_in[l], w_out[l], sinks[l], rel_bias)

    def mix_s(l, h):
        return mixer_sample(h, cache_a_k[l], cache_a_v[l], cache_b_k[l], cache_b_v[l],
                            w_in[l], w_out[l], sinks[l], rel_bias)

    y_prompt, pak, pav, pbk, pbv = trunk(x_prompt, c_prompt, mix_p, w_ada, b_ada, ffn1_wg, ffn1_wu, ffn1_wd,
                                         ffn2_wg, ffn2_wu, ffn2_wd, ln_g, ln_b)
    y_sample, sak, sav, sbk, sbv = trunk(x_sample, c_sample, mix_s, w_ada, b_ada, ffn1_wg, ffn1_wu, ffn1_wd,
                                         ffn2_wg, ffn2_wu, ffn2_wd, ln_g, ln_b)
    return (y_prompt, y_sample, pak, pav, pbk, pbv, sak, sav, sbk, sbv)
```

```python
import functools
import math

import jax
import jax.numpy as jnp
from jax import lax
from jax.experimental import pallas as pl
from jax.experimental.pallas import tpu as pltpu

HEAD_DIM = 64
N_HEADS_A = 8
N_HEADS_B = 8
N_KV_B = 2
A_PATTERNS = ((128, 1), (512, 4), (2048, 16))
B_WINDOW = 128
BLK = 128
NUM_BUCKETS = 32
MAX_DISTANCE = 2048
LN_EPS = 1e-5
NEG = -1e30
SCALE = HEAD_DIM ** -0.5

LANES = 128
V7X_VMEM_BYTES = 64 * 1024 * 1024
A_COLS = N_HEADS_A * HEAD_DIM
N_PAIRS = A_COLS // LANES
N_MOD = 9

F32 = jnp.float32
BF16 = jnp.bfloat16


def _vmem_limit(pipelined_bytes, scratch_bytes=0, temp_bytes=0):
    want = 2 * pipelined_bytes + scratch_bytes + temp_bytes + (4 << 20)
    return int(min(want, V7X_VMEM_BYTES - (6 << 20)))


def _resident_spec(shape):
    return pl.BlockSpec(shape, lambda *_: (0,) * len(shape), pipeline_mode=pl.Buffered(1))


def _nbytes(shape, dtype):
    return math.prod(shape) * jnp.dtype(dtype).itemsize


def _layer_norm(x, g, b):
    mu = jnp.mean(x, axis=-1, keepdims=True)
    xc = x - mu
    var = jnp.mean(xc * xc, axis=-1, keepdims=True)
    return xc * lax.rsqrt(var + LN_EPS) * g + b


def _silu(x):
    return x * jax.nn.sigmoid(x)


def _ada_kernel(c_ref, w_ref, b_ref, o_ref):
    a = _silu(c_ref[...]).astype(BF16)
    o_ref[...] = jnp.dot(a, w_ref[...].astype(BF16), preferred_element_type=F32) + b_ref[...]


def _ada_call(c_all, w_ada, b_ada):
    m, d = c_all.shape
    n = w_ada.shape[1]
    tn = n // 8
    blocks = _nbytes((m, d), F32) + _nbytes((d, tn), F32) + _nbytes((m, tn), F32)
    return pl.pallas_call(
        _ada_kernel,
        grid=(n // tn,),
        in_specs=[pl.BlockSpec((m, d), lambda j: (0, 0)),
                  pl.BlockSpec((d, tn), lambda j: (0, j)),
                  pl.BlockSpec((1, tn), lambda j: (0, j))],
        out_specs=pl.BlockSpec((m, tn), lambda j: (0, j)),
        out_shape=jax.ShapeDtypeStruct((m, n), F32),
        compiler_params=pltpu.CompilerParams(
            dimension_semantics=("arbitrary",),
            vmem_limit_bytes=_vmem_limit(blocks, temp_bytes=_nbytes((d, tn), BF16))),
        name="ada_mod",
    )(c_all, w_ada, b_ada.reshape(1, n))


def _t5_bucket(n):
    max_exact = NUM_BUCKETS // 2
    nf = jnp.maximum(n, 1).astype(F32)
    large = max_exact + (jnp.log(nf / max_exact) / math.log(MAX_DISTANCE / max_exact)
                         * (NUM_BUCKETS - max_exact)).astype(jnp.int32)
    return jnp.where(n < max_exact, n, jnp.minimum(large, NUM_BUCKETS - 1))


def _bucket_maps(a_rows, b_rows):
    steps = [d for _, d in A_PATTERNS] + [1]
    q = jnp.arange(BLK)[:, None]
    k = jnp.arange(2 * BLK)[None, :]
    dist = q + BLK - k
    valid = (dist >= 0) & (dist <= BLK)
    band = jnp.stack([jnp.where(valid, _t5_bucket(jnp.maximum(dist, 0) * s), -1) for s in steps])

    def cached(rows, step):
        back = rows - jnp.arange(rows)
        hit = (back % step == 0) & (back // step <= BLK)
        return jnp.where(hit, _t5_bucket(back), -1)[None, :].astype(jnp.int32)

    cached_a = jnp.stack([cached(a_rows, s) for s in steps[:-1]])
    return band.astype(jnp.int32), cached_a, cached(b_rows, 1)


def _band_bias_kernel(rel_ref, bmap_ref, o_ref):
    t = pl.program_id(0)
    hp = pl.program_id(1)
    bm = bmap_ref[...]
    head0 = jnp.where(t == len(A_PATTERNS), N_HEADS_A, 0) + 2 * hp
    for half in range(2):
        acc = jnp.full(bm.shape, NEG, F32)
        for b in range(NUM_BUCKETS):
            acc = jnp.where(bm == b, rel_ref[b, head0 + half], acc)
        o_ref[half * BLK:(half + 1) * BLK, :] = acc


def _band_bias_call(rel_bias, band_map):
    n_tab = band_map.shape[0]
    return pl.pallas_call(
        _band_bias_kernel,
        grid=(n_tab, N_PAIRS),
        in_specs=[pl.BlockSpec(memory_space=pltpu.SMEM),
                  pl.BlockSpec((None, BLK, 2 * BLK), lambda t, hp: (t, 0, 0))],
        out_specs=pl.BlockSpec((None, None, 2 * BLK, 2 * BLK), lambda t, hp: (t, hp, 0, 0)),
        out_shape=jax.ShapeDtypeStruct((n_tab, N_PAIRS, 2 * BLK, 2 * BLK), F32),
        compiler_params=pltpu.CompilerParams(dimension_semantics=("arbitrary", "arbitrary")),
        name="band_bias",
    )(rel_bias, band_map)


def _cached_bias_kernel(relt_ref, amap_ref, bmap_ref, oa_ref, ob_ref):
    relt = relt_ref[...]

    def table(bm, heads):
        acc = jnp.full(bm.shape, NEG, F32)
        for b in range(NUM_BUCKETS):
            acc = jnp.where(bm == b, heads[:, b:b + 1], acc)
        return acc

    for t in range(amap_ref.shape[0]):
        oa_ref[t] = table(jnp.broadcast_to(amap_ref[t], oa_ref.shape[1:]), relt[:N_HEADS_A])
    ob_ref[...] = table(jnp.broadcast_to(bmap_ref[...], ob_ref.shape), relt[N_HEADS_A:])


def _cached_bias_call(rel_bias, cached_a, cached_b):
    return pl.pallas_call(
        _cached_bias_kernel,
        out_shape=(jax.ShapeDtypeStruct((cached_a.shape[0], N_HEADS_A, cached_a.shape[2]), F32),
                   jax.ShapeDtypeStruct((N_HEADS_B, cached_b.shape[1]), F32)),
        name="cached_bias",
    )(rel_bias.T, cached_a, cached_b)


def _ffn_kernel(*refs, alpha, pre_mix, mix_slots, n_chunks, fc):
    refs = list(refs)
    x_ref = refs.pop(0)
    if pre_mix:
        mix_refs = [refs.pop(0) for _ in range(2 if mix_slots else 1)]
        wo_ref, gm_ref, ln1g_ref, ln1b_ref = (refs.pop(0) for _ in range(4))
    sh_ref, sc_ref, gt_ref, wg_ref, wu_ref, wd_ref, lng_ref, lnb_ref, o_ref, h_ref, acc_ref = refs

    x = x_ref[...]
    if pre_mix:
        if mix_slots:
            mixed = jnp.concatenate([r[j] for r in mix_refs for j in range(N_PAIRS)], axis=-1)
        else:
            mixed = mix_refs[0][...].astype(BF16)
        y = jnp.dot(mixed, wo_ref[...], preferred_element_type=F32)
        x = _layer_norm(alpha * x + gm_ref[...] * y, ln1g_ref[...], ln1b_ref[...])
    h_ref[...] = (x * (1.0 + sc_ref[...]) + sh_ref[...]).astype(BF16)
    for c in range(n_chunks):
        cols = slice(c * fc, (c + 1) * fc)
        g = jnp.dot(h_ref[...], wg_ref[:, cols], preferred_element_type=F32)
        u = jnp.dot(h_ref[...], wu_ref[:, cols], preferred_element_type=F32)
        y = jnp.dot((_silu(g) * u).astype(BF16), wd_ref[cols, :], preferred_element_type=F32)
        if c == 0:
            acc_ref[...] = y
        else:
            acc_ref[...] += y
    o_ref[...] = _layer_norm(alpha * x + 0.5 * gt_ref[...] * acc_ref[...], lng_ref[...], lnb_ref[...])


def _ffn_call(x, mod2, mod3, mod_ks, seq0, rows_per_seq, wg, wu, wd, ln_g, ln_b, alpha, tm, fc,
              mix=None, w_out=None, gate_k=None, ln1_g=None, ln1_b=None):
    n, d = x.shape
    d_ff = wg.shape[1]
    n_chunks = d_ff // fc
    assert n_chunks * fc == d_ff
    per_row = rows_per_seq == 1
    tiles_per_seq = max(rows_per_seq // tm, 1)

    def mod_spec(k):
        if per_row:
            return pl.BlockSpec((tm, d), lambda i, k=k: (seq0 // tm + i, k))
        return pl.BlockSpec((None, 1, d), lambda i, k=k: ((seq0 + i // tiles_per_seq) * N_MOD + k, 0, 0))

    mod_arr = mod2 if per_row else mod3
    row_spec = pl.BlockSpec((tm, d), lambda i: (i, 0))
    vec_spec = pl.BlockSpec((1, d), lambda i: (0, 0))
    args, specs = [x], [row_spec]
    pre_mix = mix is not None
    mix_slots = pre_mix and isinstance(mix, tuple)
    blocks = 2 * _nbytes((tm, d), F32) + (3 * _nbytes((tm, d), F32) if per_row else 0)
    resident = 3 * _nbytes(wg.shape, BF16)
    if pre_mix:
        if mix_slots:
            for mm in mix:
                args.append(mm)
                specs.append(pl.BlockSpec((N_PAIRS, tm, LANES), lambda i: (0, i, 0)))
        else:
            args.append(mix)
            specs.append(row_spec)
        args += [w_out, mod_arr, ln1_g.reshape(1, d), ln1_b.reshape(1, d)]
        specs += [_resident_spec((d, d)), mod_spec(gate_k), vec_spec, vec_spec]
        blocks += _nbytes((tm, d), F32)
        resident += _nbytes((d, d), BF16)
    args += [mod_arr, mod_arr, mod_arr, wg, wu, wd, ln_g.reshape(1, d), ln_b.reshape(1, d)]
    specs += [mod_spec(mod_ks[0]), mod_spec(mod_ks[1]), mod_spec(mod_ks[2]),
              _resident_spec(wg.shape), _resident_spec(wu.shape), _resident_spec(wd.shape), vec_spec, vec_spec]
    scratch = _nbytes((tm, d), BF16) + _nbytes((tm, d), F32) + resident
    temps = 6 * _nbytes((tm, fc), F32) + 3 * _nbytes((tm, d), F32)
    return pl.pallas_call(
        functools.partial(_ffn_kernel, alpha=alpha, pre_mix=pre_mix, mix_slots=mix_slots,
                          n_chunks=n_chunks, fc=fc),
        grid=(n // tm,),
        in_specs=specs,
        out_specs=row_spec,
        out_shape=jax.ShapeDtypeStruct((n, d), F32),
        scratch_shapes=[pltpu.VMEM((tm, d), BF16), pltpu.VMEM((tm, d), F32)],
        compiler_params=pltpu.CompilerParams(
            dimension_semantics=("arbitrary",),
            vmem_limit_bytes=_vmem_limit(blocks, scratch, temps)),
        name="ffn_mix" if pre_mix else "ffn",
    )(*args)


def _proj_prompt_kernel(x_ref, sh_ref, sc_ref, w_ref, p_ref, ak_ref, av_ref, bk_ref, bv_ref,
                        *, tm, tiles_per_seq, a_tiles, b_rows):
    t = pl.program_id(0) % tiles_per_seq
    h = (x_ref[...] * (1.0 + sc_ref[...]) + sh_ref[...]).astype(BF16)
    n_groups = w_ref.shape[1] // A_COLS
    lane_lo = lax.broadcasted_iota(jnp.int32, (b_rows, LANES), 1) < HEAD_DIM
    for g in range(n_groups):
        res = jnp.dot(h, w_ref[:, g * A_COLS:(g + 1) * A_COLS], preferred_element_type=F32)
        scaled = res * SCALE if g in (0, 3) else res
        for s in range(N_PAIRS):
            p_ref[g * N_PAIRS + s] = scaled[:, s * LANES:(s + 1) * LANES].astype(BF16)
        if g in (1, 2):
            out_ref = ak_ref if g == 1 else av_ref

            @pl.when(t >= tiles_per_seq - a_tiles)
            def _(res=res, out_ref=out_ref):
                out_ref[...] = res.T.reshape(out_ref.shape)
        if g == 4:
            @pl.when(t == tiles_per_seq - 1)
            def _(res=res):
                tail = res[tm - b_rows:, :]
                kb = jnp.where(lane_lo, tail[:, 0:LANES], tail[:, LANES:2 * LANES])
                vb = jnp.where(lane_lo, tail[:, 2 * LANES:3 * LANES], tail[:, 3 * LANES:4 * LANES])
                bk_ref[...] = kb.T.reshape(bk_ref.shape)
                bv_ref[...] = vb.T.reshape(bv_ref.shape)


def _proj_prompt_call(x, mod3, seq0, seq, w_ext, a_rows, b_rows, tm):
    n, d = x.shape
    batch = n // seq
    tiles_per_seq = seq // tm
    a_tiles = a_rows // tm
    n_slots = w_ext.shape[1] // LANES

    def mod_spec(k):
        return pl.BlockSpec((None, 1, d), lambda i: ((seq0 + i // tiles_per_seq) * N_MOD + k, 0, 0))

    def a_map(i):
        return (i // tiles_per_seq, 0, 0, jnp.maximum(i % tiles_per_seq - (tiles_per_seq - a_tiles), 0))

    a_blk = (None, N_HEADS_A, HEAD_DIM, tm)
    b_blk = (None, N_KV_B, HEAD_DIM, b_rows)
    blocks = (_nbytes((tm, d), F32) + _nbytes(w_ext.shape, BF16) + _nbytes((n_slots, tm, LANES), BF16)
              + 2 * _nbytes((tm, A_COLS), F32) + 2 * _nbytes((b_rows, LANES), F32))
    return pl.pallas_call(
        functools.partial(_proj_prompt_kernel, tm=tm, tiles_per_seq=tiles_per_seq, a_tiles=a_tiles,
                          b_rows=b_rows),
        grid=(n // tm,),
        in_specs=[pl.BlockSpec((tm, d), lambda i: (i, 0)), mod_spec(3), mod_spec(4),
                  pl.BlockSpec(w_ext.shape, lambda i: (0, 0))],
        out_specs=[pl.BlockSpec((n_slots, tm, LANES), lambda i: (0, i, 0)),
                   pl.BlockSpec(a_blk, a_map),
                   pl.BlockSpec(a_blk, a_map),
                   pl.BlockSpec(b_blk, lambda i: (i // tiles_per_seq, 0, 0, 0)),
                   pl.BlockSpec(b_blk, lambda i: (i // tiles_per_seq, 0, 0, 0))],
        out_shape=[jax.ShapeDtypeStruct((n_slots, n, LANES), BF16),
                   jax.ShapeDtypeStruct((batch, N_HEADS_A, HEAD_DIM, a_rows), F32),
                   jax.ShapeDtypeStruct((batch, N_HEADS_A, HEAD_DIM, a_rows), F32),
                   jax.ShapeDtypeStruct((batch, N_KV_B, HEAD_DIM, b_rows), F32),
                   jax.ShapeDtypeStruct((batch, N_KV_B, HEAD_DIM, b_rows), F32)],
        compiler_params=pltpu.CompilerParams(
            dimension_semantics=("arbitrary",),
            vmem_limit_bytes=_vmem_limit(blocks, temp_bytes=4 * _nbytes((tm, A_COLS), F32))),
        name="proj_prompt",
    )(x, mod3, mod3, w_ext)


def _proj_sample_kernel(x_ref, sh_ref, sc_ref, w_ref, o_ref):
    h = (x_ref[...] * (1.0 + sc_ref[...]) + sh_ref[...]).astype(BF16)
    o_ref[...] = jnp.dot(h, w_ref[...], preferred_element_type=F32)


def _proj_sample_call(x, mod2, w_ext):
    n, d = x.shape
    cols = w_ext.shape[1]
    blocks = 3 * _nbytes((n, d), F32) + _nbytes(w_ext.shape, BF16) + _nbytes((n, cols), F32)
    return pl.pallas_call(
        _proj_sample_kernel,
        grid=(1,),
        in_specs=[pl.BlockSpec((n, d), lambda i: (0, 0)),
                  pl.BlockSpec((n, d), lambda i: (0, 3)),
                  pl.BlockSpec((n, d), lambda i: (0, 4)),
                  pl.BlockSpec(w_ext.shape, lambda i: (0, 0))],
        out_specs=pl.BlockSpec((n, cols), lambda i: (0, 0)),
        out_shape=jax.ShapeDtypeStruct((n, cols), F32),
        compiler_params=pltpu.CompilerParams(
            dimension_semantics=("arbitrary",), vmem_limit_bytes=_vmem_limit(blocks)),
        name="proj_sample",
    )(x, mod2, mod2, w_ext)


def _pair_scores(q, kk, bias, lane_lo):
    zero = jnp.zeros_like(q)
    q2 = jnp.concatenate([jnp.where(lane_lo, q, zero), jnp.where(lane_lo, zero, q)], axis=0)
    s = lax.dot_general(q2, kk, (((1,), (1,)), ((), ())), preferred_element_type=F32)
    return s + bias


def _pair_select(a, lane_lo):
    return jnp.where(lane_lo, a[:BLK], a[BLK:])


def _attn_a_kernel(q16_ref, k16_ref, v16_ref, q4_ref, k4_ref, v4_ref, q1_ref, k1_ref, v1_ref,
                   bias_ref, o_ref, acc_ref, m_ref, den_ref, *, seq, steps_per_branch):
    step = pl.program_id(2)
    lane_lo = lax.broadcasted_iota(jnp.int32, (BLK, LANES), 1) < HEAD_DIM

    def block(q, kk, vv, bias, rows, first_branch, last_branch):
        s = _pair_scores(q, kk, bias, lane_lo)
        m2 = jnp.max(s, axis=-1, keepdims=True)
        if not first_branch:
            m_old = m_ref[rows, :]
            m2 = jnp.maximum(m2, jnp.concatenate([m_old[:, 0:1], m_old[:, HEAD_DIM:HEAD_DIM + 1]], axis=0))
        p = jnp.exp(s - m2)
        l2 = jnp.sum(p, axis=-1, keepdims=True)
        pv = _pair_select(jnp.dot(p.astype(BF16), vv, preferred_element_type=F32), lane_lo)
        m_new = _pair_select(m2, lane_lo)
        den = _pair_select(l2, lane_lo)
        if first_branch:
            acc = pv
        else:
            a = jnp.exp(m_old - m_new)
            acc = a * acc_ref[rows, :] + pv
            den = a * den_ref[rows, :] + den
        if last_branch:
            o_ref[rows, :] = (acc / den).astype(o_ref.dtype)
        else:
            acc_ref[rows, :] = acc
            m_ref[rows, :] = m_new
            den_ref[rows, :] = den

    def residue(q_ref, k_ref, v_ref, lanes, table, r, dil, jb_lo, jb_hi, first_branch, last_branch):
        def rows(jb):
            start = r + jb * (BLK * dil)
            return pl.ds(start, BLK, stride=dil) if dil > 1 else pl.ds(start, BLK)

        def first():
            block(q_ref[0:BLK, lanes], k_ref[0:BLK, lanes], v_ref[0:BLK, lanes],
                  bias_ref[table, :, BLK:2 * BLK], rows(0), first_branch, last_branch)

        def body(jb, carry):
            r0 = pl.multiple_of(jb * BLK, BLK)
            block(q_ref[pl.ds(r0, BLK), lanes], k_ref[pl.ds(r0 - BLK, 2 * BLK), lanes],
                  v_ref[pl.ds(r0 - BLK, 2 * BLK), lanes], bias_ref[table], rows(jb),
                  first_branch, last_branch)
            return carry

        if isinstance(jb_lo, int):
            assert jb_lo == 0
            first()
            if jb_hi > 1:
                lax.fori_loop(1, jb_hi, body, 0)
        else:
            lax.fori_loop(jb_lo, jb_hi, body, 0)

    spb = steps_per_branch
    d16 = A_PATTERNS[2][1]
    nb16 = seq // d16 // BLK
    res_per_step = d16 // spb

    @pl.when(step < spb)
    def _():
        for rr in range(res_per_step):
            residue(q16_ref, k16_ref, v16_ref, slice(rr * LANES, (rr + 1) * LANES), 2,
                    step * res_per_step + rr, d16, 0, nb16, True, False)

    d4 = A_PATTERNS[1][1]
    nb4 = seq // d4 // BLK

    @pl.when((step >= spb) & (step < 2 * spb))
    def _():
        residue(q4_ref, k4_ref, v4_ref, slice(0, LANES), 1, step - spb, d4, 0, nb4, False, False)

    nbq = seq // BLK // spb

    @pl.when(step == 2 * spb)
    def _():
        residue(q1_ref, k1_ref, v1_ref, slice(0, LANES), 0, 0, 1, 0, nbq, False, True)

    @pl.when(step > 2 * spb)
    def _():
        qt = step - 2 * spb
        residue(q1_ref, k1_ref, v1_ref, slice(0, LANES), 0, 0, 1, qt * nbq, (qt + 1) * nbq, False, True)


def _attn_a_call(p_slots, bias_a, batch, seq):
    n_slots = p_slots.shape[0]
    spb = 4
    dils = [d for _, d in A_PATTERNS]
    assert dils == [1, 4, 16] and dils[1] == spb and dils[2] % spb == 0
    assert seq % (dils[2] * BLK) == 0 and (seq // BLK) % spb == 0
    views = {d: p_slots.reshape(n_slots, batch, seq // d, d * LANES) for d in dils}
    res16 = dils[2] // spb

    def spec(d, slot0):
        if d == 16:
            return pl.BlockSpec((None, None, seq // 16, res16 * LANES),
                                lambda b, hp, s: (slot0 + hp, b, 0, jnp.minimum(s, spb - 1)))
        if d == 4:
            return pl.BlockSpec((None, None, seq // 4, LANES),
                                lambda b, hp, s: (slot0 + hp, b, 0, jnp.clip(s - spb, 0, spb - 1)))
        return pl.BlockSpec((None, None, seq, LANES), lambda b, hp, s: (slot0 + hp, b, 0, 0))

    in_specs, args = [], []
    for d in (16, 4, 1):
        for slot0 in (0, N_PAIRS, 2 * N_PAIRS):
            in_specs.append(spec(d, slot0))
            args.append(views[d])
    in_specs.append(pl.BlockSpec((len(A_PATTERNS), None, 2 * BLK, 2 * BLK), lambda b, hp, s: (0, hp, 0, 0)))
    args.append(bias_a)
    blocks = (3 * _nbytes((seq // 16, res16 * LANES), BF16) + 3 * _nbytes((seq // 4, LANES), BF16)
              + 3 * _nbytes((seq, LANES), BF16) + _nbytes((3, 2 * BLK, 2 * BLK), F32) + _nbytes((seq, LANES), BF16))
    scratch = 3 * _nbytes((seq, LANES), F32)
    return pl.pallas_call(
        functools.partial(_attn_a_kernel, seq=seq, steps_per_branch=spb),
        grid=(batch, N_PAIRS, 3 * spb),
        in_specs=in_specs,
        out_specs=pl.BlockSpec((None, seq, LANES), lambda b, hp, s: (hp, b, 0)),
        out_shape=jax.ShapeDtypeStruct((N_PAIRS, batch * seq, LANES), BF16),
        scratch_shapes=[pltpu.VMEM((seq, LANES), F32)] * 3,
        compiler_params=pltpu.CompilerParams(
            dimension_semantics=("arbitrary", "arbitrary", "arbitrary"),
            vmem_limit_bytes=_vmem_limit(blocks, scratch, 8 * _nbytes((2 * BLK, 2 * BLK), F32))),
        name="attn_dilated",
    )(*args)


def _attn_b_kernel(sink_ref, q_ref, k_ref, v_ref, bias_ref, o_ref, *, seq):
    hp = pl.program_id(1)
    lane_lo = lax.broadcasted_iota(jnp.int32, (BLK, LANES), 1) < HEAD_DIM
    row_lo = lax.broadcasted_iota(jnp.int32, (2 * BLK, 1), 0) < BLK
    sink2 = jnp.where(row_lo, sink_ref[2 * hp], sink_ref[2 * hp + 1])

    def block(q, kk, vv, bias, rows):
        s = _pair_scores(q, kk, bias, lane_lo)
        m2 = jnp.maximum(jnp.max(s, axis=-1, keepdims=True), sink2)
        p = jnp.exp(s - m2)
        l2 = jnp.sum(p, axis=-1, keepdims=True) + jnp.exp(sink2 - m2)
        pv = _pair_select(jnp.dot(p.astype(BF16), vv, preferred_element_type=F32), lane_lo)
        o_ref[rows, :] = (pv / _pair_select(l2, lane_lo)).astype(o_ref.dtype)

    block(q_ref[0:BLK, :], k_ref[0:BLK, :], v_ref[0:BLK, :], bias_ref[:, BLK:2 * BLK], pl.ds(0, BLK))

    def body(jb, carry):
        r0 = pl.multiple_of(jb * BLK, BLK)
        block(q_ref[pl.ds(r0, BLK), :], k_ref[pl.ds(r0 - BLK, 2 * BLK), :],
              v_ref[pl.ds(r0 - BLK, 2 * BLK), :], bias_ref[...], pl.ds(r0, BLK))
        return carry

    if seq // BLK > 1:
        lax.fori_loop(1, seq // BLK, body, 0)


def _attn_b_call(p_slots, bias_b, sinks, batch, seq):
    n_slots = p_slots.shape[0]
    view = p_slots.reshape(n_slots, batch, seq, LANES)
    q0, k0, v0 = 3 * N_PAIRS, 4 * N_PAIRS, 4 * N_PAIRS + N_KV_B
    pairs_per_kv = N_PAIRS // N_KV_B
    blk = (None, None, seq, LANES)
    blocks = 4 * _nbytes((seq, LANES), BF16) + _nbytes((2 * BLK, 2 * BLK), F32)
    return pl.pallas_call(
        functools.partial(_attn_b_kernel, seq=seq),
        grid=(batch, N_PAIRS),
        in_specs=[pl.BlockSpec(memory_space=pltpu.SMEM),
                  pl.BlockSpec(blk, lambda b, hp: (q0 + hp, b, 0, 0)),
                  pl.BlockSpec(blk, lambda b, hp: (k0 + hp // pairs_per_kv, b, 0, 0)),
                  pl.BlockSpec(blk, lambda b, hp: (v0 + hp // pairs_per_kv, b, 0, 0)),
                  pl.BlockSpec((None, 2 * BLK, 2 * BLK), lambda b, hp: (hp, 0, 0))],
        out_specs=pl.BlockSpec((None, seq, LANES), lambda b, hp: (hp, b, 0)),
        out_shape=jax.ShapeDtypeStruct((N_PAIRS, batch * seq, LANES), BF16),
        compiler_params=pltpu.CompilerParams(
            dimension_semantics=("arbitrary", "arbitrary"),
            vmem_limit_bytes=_vmem_limit(blocks, temp_bytes=8 * _nbytes((2 * BLK, 2 * BLK), F32))),
        name="attn_window",
    )(sinks, view, view, view, bias_b)


def _attn_s_kernel(q_ref, kn_ref, vn_ref, ak_ref, av_ref, bk_ref, bv_ref, bias_a_ref, bias_b_ref,
                   rel_ref, sink_ref, o_ref, *, bb):
    n_branches = bias_a_ref.shape[0]
    heads_per_kv = N_HEADS_B // N_KV_B
    bias0 = rel_ref[0:1, :]

    for i in range(bb):
        q = q_ref[i] * SCALE
        kn = kn_ref[i]
        vn = vn_ref[i]
        s0_all = jnp.sum(q * kn, axis=0, keepdims=True) + bias0
        cols = []
        for h in range(N_HEADS_A + N_HEADS_B):
            qh = q[:, h:h + 1]
            s0 = s0_all[:, h:h + 1]
            if h < N_HEADS_A:
                s = jnp.sum(ak_ref[i, h] * qh, axis=0, keepdims=True)
                ts = [s + bias_a_ref[t, h:h + 1, :] for t in range(n_branches)]
                m = s0
                for t_ in ts:
                    m = jnp.maximum(m, jnp.max(t_, axis=-1, keepdims=True))
                w = jnp.exp(ts[0] - m)
                for t_ in ts[1:]:
                    w = w + jnp.exp(t_ - m)
                p0 = n_branches * jnp.exp(s0 - m)
                den = jnp.sum(w, axis=-1, keepdims=True) + p0
                v = av_ref[i, h]
            else:
                hb = h - N_HEADS_A
                kv = hb // heads_per_kv
                sink = sink_ref[0:1, hb:hb + 1]
                t_ = jnp.sum(bk_ref[i, kv] * qh, axis=0, keepdims=True) + bias_b_ref[hb:hb + 1, :]
                m = jnp.maximum(jnp.maximum(s0, sink), jnp.max(t_, axis=-1, keepdims=True))
                w = jnp.exp(t_ - m)
                p0 = jnp.exp(s0 - m)
                den = jnp.sum(w, axis=-1, keepdims=True) + p0 + jnp.exp(sink - m)
                v = bv_ref[i, kv]
            acc = jnp.sum(v * w, axis=-1, keepdims=True) + p0 * vn[:, h:h + 1]
            cols.append(acc / den)
        o_ref[i] = jnp.concatenate(cols, axis=1)


def _attn_s_call(q3, kn3, vn3, cak_t, cav_t, cbk_t, cbv_t, bias_sa, bias_sb, rel_bias, sinks, bb):
    n, hd, n_heads = q3.shape
    a_blk = (bb,) + cak_t.shape[1:]
    b_blk = (bb,) + cbk_t.shape[1:]
    col_spec = pl.BlockSpec((bb, hd, n_heads), lambda i: (i, 0, 0))

    def whole(a):
        return pl.BlockSpec(a.shape, lambda i: (0,) * a.ndim)

    blocks = (2 * _nbytes(a_blk, F32) + 2 * _nbytes(b_blk, F32) + 4 * _nbytes((bb, hd, LANES), F32)
              + _nbytes(bias_sa.shape, F32))
    return pl.pallas_call(
        functools.partial(_attn_s_kernel, bb=bb),
        grid=(n // bb,),
        in_specs=[col_spec, col_spec, col_spec,
                  pl.BlockSpec(a_blk, lambda i: (i, 0, 0, 0)), pl.BlockSpec(a_blk, lambda i: (i, 0, 0, 0)),
                  pl.BlockSpec(b_blk, lambda i: (i, 0, 0, 0)), pl.BlockSpec(b_blk, lambda i: (i, 0, 0, 0)),
                  whole(bias_sa), whole(bias_sb), whole(rel_bias), whole(sinks)],
        out_specs=col_spec,
        out_shape=jax.ShapeDtypeStruct((n, hd, n_heads), F32),
        compiler_params=pltpu.CompilerParams(
            dimension_semantics=("arbitrary",),
            vmem_limit_bytes=_vmem_limit(blocks, temp_bytes=4 * _nbytes(cak_t.shape[2:], F32))),
        name="attn_cached",
    )(q3, kn3, vn3, cak_t, cav_t, cbk_t, cbv_t, bias_sa, bias_sb, rel_bias, sinks)


def _dup_cols(w):
    parts = []
    for j in range(2 * N_KV_B):
        col = w[:, j * HEAD_DIM:(j + 1) * HEAD_DIM]
        parts += [col, col]
    return jnp.concatenate(parts, axis=1)


def kernel(x_prompt, x_sample, cache_a_k, cache_a_v, cache_b_k, cache_b_v, c_prompt, c_sample, rel_bias, w_ada, b_ada, ffn1_wg, ffn1_wu, ffn1_wd, w_in, w_out, sinks, ffn2_wg, ffn2_wu, ffn2_wd, ln_g, ln_b):
    batch, seq, d = x_prompt.shape
    n_dec, dec_seq, _ = x_sample.shape
    depth = w_ada.shape[0]
    d_ff = ffn1_wg.shape[2]
    assert dec_seq == 1 and d == 2 * A_COLS and all(w // dl == BLK for w, dl in A_PATTERNS)
    alpha = (2 * depth) ** 0.25
    fc = 256
    tm = 512
    a_rows, b_rows = min(A_PATTERNS[-1][0], seq), min(B_WINDOW, seq)
    q_cols = 4 * A_COLS
    kvb = N_KV_B * HEAD_DIM
    heads_per_kv = N_HEADS_B // N_KV_B

    band_map, cached_a, cached_b = _bucket_maps(cache_a_k.shape[2], cache_b_k.shape[2])
    band_bias = _band_bias_call(rel_bias, band_map)
    bias_a, bias_b = band_bias[:len(A_PATTERNS)], band_bias[len(A_PATTERNS)]
    bias_sa, bias_sb = _cached_bias_call(rel_bias, cached_a, cached_b)

    pad = (-(n_dec + batch)) % 16
    c_all = jnp.concatenate([c_sample, c_prompt, jnp.zeros((pad, d), F32)], axis=0)
    xp = x_prompt.reshape(batch * seq, d)
    xs = x_sample.reshape(n_dec, d)
    outs = [[] for _ in range(8)]

    def head_cols(x, n_heads):
        return x.reshape(n_dec, n_heads, HEAD_DIM).transpose(0, 2, 1)

    for l in range(depth):
        mod2 = _ada_call(c_all, w_ada[l], b_ada[l])
        mod3 = mod2.reshape(mod2.shape[0] * N_MOD, 1, d)
        ffn1 = [w[l].astype(BF16) for w in (ffn1_wg, ffn1_wu, ffn1_wd)]
        ffn2 = [w[l].astype(BF16) for w in (ffn2_wg, ffn2_wu, ffn2_wd)]
        w_ext_p = jnp.concatenate([w_in[l][:, :q_cols], _dup_cols(w_in[l][:, q_cols:])], axis=1).astype(BF16)
        w_in_s = w_in[l].astype(BF16)
        wo = w_out[l].astype(BF16)

        x1 = _ffn_call(xp, mod2, mod3, (0, 1, 2), n_dec, seq, *ffn1, ln_g[l, 0], ln_b[l, 0], alpha, tm, fc)
        p_slots, pak, pav, pbk, pbv = _proj_prompt_call(x1, mod3, n_dec, seq, w_ext_p, a_rows, b_rows, tm)
        mix_a = _attn_a_call(p_slots, bias_a, batch, seq)
        mix_b = _attn_b_call(p_slots, bias_b, sinks[l], batch, seq)
        xp = _ffn_call(x1, mod2, mod3, (6, 7, 8), n_dec, seq, *ffn2, ln_g[l, 2], ln_b[l, 2], alpha, tm, fc,
                       mix=(mix_a, mix_b), w_out=wo, gate_k=5, ln1_g=ln_g[l, 1], ln1_b=ln_b[l, 1])

        s1 = _ffn_call(xs, mod2, mod3, (0, 1, 2), 0, 1, *ffn1, ln_g[l, 0], ln_b[l, 0], alpha, n_dec, fc)
        proj_s = _proj_sample_call(s1, mod2, w_in_s)
        c = A_COLS
        sak, sav = proj_s[:, c:2 * c], proj_s[:, 2 * c:3 * c]
        sbk, sbv = proj_s[:, q_cols:q_cols + kvb], proj_s[:, q_cols + kvb:q_cols + 2 * kvb]
        q3 = jnp.concatenate([head_cols(proj_s[:, :c], N_HEADS_A),
                              head_cols(proj_s[:, 3 * c:q_cols], N_HEADS_B)], axis=2)
        kn3 = jnp.concatenate([head_cols(sak, N_HEADS_A),
                               jnp.repeat(head_cols(sbk, N_KV_B), heads_per_kv, axis=2)], axis=2)
        vn3 = jnp.concatenate([head_cols(sav, N_HEADS_A),
                               jnp.repeat(head_cols(sbv, N_KV_B), heads_per_kv, axis=2)], axis=2)
        to_cols = (0, 2, 3, 1)
        mix3 = _attn_s_call(q3, kn3, vn3, cache_a_k[l].transpose(to_cols), cache_a_v[l].transpose(to_cols),
                            cache_b_k[l].transpose(to_cols), cache_b_v[l].transpose(to_cols),
                            bias_sa, bias_sb, rel_bias, sinks[l].reshape(1, N_HEADS_B), 2)
        mix_s = mix3.transpose(0, 2, 1).reshape(n_dec, d)
        xs = _ffn_call(s1, mod2, mod3, (6, 7, 8), 0, 1, *ffn2, ln_g[l, 2], ln_b[l, 2], alpha, n_dec, fc,
                       mix=mix_s, w_out=wo, gate_k=5, ln1_g=ln_g[l, 1], ln1_b=ln_b[l, 1])

        to_rows = (0, 3, 1, 2)
        new = [pak.transpose(to_rows), pav.transpose(to_rows), pbk.transpose(to_rows), pbv.transpose(to_rows),
               sak.reshape(n_dec, 1, N_HEADS_A, HEAD_DIM), sav.reshape(n_dec, 1, N_HEADS_A, HEAD_DIM),
               sbk.reshape(n_dec, 1, N_KV_B, HEAD_DIM), sbv.reshape(n_dec, 1, N_KV_B, HEAD_DIM)]
        for acc, arr in zip(outs, new):
            acc.append(arr)

    return (xp.reshape(batch, seq, d), xs.reshape(n_dec, 1, d)) + tuple(jnp.stack(o) for o in outs)
```

```python
import functools
import math

import jax
import jax.numpy as jnp
from jax import lax
from jax.experimental import pallas as pl
from jax.experimental.pallas import tpu as pltpu

HEAD_DIM = 64
N_HEADS_A = 8
N_HEADS_B = 8
N_KV_B = 2
A_PATTERNS = ((128, 1), (512, 4), (2048, 16))
B_WINDOW = 128
BLK = 128
NUM_BUCKETS = 32
MAX_DISTANCE = 2048
LN_EPS = 1e-5
NEG = -1e30
SCALE = HEAD_DIM ** -0.5

LANES = 128
V7X_VMEM_BYTES = 64 * 1024 * 1024
A_COLS = N_HEADS_A * HEAD_DIM
N_PAIRS = A_COLS // LANES
N_MOD = 9

F32 = jnp.float32
BF16 = jnp.bfloat16


def _vmem_limit(pipelined_bytes, scratch_bytes=0, temp_bytes=0):
    want = 2 * pipelined_bytes + scratch_bytes + temp_bytes + (4 << 20)
    return int(min(want, V7X_VMEM_BYTES - (6 << 20)))


def _resident_spec(shape):
    return pl.BlockSpec(shape, lambda *_: (0,) * len(shape), pipeline_mode=pl.Buffered(1))


def _nbytes(shape, dtype):
    return math.prod(shape) * jnp.dtype(dtype).itemsize


def _layer_norm(x, g, b):
    mu = jnp.mean(x, axis=-1, keepdims=True)
    xc = x - mu
    var = jnp.mean(xc * xc, axis=-1, keepdims=True)
    return xc * lax.rsqrt(var + LN_EPS) * g + b


def _silu(x):
    return x * jax.nn.sigmoid(x)


def _ada_kernel(c_ref, w_ref, b_ref, o_ref):
    a = _silu(c_ref[...]).astype(BF16)
    o_ref[...] = jnp.dot(a, w_ref[...].astype(BF16), preferred_element_type=F32) + b_ref[...]


def _ada_call(c_all, w_ada, b_ada):
    m, d = c_all.shape
    n = w_ada.shape[1]
    tn = n // 8
    blocks = _nbytes((m, d), F32) + _nbytes((d, tn), F32) + _nbytes((m, tn), F32)
    return pl.pallas_call(
        _ada_kernel,
        grid=(n // tn,),
        in_specs=[pl.BlockSpec((m, d), lambda j: (0, 0)),
                  pl.BlockSpec((d, tn), lambda j: (0, j)),
                  pl.BlockSpec((1, tn), lambda j: (0, j))],
        out_specs=pl.BlockSpec((m, tn), lambda j: (0, j)),
        out_shape=jax.ShapeDtypeStruct((m, n), F32),
        compiler_params=pltpu.CompilerParams(
            dimension_semantics=("arbitrary",),
            vmem_limit_bytes=_vmem_limit(blocks, temp_bytes=_nbytes((d, tn), BF16))),
        name="ada_mod",
    )(c_all, w_ada, b_ada.reshape(1, n))


def _t5_bucket(n):
    max_exact = NUM_BUCKETS // 2
    nf = jnp.maximum(n, 1).astype(F32)
    large = max_exact + (jnp.log(nf / max_exact) / math.log(MAX_DISTANCE / max_exact)
                         * (NUM_BUCKETS - max_exact)).astype(jnp.int32)
    return jnp.where(n < max_exact, n, jnp.minimum(large, NUM_BUCKETS - 1))


def _bucket_maps(a_rows, b_rows):
    steps = [d for _, d in A_PATTERNS] + [1]
    q = jnp.arange(BLK)[:, None]
    k = jnp.arange(2 * BLK)[None, :]
    dist = q + BLK - k
    valid = (dist >= 0) & (dist <= BLK)
    band = jnp.stack([jnp.where(valid, _t5_bucket(jnp.maximum(dist, 0) * s), -1) for s in steps])

    def cached(rows, step):
        back = rows - jnp.arange(rows)
        hit = (back % step == 0) & (back // step <= BLK)
        return jnp.where(hit, _t5_bucket(back), -1)[None, :].astype(jnp.int32)

    cached_a = jnp.stack([cached(a_rows, s) for s in steps[:-1]])
    return band.astype(jnp.int32), cached_a, cached(b_rows, 1)


def _band_bias_kernel(rel_ref, bmap_ref, o_ref):
    t = pl.program_id(0)
    hp = pl.program_id(1)
    bm = bmap_ref[...]
    head0 = jnp.where(t == len(A_PATTERNS), N_HEADS_A, 0) + 2 * hp
    for half in range(2):
        acc = jnp.full(bm.shape, NEG, F32)
        for b in range(NUM_BUCKETS):
            acc = jnp.where(bm == b, rel_ref[b, head0 + half], acc)
        o_ref[half * BLK:(half + 1) * BLK, :] = acc


def _band_bias_call(rel_bias, band_map):
    n_tab = band_map.shape[0]
    return pl.pallas_call(
        _band_bias_kernel,
        grid=(n_tab, N_PAIRS),
        in_specs=[pl.BlockSpec(memory_space=pltpu.SMEM),
                  pl.BlockSpec((None, BLK, 2 * BLK), lambda t, hp: (t, 0, 0))],
        out_specs=pl.BlockSpec((None, None, 2 * BLK, 2 * BLK), lambda t, hp: (t, hp, 0, 0)),
        out_shape=jax.ShapeDtypeStruct((n_tab, N_PAIRS, 2 * BLK, 2 * BLK), F32),
        compiler_params=pltpu.CompilerParams(dimension_semantics=("arbitrary", "arbitrary")),
        name="band_bias",
    )(rel_bias, band_map)


def _cached_bias_kernel(relt_ref, amap_ref, bmap_ref, oa_ref, ob_ref):
    relt = relt_ref[...]

    def table(bm, heads):
        acc = jnp.full(bm.shape, NEG, F32)
        for b in range(NUM_BUCKETS):
            acc = jnp.where(bm == b, heads[:, b:b + 1], acc)
        return acc

    for t in range(amap_ref.shape[0]):
        oa_ref[t] = table(jnp.broadcast_to(amap_ref[t], oa_ref.shape[1:]), relt[:N_HEADS_A])
    ob_ref[...] = table(jnp.broadcast_to(bmap_ref[...], ob_ref.shape), relt[N_HEADS_A:])


def _cached_bias_call(rel_bias, cached_a, cached_b):
    return pl.pallas_call(
        _cached_bias_kernel,
        out_shape=(jax.ShapeDtypeStruct((cached_a.shape[0], N_HEADS_A, cached_a.shape[2]), F32),
                   jax.ShapeDtypeStruct((N_HEADS_B, cached_b.shape[1]), F32)),
        name="cached_bias",
    )(rel_bias.T, cached_a, cached_b)


def _ffn_kernel(*refs, alpha, pre_mix, mix_slots, n_chunks, fc):
    refs = list(refs)
    x_ref = refs.pop(0)
    if pre_mix:
        mix_refs = [refs.pop(0) for _ in range(2 if mix_slots else 1)]
        wo_ref, gm_ref, ln1g_ref, ln1b_ref = (refs.pop(0) for _ in range(4))
    sh_ref, sc_ref, gt_ref, wg_ref, wu_ref, wd_ref, lng_ref, lnb_ref, o_ref, h_ref, acc_ref = refs

    x = x_ref[...]
    if pre_mix:
        if mix_slots:
            mixed = jnp.concatenate([r[j] for r in mix_refs for j in range(N_PAIRS)], axis=-1)
        else:
            mixed = mix_refs[0][...].astype(BF16)
        y = jnp.dot(mixed, wo_ref[...], preferred_element_type=F32)
        x = _layer_norm(alpha * x + gm_ref[...] * y, ln1g_ref[...], ln1b_ref[...])
    h_ref[...] = (x * (1.0 + sc_ref[...]) + sh_ref[...]).astype(BF16)
    for c in range(n_chunks):
        cols = slice(c * fc, (c + 1) * fc)
        g = jnp.dot(h_ref[...], wg_ref[:, cols], preferred_element_type=F32)
        u = jnp.dot(h_ref[...], wu_ref[:, cols], preferred_element_type=F32)
        y = jnp.dot((_silu(g) * u).astype(BF16), wd_ref[cols, :], preferred_element_type=F32)
        if c == 0:
            acc_ref[...] = y
        else:
            acc_ref[...] += y
    o_ref[...] = _layer_norm(alpha * x + 0.5 * gt_ref[...] * acc_ref[...], lng_ref[...], lnb_ref[...])


def _ffn_call(x, mod2, mod3, mod_ks, seq0, rows_per_seq, wg, wu, wd, ln_g, ln_b, alpha, tm, fc,
              mix=None, w_out=None, gate_k=None, ln1_g=None, ln1_b=None):
    n, d = x.shape
    d_ff = wg.shape[1]
    n_chunks = d_ff // fc
    assert n_chunks * fc == d_ff
    per_row = rows_per_seq == 1
    tiles_per_seq = max(rows_per_seq // tm, 1)

    def mod_spec(k):
        if per_row:
            return pl.BlockSpec((tm, d), lambda i, k=k: (seq0 // tm + i, k))
        return pl.BlockSpec((None, 1, d), lambda i, k=k: ((seq0 + i // tiles_per_seq) * N_MOD + k, 0, 0))

    mod_arr = mod2 if per_row else mod3
    row_spec = pl.BlockSpec((tm, d), lambda i: (i, 0))
    vec_spec = pl.BlockSpec((1, d), lambda i: (0, 0))
    args, specs = [x], [row_spec]
    pre_mix = mix is not None
    mix_slots = pre_mix and isinstance(mix, tuple)
    blocks = 2 * _nbytes((tm, d), F32) + (3 * _nbytes((tm, d), F32) if per_row else 0)
    resident = 3 * _nbytes(wg.shape, BF16)
    if pre_mix:
        if mix_slots:
            for mm in mix:
                args.append(mm)
                specs.append(pl.BlockSpec((N_PAIRS, tm, LANES), lambda i: (0, i, 0)))
        else:
            args.append(mix)
            specs.append(row_spec)
        args += [w_out, mod_arr, ln1_g.reshape(1, d), ln1_b.reshape(1, d)]
        specs += [_resident_spec((d, d)), mod_spec(gate_k), vec_spec, vec_spec]
        blocks += _nbytes((tm, d), F32)
        resident += _nbytes((d, d), BF16)
    args += [mod_arr, mod_arr, mod_arr, wg, wu, wd, ln_g.reshape(1, d), ln_b.reshape(1, d)]
    specs += [mod_spec(mod_ks[0]), mod_spec(mod_ks[1]), mod_spec(mod_ks[2]),
              _resident_spec(wg.shape), _resident_spec(wu.shape), _resident_spec(wd.shape), vec_spec, vec_spec]
    scratch = _nbytes((tm, d), BF16) + _nbytes((tm, d), F32) + resident
    temps = 6 * _nbytes((tm, fc), F32) + 3 * _nbytes((tm, d), F32)
    return pl.pallas_call(
        functools.partial(_ffn_kernel, alpha=alpha, pre_mix=pre_mix, mix_slots=mix_slots,
                          n_chunks=n_chunks, fc=fc),
        grid=(n // tm,),
        in_specs=specs,
        out_specs=row_spec,
        out_shape=jax.ShapeDtypeStruct((n, d), F32),
        scratch_shapes=[pltpu.VMEM((tm, d), BF16), pltpu.VMEM((tm, d), F32)],
        compiler_params=pltpu.CompilerParams(
            dimension_semantics=("arbitrary",),
            vmem_limit_bytes=_vmem_limit(blocks, scratch, temps)),
        name="ffn_mix" if pre_mix else "ffn",
    )(*args)


def _proj_prompt_kernel(x_ref, sh_ref, sc_ref, w_ref, p_ref, p4_ref, p16_ref, ak_ref, av_ref, bk_ref, bv_ref,
                        stage_ref, *, tm, tiles_per_seq, a_tiles, b_rows):
    t = pl.program_id(0) % tiles_per_seq
    h = (x_ref[...] * (1.0 + sc_ref[...]) + sh_ref[...]).astype(BF16)
    n_groups = w_ref.shape[1] // A_COLS
    lane_lo = lax.broadcasted_iota(jnp.int32, (b_rows, LANES), 1) < HEAD_DIM
    for g in range(n_groups):
        res = jnp.dot(h, w_ref[:, g * A_COLS:(g + 1) * A_COLS], preferred_element_type=F32)
        scaled = res * SCALE if g in (0, 3) else res
        for s in range(N_PAIRS):
            slab = scaled[:, s * LANES:(s + 1) * LANES]
            p_ref[g * N_PAIRS + s] = slab.astype(BF16)
            if g < 3:
                slot = g * N_PAIRS + s
                stage_ref[slot] = slab
                for dil_ref in (p4_ref, p16_ref):
                    dil = dil_ref.shape[1]
                    for r in range(dil):
                        dil_ref[slot, r] = stage_ref[slot, pl.ds(r, tm // dil, stride=dil), :].astype(BF16)
        if g in (1, 2):
            out_ref = ak_ref if g == 1 else av_ref

            @pl.when(t >= tiles_per_seq - a_tiles)
            def _(res=res, out_ref=out_ref):
                out_ref[...] = res.T.reshape(out_ref.shape)
        if g == 4:
            @pl.when(t == tiles_per_seq - 1)
            def _(res=res):
                tail = res[tm - b_rows:, :]
                kb = jnp.where(lane_lo, tail[:, 0:LANES], tail[:, LANES:2 * LANES])
                vb = jnp.where(lane_lo, tail[:, 2 * LANES:3 * LANES], tail[:, 3 * LANES:4 * LANES])
                bk_ref[...] = kb.T.reshape(bk_ref.shape)
                bv_ref[...] = vb.T.reshape(bv_ref.shape)


def _proj_prompt_call(x, mod3, seq0, seq, w_ext, a_rows, b_rows, tm):
    n, d = x.shape
    batch = n // seq
    tiles_per_seq = seq // tm
    a_tiles = a_rows // tm
    n_slots = w_ext.shape[1] // LANES

    def mod_spec(k):
        return pl.BlockSpec((None, 1, d), lambda i: ((seq0 + i // tiles_per_seq) * N_MOD + k, 0, 0))

    def a_map(i):
        return (i // tiles_per_seq, 0, 0, jnp.maximum(i % tiles_per_seq - (tiles_per_seq - a_tiles), 0))

    a_blk = (None, N_HEADS_A, HEAD_DIM, tm)
    b_blk = (None, N_KV_B, HEAD_DIM, b_rows)
    a_slots = 3 * N_PAIRS
    d4, d16 = A_PATTERNS[1][1], A_PATTERNS[2][1]
    assert tm % (16 * d16) == 0

    def dil_spec(dil):
        return pl.BlockSpec((a_slots, None, dil, tm // dil, LANES),
                            lambda i: (0, i // tiles_per_seq, 0, i % tiles_per_seq, 0))

    blocks = (_nbytes((tm, d), F32) + _nbytes(w_ext.shape, BF16) + _nbytes((n_slots + 2 * a_slots, tm, LANES), BF16)
              + 2 * _nbytes((tm, A_COLS), F32) + 2 * _nbytes((b_rows, LANES), F32))
    return pl.pallas_call(
        functools.partial(_proj_prompt_kernel, tm=tm, tiles_per_seq=tiles_per_seq, a_tiles=a_tiles,
                          b_rows=b_rows),
        grid=(n // tm,),
        in_specs=[pl.BlockSpec((tm, d), lambda i: (i, 0)), mod_spec(3), mod_spec(4),
                  pl.BlockSpec(w_ext.shape, lambda i: (0, 0))],
        out_specs=[pl.BlockSpec((n_slots, tm, LANES), lambda i: (0, i, 0)),
                   dil_spec(d4), dil_spec(d16),
                   pl.BlockSpec(a_blk, a_map),
                   pl.BlockSpec(a_blk, a_map),
                   pl.BlockSpec(b_blk, lambda i: (i // tiles_per_seq, 0, 0, 0)),
                   pl.BlockSpec(b_blk, lambda i: (i // tiles_per_seq, 0, 0, 0))],
        out_shape=[jax.ShapeDtypeStruct((n_slots, n, LANES), BF16),
                   jax.ShapeDtypeStruct((a_slots, batch, d4, seq // d4, LANES), BF16),
                   jax.ShapeDtypeStruct((a_slots, batch, d16, seq // d16, LANES), BF16),
                   jax.ShapeDtypeStruct((batch, N_HEADS_A, HEAD_DIM, a_rows), F32),
                   jax.ShapeDtypeStruct((batch, N_HEADS_A, HEAD_DIM, a_rows), F32),
                   jax.ShapeDtypeStruct((batch, N_KV_B, HEAD_DIM, b_rows), F32),
                   jax.ShapeDtypeStruct((batch, N_KV_B, HEAD_DIM, b_rows), F32)],
        scratch_shapes=[pltpu.VMEM((a_slots, tm, LANES), F32)],
        compiler_params=pltpu.CompilerParams(
            dimension_semantics=("arbitrary",),
            vmem_limit_bytes=_vmem_limit(blocks, _nbytes((a_slots, tm, LANES), F32),
                                         4 * _nbytes((tm, A_COLS), F32))),
        name="proj_prompt",
    )(x, mod3, mod3, w_ext)


def _proj_sample_kernel(x_ref, sh_ref, sc_ref, w_ref, o_ref):
    h = (x_ref[...] * (1.0 + sc_ref[...]) + sh_ref[...]).astype(BF16)
    o_ref[...] = jnp.dot(h, w_ref[...], preferred_element_type=F32)


def _proj_sample_call(x, mod2, w_ext):
    n, d = x.shape
    cols = w_ext.shape[1]
    blocks = 3 * _nbytes((n, d), F32) + _nbytes(w_ext.shape, BF16) + _nbytes((n, cols), F32)
    return pl.pallas_call(
        _proj_sample_kernel,
        grid=(1,),
        in_specs=[pl.BlockSpec((n, d), lambda i: (0, 0)),
                  pl.BlockSpec((n, d), lambda i: (0, 3)),
                  pl.BlockSpec((n, d), lambda i: (0, 4)),
                  pl.BlockSpec(w_ext.shape, lambda i: (0, 0))],
        out_specs=pl.BlockSpec((n, cols), lambda i: (0, 0)),
        out_shape=jax.ShapeDtypeStruct((n, cols), F32),
        compiler_params=pltpu.CompilerParams(
            dimension_semantics=("arbitrary",), vmem_limit_bytes=_vmem_limit(blocks)),
        name="proj_sample",
    )(x, mod2, mod2, w_ext)


def _pair_scores(q, kk, bias, lane_lo):
    zero = jnp.zeros_like(q)
    q2 = jnp.concatenate([jnp.where(lane_lo, q, zero), jnp.where(lane_lo, zero, q)], axis=0)
    s = lax.dot_general(q2, kk, (((1,), (1,)), ((), ())), preferred_element_type=F32)
    return s + bias


def _pair_select(a, lane_lo):
    return jnp.where(lane_lo, a[:BLK], a[BLK:])


GROUP = 8


def _band_sequence(q_ref, k_ref, v_ref, n_blocks, load_bias, block):
    def one(jb, first):
        if first:
            block(jb, q_ref[0:BLK, :], k_ref[0:BLK, :], v_ref[0:BLK, :], load_bias(True))
        else:
            r0 = jb * BLK
            if not isinstance(jb, int):
                r0 = pl.multiple_of(r0, BLK)
            block(jb, q_ref[pl.ds(r0, BLK), :], k_ref[pl.ds(r0 - BLK, 2 * BLK), :],
                  v_ref[pl.ds(r0 - BLK, 2 * BLK), :], load_bias(False))

    for jb in range(min(GROUP, n_blocks)):
        one(jb, jb == 0)
    if n_blocks > GROUP:
        assert n_blocks % GROUP == 0

        def group(g, carry):
            for j in range(GROUP):
                one(g * GROUP + j, False)
            return carry

        lax.fori_loop(1, n_blocks // GROUP, group, 0)


def _attn_a_kernel(q16_ref, k16_ref, v16_ref, q4_ref, k4_ref, v4_ref, q1_ref, k1_ref, v1_ref,
                   bias_ref, o_ref, acc16_ref, m16_ref, den16_ref, acc4_ref, m4_ref, den4_ref, *, seq):
    lane_lo = lax.broadcasted_iota(jnp.int32, (BLK, LANES), 1) < HEAD_DIM
    state16 = (acc16_ref, m16_ref, den16_ref)
    state4 = (acc4_ref, m4_ref, den4_ref)

    def update(q, kk, vv, bias, rows, old, new):
        s = _pair_scores(q, kk, bias, lane_lo)
        m2 = jnp.max(s, axis=-1, keepdims=True)
        if old is not None:
            m_old = old[1][rows, :]
            m2 = jnp.maximum(m2, jnp.concatenate([m_old[:, 0:1], m_old[:, HEAD_DIM:HEAD_DIM + 1]], axis=0))
        p = jnp.exp(s - m2)
        l2 = jnp.sum(p, axis=-1, keepdims=True)
        pv = _pair_select(jnp.dot(p.astype(BF16), vv, preferred_element_type=F32), lane_lo)
        m_new = _pair_select(m2, lane_lo)
        den = _pair_select(l2, lane_lo)
        if old is None:
            acc = pv
        else:
            a = jnp.exp(m_old - m_new)
            acc = a * old[0][rows, :] + pv
            den = a * old[2][rows, :] + den
        if new is None:
            o_ref[rows, :] = (acc / den).astype(o_ref.dtype)
        else:
            new[0][rows, :] = acc
            new[1][rows, :] = m_new
            new[2][rows, :] = den

    def branch(q_ref, k_ref, v_ref, table, dil, r, old, new):
        def block(jb, q, kk, vv, bias):
            start = r + jb * (BLK * dil)
            rows = pl.ds(start, BLK, stride=dil) if dil > 1 else pl.ds(start, BLK)
            update(q, kk, vv, bias, rows, old, new)

        def load_bias(first):
            return bias_ref[table, :, BLK:2 * BLK] if first else bias_ref[table]

        _band_sequence(q_ref, k_ref, v_ref, q_ref.shape[0] // BLK, load_bias, block)

    d16 = q16_ref.shape[0]
    per_group = max(GROUP // (q16_ref.shape[1] // BLK), 1)
    assert d16 % per_group == 0

    def group16(g, carry):
        for rr in range(per_group):
            r = g * per_group + rr
            branch(q16_ref.at[r], k16_ref.at[r], v16_ref.at[r], 2, d16, r, None, state16)
        return carry

    lax.fori_loop(0, d16 // per_group, group16, 0)

    d4 = q4_ref.shape[0]

    def residue4(r, carry):
        branch(q4_ref.at[r], k4_ref.at[r], v4_ref.at[r], 1, d4, r, state16, state4)
        return carry

    lax.fori_loop(0, d4, residue4, 0)
    branch(q1_ref, k1_ref, v1_ref, 0, 1, 0, state4, None)


def _attn_a_call(p_slots, p4, p16, bias_a, batch, seq):
    n_slots = p_slots.shape[0]
    dils = [d for _, d in A_PATTERNS]
    assert dils == [1, p4.shape[2], p16.shape[2]] and seq % (dils[2] * BLK) == 0
    view1 = p_slots.reshape(n_slots, batch, seq, LANES)

    in_specs, args = [], []
    for arr in (p16, p4, view1):
        for slot0 in (0, N_PAIRS, 2 * N_PAIRS):
            in_specs.append(pl.BlockSpec((None, None) + arr.shape[2:],
                                         lambda b, hp, slot0=slot0, nd=arr.ndim: (slot0 + hp, b) + (0,) * (nd - 2)))
            args.append(arr)
    in_specs.append(pl.BlockSpec((len(A_PATTERNS), None, 2 * BLK, 2 * BLK), lambda b, hp: (0, hp, 0, 0)))
    args.append(bias_a)
    blocks = 10 * _nbytes((seq, LANES), BF16) + _nbytes((3, 2 * BLK, 2 * BLK), F32)
    scratch = 6 * _nbytes((seq, LANES), F32)
    return pl.pallas_call(
        functools.partial(_attn_a_kernel, seq=seq),
        grid=(batch, N_PAIRS),
        in_specs=in_specs,
        out_specs=pl.BlockSpec((None, seq, LANES), lambda b, hp: (hp, b, 0)),
        out_shape=jax.ShapeDtypeStruct((N_PAIRS, batch * seq, LANES), BF16),
        scratch_shapes=[pltpu.VMEM((seq, LANES), F32)] * 6,
        compiler_params=pltpu.CompilerParams(
            dimension_semantics=("arbitrary", "arbitrary"),
            vmem_limit_bytes=_vmem_limit(blocks, scratch, 4 * GROUP * _nbytes((2 * BLK, 2 * BLK), F32))),
        name="attn_dilated",
    )(*args)


def _attn_b_kernel(sink_ref, q_ref, k_ref, v_ref, bias_ref, o_ref, *, seq):
    hp = pl.program_id(1)
    lane_lo = lax.broadcasted_iota(jnp.int32, (BLK, LANES), 1) < HEAD_DIM
    row_lo = lax.broadcasted_iota(jnp.int32, (2 * BLK, 1), 0) < BLK
    sink2 = jnp.where(row_lo, sink_ref[2 * hp], sink_ref[2 * hp + 1])

    def block(jb, q, kk, vv, bias):
        s = _pair_scores(q, kk, bias, lane_lo)
        m2 = jnp.maximum(jnp.max(s, axis=-1, keepdims=True), sink2)
        p = jnp.exp(s - m2)
        l2 = jnp.sum(p, axis=-1, keepdims=True) + jnp.exp(sink2 - m2)
        pv = _pair_select(jnp.dot(p.astype(BF16), vv, preferred_element_type=F32), lane_lo)
        r0 = jb * BLK if isinstance(jb, int) else pl.multiple_of(jb * BLK, BLK)
        o_ref[pl.ds(r0, BLK), :] = (pv / _pair_select(l2, lane_lo)).astype(o_ref.dtype)

    def load_bias(first):
        return bias_ref[:, BLK:2 * BLK] if first else bias_ref[...]

    _band_sequence(q_ref, k_ref, v_ref, seq // BLK, load_bias, block)


def _attn_b_call(p_slots, bias_b, sinks, batch, seq):
    n_slots = p_slots.shape[0]
    view = p_slots.reshape(n_slots, batch, seq, LANES)
    q0, k0, v0 = 3 * N_PAIRS, 4 * N_PAIRS, 4 * N_PAIRS + N_KV_B
    pairs_per_kv = N_PAIRS // N_KV_B
    blk = (None, None, seq, LANES)
    blocks = 4 * _nbytes((seq, LANES), BF16) + _nbytes((2 * BLK, 2 * BLK), F32)
    return pl.pallas_call(
        functools.partial(_attn_b_kernel, seq=seq),
        grid=(batch, N_PAIRS),
        in_specs=[pl.BlockSpec(memory_space=pltpu.SMEM),
                  pl.BlockSpec(blk, lambda b, hp: (q0 + hp, b, 0, 0)),
                  pl.BlockSpec(blk, lambda b, hp: (k0 + hp // pairs_per_kv, b, 0, 0)),
                  pl.BlockSpec(blk, lambda b, hp: (v0 + hp // pairs_per_kv, b, 0, 0)),
                  pl.BlockSpec((None, 2 * BLK, 2 * BLK), lambda b, hp: (hp, 0, 0))],
        out_specs=pl.BlockSpec((None, seq, LANES), lambda b, hp: (hp, b, 0)),
        out_shape=jax.ShapeDtypeStruct((N_PAIRS, batch * seq, LANES), BF16),
        compiler_params=pltpu.CompilerParams(
            dimension_semantics=("arbitrary", "arbitrary"),
            vmem_limit_bytes=_vmem_limit(blocks, temp_bytes=8 * _nbytes((2 * BLK, 2 * BLK), F32))),
        name="attn_window",
    )(sinks, view, view, view, bias_b)


def _attn_s_kernel(q_ref, qt_ref, knt_ref, vn_ref, ak_ref, av_ref, bk_ref, bv_ref, bias_a_ref, bias_b_ref,
                   relt_ref, sink_ref, o_ref, sa_ref, sb_ref, wa_ref, wb_ref, *, bb):
    n_branches = bias_a_ref.shape[0]
    heads_per_kv = N_HEADS_B // N_KV_B
    bias0 = relt_ref[:, 0:1]
    sink = sink_ref[...]

    for i in range(bb):
        q = q_ref[i] * SCALE
        vn = vn_ref[i]
        s0 = jnp.sum(qt_ref[i] * SCALE * knt_ref[i], axis=-1, keepdims=True) + bias0
        for h in range(N_HEADS_A):
            sa_ref[i, h:h + 1, :] = jnp.sum(ak_ref[i, h] * q[:, h:h + 1], axis=0, keepdims=True)
        for hb in range(N_HEADS_B):
            h = N_HEADS_A + hb
            sb_ref[i, hb:hb + 1, :] = jnp.sum(bk_ref[i, hb // heads_per_kv] * q[:, h:h + 1], axis=0,
                                              keepdims=True)

        s0a = s0[:N_HEADS_A]
        ts = [sa_ref[i] + bias_a_ref[t] for t in range(n_branches)]
        m = s0a
        for t_ in ts:
            m = jnp.maximum(m, jnp.max(t_, axis=-1, keepdims=True))
        w = jnp.exp(ts[0] - m)
        for t_ in ts[1:]:
            w = w + jnp.exp(t_ - m)
        p0a = n_branches * jnp.exp(s0a - m)
        inv = 1.0 / (jnp.sum(w, axis=-1, keepdims=True) + p0a)
        wa_ref[i] = w * inv
        p0a = p0a * inv

        s0b = s0[N_HEADS_A:]
        tb = sb_ref[i] + bias_b_ref[...]
        mb = jnp.maximum(jnp.maximum(s0b, sink), jnp.max(tb, axis=-1, keepdims=True))
        wb = jnp.exp(tb - mb)
        p0b = jnp.exp(s0b - mb)
        invb = 1.0 / (jnp.sum(wb, axis=-1, keepdims=True) + p0b + jnp.exp(sink - mb))
        wb_ref[i] = wb * invb
        p0b = p0b * invb

        cols = []
        for h in range(N_HEADS_A):
            acc = jnp.sum(av_ref[i, h] * wa_ref[i, h:h + 1, :], axis=-1, keepdims=True)
            cols.append(acc + p0a[h:h + 1, :] * vn[:, h:h + 1])
        for hb in range(N_HEADS_B):
            h = N_HEADS_A + hb
            acc = jnp.sum(bv_ref[i, hb // heads_per_kv] * wb_ref[i, hb:hb + 1, :], axis=-1, keepdims=True)
            cols.append(acc + p0b[hb:hb + 1, :] * vn[:, h:h + 1])
        o_ref[i] = jnp.concatenate(cols, axis=1)


def _attn_s_call(q3, qt3, knt3, vn3, cak_t, cav_t, cbk_t, cbv_t, bias_sa, bias_sb, rel_t, sinks, bb):
    n, hd, n_heads = q3.shape
    a_blk = (bb,) + cak_t.shape[1:]
    b_blk = (bb,) + cbk_t.shape[1:]
    col_spec = pl.BlockSpec((bb, hd, n_heads), lambda i: (i, 0, 0))
    row_spec = pl.BlockSpec((bb, n_heads, hd), lambda i: (i, 0, 0))

    def whole(a):
        return pl.BlockSpec(a.shape, lambda i: (0,) * a.ndim)

    sa_shape = (bb, N_HEADS_A, cak_t.shape[3])
    sb_shape = (bb, N_HEADS_B, cbk_t.shape[3])
    blocks = (2 * _nbytes(a_blk, F32) + 2 * _nbytes(b_blk, F32) + 5 * _nbytes((bb, hd, LANES), F32)
              + _nbytes(bias_sa.shape, F32))
    scratch = 2 * _nbytes(sa_shape, F32) + 2 * _nbytes(sb_shape, F32)
    return pl.pallas_call(
        functools.partial(_attn_s_kernel, bb=bb),
        grid=(n // bb,),
        in_specs=[col_spec, row_spec, row_spec, col_spec,
                  pl.BlockSpec(a_blk, lambda i: (i, 0, 0, 0)), pl.BlockSpec(a_blk, lambda i: (i, 0, 0, 0)),
                  pl.BlockSpec(b_blk, lambda i: (i, 0, 0, 0)), pl.BlockSpec(b_blk, lambda i: (i, 0, 0, 0)),
                  whole(bias_sa), whole(bias_sb), whole(rel_t), whole(sinks)],
        out_specs=col_spec,
        out_shape=jax.ShapeDtypeStruct((n, hd, n_heads), F32),
        scratch_shapes=[pltpu.VMEM(sa_shape, F32), pltpu.VMEM(sb_shape, F32),
                        pltpu.VMEM(sa_shape, F32), pltpu.VMEM(sb_shape, F32)],
        compiler_params=pltpu.CompilerParams(
            dimension_semantics=("arbitrary",),
            vmem_limit_bytes=_vmem_limit(blocks, scratch, 4 * _nbytes(cak_t.shape[2:], F32))),
        name="attn_cached",
    )(q3, qt3, knt3, vn3, cak_t, cav_t, cbk_t, cbv_t, bias_sa, bias_sb, rel_t, sinks)


def _dup_cols(w):
    parts = []
    for j in range(2 * N_KV_B):
        col = w[:, j * HEAD_DIM:(j + 1) * HEAD_DIM]
        parts += [col, col]
    return jnp.concatenate(parts, axis=1)


def kernel(x_prompt, x_sample, cache_a_k, cache_a_v, cache_b_k, cache_b_v, c_prompt, c_sample, rel_bias, w_ada, b_ada, ffn1_wg, ffn1_wu, ffn1_wd, w_in, w_out, sinks, ffn2_wg, ffn2_wu, ffn2_wd, ln_g, ln_b):
    batch, seq, d = x_prompt.shape
    n_dec, dec_seq, _ = x_sample.shape
    depth = w_ada.shape[0]
    d_ff = ffn1_wg.shape[2]
    assert dec_seq == 1 and d == 2 * A_COLS and all(w // dl == BLK for w, dl in A_PATTERNS)
    alpha = (2 * depth) ** 0.25
    fc = 256
    tm = 512
    a_rows, b_rows = min(A_PATTERNS[-1][0], seq), min(B_WINDOW, seq)
    q_cols = 4 * A_COLS
    kvb = N_KV_B * HEAD_DIM
    heads_per_kv = N_HEADS_B // N_KV_B

    band_map, cached_a, cached_b = _bucket_maps(cache_a_k.shape[2], cache_b_k.shape[2])
    band_bias = _band_bias_call(rel_bias, band_map)
    bias_a, bias_b = band_bias[:len(A_PATTERNS)], band_bias[len(A_PATTERNS)]
    bias_sa, bias_sb = _cached_bias_call(rel_bias, cached_a, cached_b)

    pad = (-(n_dec + batch)) % 16
    c_all = jnp.concatenate([c_sample, c_prompt, jnp.zeros((pad, d), F32)], axis=0)
    xp = x_prompt.reshape(batch * seq, d)
    xs = x_sample.reshape(n_dec, d)
    outs = [[] for _ in range(8)]

    def head_rows(x, n_heads):
        return x.reshape(n_dec, n_heads, HEAD_DIM)

    for l in range(depth):
        mod2 = _ada_call(c_all, w_ada[l], b_ada[l])
        mod3 = mod2.reshape(mod2.shape[0] * N_MOD, 1, d)
        ffn1 = [w[l].astype(BF16) for w in (ffn1_wg, ffn1_wu, ffn1_wd)]
        ffn2 = [w[l].astype(BF16) for w in (ffn2_wg, ffn2_wu, ffn2_wd)]
        w_ext_p = jnp.concatenate([w_in[l][:, :q_cols], _dup_cols(w_in[l][:, q_cols:])], axis=1).astype(BF16)
        w_in_s = w_in[l].astype(BF16)
        wo = w_out[l].astype(BF16)

        x1 = _ffn_call(xp, mod2, mod3, (0, 1, 2), n_dec, seq, *ffn1, ln_g[l, 0], ln_b[l, 0], alpha, tm, fc)
        p_slots, p4, p16, pak, pav, pbk, pbv = _proj_prompt_call(x1, mod3, n_dec, seq, w_ext_p, a_rows, b_rows, tm)
        mix_a = _attn_a_call(p_slots, p4, p16, bias_a, batch, seq)
        mix_b = _attn_b_call(p_slots, bias_b, sinks[l], batch, seq)
        xp = _ffn_call(x1, mod2, mod3, (6, 7, 8), n_dec, seq, *ffn2, ln_g[l, 2], ln_b[l, 2], alpha, tm, fc,
                       mix=(mix_a, mix_b), w_out=wo, gate_k=5, ln1_g=ln_g[l, 1], ln1_b=ln_b[l, 1])

        s1 = _ffn_call(xs, mod2, mod3, (0, 1, 2), 0, 1, *ffn1, ln_g[l, 0], ln_b[l, 0], alpha, n_dec, fc)
        proj_s = _proj_sample_call(s1, mod2, w_in_s)
        c = A_COLS
        sak, sav = proj_s[:, c:2 * c], proj_s[:, 2 * c:3 * c]
        sbk, sbv = proj_s[:, q_cols:q_cols + kvb], proj_s[:, q_cols + kvb:q_cols + 2 * kvb]
        qt3 = jnp.concatenate([head_rows(proj_s[:, :c], N_HEADS_A),
                               head_rows(proj_s[:, 3 * c:q_cols], N_HEADS_B)], axis=1)
        knt3 = jnp.concatenate([head_rows(sak, N_HEADS_A),
                                jnp.repeat(head_rows(sbk, N_KV_B), heads_per_kv, axis=1)], axis=1)
        vnt3 = jnp.concatenate([head_rows(sav, N_HEADS_A),
                                jnp.repeat(head_rows(sbv, N_KV_B), heads_per_kv, axis=1)], axis=1)
        to_cols = (0, 2, 3, 1)
        mix3 = _attn_s_call(qt3.transpose(0, 2, 1), qt3, knt3, vnt3.transpose(0, 2, 1),
                            cache_a_k[l].transpose(to_cols), cache_a_v[l].transpose(to_cols),
                            cache_b_k[l].transpose(to_cols), cache_b_v[l].transpose(to_cols),
                            bias_sa, bias_sb, rel_bias.T, sinks[l].reshape(N_HEADS_B, 1), 2)
        mix_s = mix3.transpose(0, 2, 1).reshape(n_dec, d)
        xs = _ffn_call(s1, mod2, mod3, (6, 7, 8), 0, 1, *ffn2, ln_g[l, 2], ln_b[l, 2], alpha, n_dec, fc,
                       mix=mix_s, w_out=wo, gate_k=5, ln1_g=ln_g[l, 1], ln1_b=ln_b[l, 1])

        to_rows = (0, 3, 1, 2)
        new = [pak.transpose(to_rows), pav.transpose(to_rows), pbk.transpose(to_rows), pbv.transpose(to_rows),
               sak.reshape(n_dec, 1, N_HEADS_A, HEAD_DIM), sav.reshape(n_dec, 1, N_HEADS_A, HEAD_DIM),
               sbk.reshape(n_dec, 1, N_KV_B, HEAD_DIM), sbv.reshape(n_dec, 1, N_KV_B, HEAD_DIM)]
        for acc, arr in zip(outs, new):
            acc.append(arr)

    return (xp.reshape(batch, seq, d), xs.reshape(n_dec, 1, d)) + tuple(jnp.stack(o) for o in outs)
```

```python
import functools
import math

import jax
import jax.numpy as jnp
from jax import lax
from jax.experimental import pallas as pl
from jax.experimental.pallas import tpu as pltpu

HEAD_DIM = 64
N_HEADS_A = 8
N_HEADS_B = 8
N_KV_B = 2
A_PATTERNS = ((128, 1), (512, 4), (2048, 16))
B_WINDOW = 128
BLK = 128
NUM_BUCKETS = 32
MAX_DISTANCE = 2048
LN_EPS = 1e-5
NEG = -1e30
SCALE = HEAD_DIM ** -0.5

LANES = 128
V7X_VMEM_BYTES = 64 * 1024 * 1024
A_COLS = N_HEADS_A * HEAD_DIM
N_PAIRS = A_COLS // LANES
N_MOD = 9

F32 = jnp.float32
BF16 = jnp.bfloat16


def _vmem_limit(pipelined_bytes, scratch_bytes=0, temp_bytes=0):
    want = 2 * pipelined_bytes + scratch_bytes + temp_bytes + (4 << 20)
    return int(min(want, V7X_VMEM_BYTES - (6 << 20)))


def _resident_spec(shape):
    return pl.BlockSpec(shape, lambda *_: (0,) * len(shape), pipeline_mode=pl.Buffered(1))


def _nbytes(shape, dtype):
    return math.prod(shape) * jnp.dtype(dtype).itemsize


def _layer_norm(x, g, b):
    mu = jnp.mean(x, axis=-1, keepdims=True)
    xc = x - mu
    var = jnp.mean(xc * xc, axis=-1, keepdims=True)
    return xc * lax.rsqrt(var + LN_EPS) * g + b


def _silu(x):
    return x * jax.nn.sigmoid(x)


def _ada_kernel(c_ref, w_ref, b_ref, o_ref):
    a = _silu(c_ref[...]).astype(BF16)
    o_ref[...] = jnp.dot(a, w_ref[...].astype(BF16), preferred_element_type=F32) + b_ref[...]


def _ada_call(c_all, w_ada, b_ada):
    m, d = c_all.shape
    n = w_ada.shape[1]
    tn = n // 8
    blocks = _nbytes((m, d), F32) + _nbytes((d, tn), F32) + _nbytes((m, tn), F32)
    return pl.pallas_call(
        _ada_kernel,
        grid=(n // tn,),
        in_specs=[pl.BlockSpec((m, d), lambda j: (0, 0)),
                  pl.BlockSpec((d, tn), lambda j: (0, j)),
                  pl.BlockSpec((1, tn), lambda j: (0, j))],
        out_specs=pl.BlockSpec((m, tn), lambda j: (0, j)),
        out_shape=jax.ShapeDtypeStruct((m, n), F32),
        compiler_params=pltpu.CompilerParams(
            dimension_semantics=("arbitrary",),
            vmem_limit_bytes=_vmem_limit(blocks, temp_bytes=_nbytes((d, tn), BF16))),
        name="ada_mod",
    )(c_all, w_ada, b_ada.reshape(1, n))


def _t5_bucket(n):
    max_exact = NUM_BUCKETS // 2
    nf = jnp.maximum(n, 1).astype(F32)
    large = max_exact + (jnp.log(nf / max_exact) / math.log(MAX_DISTANCE / max_exact)
                         * (NUM_BUCKETS - max_exact)).astype(jnp.int32)
    return jnp.where(n < max_exact, n, jnp.minimum(large, NUM_BUCKETS - 1))


def _bucket_maps(a_rows, b_rows):
    steps = [d for _, d in A_PATTERNS] + [1]
    q = jnp.arange(BLK)[:, None]
    k = jnp.arange(2 * BLK)[None, :]
    dist = q + BLK - k
    valid = (dist >= 0) & (dist <= BLK)
    band = jnp.stack([jnp.where(valid, _t5_bucket(jnp.maximum(dist, 0) * s), -1) for s in steps])

    def cached(rows, step):
        back = rows - jnp.arange(rows)
        hit = (back % step == 0) & (back // step <= BLK)
        return jnp.where(hit, _t5_bucket(back), -1)[None, :].astype(jnp.int32)

    cached_a = jnp.stack([cached(a_rows, s) for s in steps[:-1]])
    return band.astype(jnp.int32), cached_a, cached(b_rows, 1)


def _band_bias_kernel(rel_ref, bmap_ref, o_ref):
    t = pl.program_id(0)
    hp = pl.program_id(1)
    bm = bmap_ref[...]
    head0 = jnp.where(t == len(A_PATTERNS), N_HEADS_A, 0) + 2 * hp
    for half in range(2):
        acc = jnp.full(bm.shape, NEG, F32)
        for b in range(NUM_BUCKETS):
            acc = jnp.where(bm == b, rel_ref[b, head0 + half], acc)
        o_ref[half * BLK:(half + 1) * BLK, :] = acc


def _band_bias_call(rel_bias, band_map):
    n_tab = band_map.shape[0]
    return pl.pallas_call(
        _band_bias_kernel,
        grid=(n_tab, N_PAIRS),
        in_specs=[pl.BlockSpec(memory_space=pltpu.SMEM),
                  pl.BlockSpec((None, BLK, 2 * BLK), lambda t, hp: (t, 0, 0))],
        out_specs=pl.BlockSpec((None, None, 2 * BLK, 2 * BLK), lambda t, hp: (t, hp, 0, 0)),
        out_shape=jax.ShapeDtypeStruct((n_tab, N_PAIRS, 2 * BLK, 2 * BLK), F32),
        compiler_params=pltpu.CompilerParams(dimension_semantics=("arbitrary", "arbitrary")),
        name="band_bias",
    )(rel_bias, band_map)


def _cached_bias_kernel(relt_ref, amap_ref, bmap_ref, oa_ref, ob_ref):
    relt = relt_ref[...]

    def table(bm, heads):
        acc = jnp.full(bm.shape, NEG, F32)
        for b in range(NUM_BUCKETS):
            acc = jnp.where(bm == b, heads[:, b:b + 1], acc)
        return acc

    for t in range(amap_ref.shape[0]):
        oa_ref[t] = table(jnp.broadcast_to(amap_ref[t], oa_ref.shape[1:]), relt[:N_HEADS_A])
    ob_ref[...] = table(jnp.broadcast_to(bmap_ref[...], ob_ref.shape), relt[N_HEADS_A:])


def _cached_bias_call(rel_bias, cached_a, cached_b):
    return pl.pallas_call(
        _cached_bias_kernel,
        out_shape=(jax.ShapeDtypeStruct((cached_a.shape[0], N_HEADS_A, cached_a.shape[2]), F32),
                   jax.ShapeDtypeStruct((N_HEADS_B, cached_b.shape[1]), F32)),
        name="cached_bias",
    )(rel_bias.T, cached_a, cached_b)


def _ffn_kernel(*refs, alpha, pre_mix, mix_slots, n_chunks, fc):
    refs = list(refs)
    x_ref = refs.pop(0)
    if pre_mix:
        mix_refs = [refs.pop(0) for _ in range(2 if mix_slots else 1)]
        wo_ref, gm_ref, ln1g_ref, ln1b_ref = (refs.pop(0) for _ in range(4))
    sh_ref, sc_ref, gt_ref, wg_ref, wu_ref, wd_ref, lng_ref, lnb_ref, o_ref, h_ref, acc_ref = refs

    x = x_ref[...]
    if pre_mix:
        if mix_slots:
            mixed = jnp.concatenate([r[j] for r in mix_refs for j in range(N_PAIRS)], axis=-1)
        else:
            mixed = mix_refs[0][...].astype(BF16)
        y = jnp.dot(mixed, wo_ref[...], preferred_element_type=F32)
        x = _layer_norm(alpha * x + gm_ref[...] * y, ln1g_ref[...], ln1b_ref[...])
    h_ref[...] = (x * (1.0 + sc_ref[...]) + sh_ref[...]).astype(BF16)
    for c in range(n_chunks):
        cols = slice(c * fc, (c + 1) * fc)
        g = jnp.dot(h_ref[...], wg_ref[:, cols], preferred_element_type=F32)
        u = jnp.dot(h_ref[...], wu_ref[:, cols], preferred_element_type=F32)
        y = jnp.dot((_silu(g) * u).astype(BF16), wd_ref[cols, :], preferred_element_type=F32)
        if c == 0:
            acc_ref[...] = y
        else:
            acc_ref[...] += y
    o_ref[...] = _layer_norm(alpha * x + 0.5 * gt_ref[...] * acc_ref[...], lng_ref[...], lnb_ref[...])


def _ffn_call(x, mod2, mod3, mod_ks, seq0, rows_per_seq, wg, wu, wd, ln_g, ln_b, alpha, tm, fc,
              mix=None, w_out=None, gate_k=None, ln1_g=None, ln1_b=None):
    n, d = x.shape
    d_ff = wg.shape[1]
    n_chunks = d_ff // fc
    assert n_chunks * fc == d_ff
    per_row = rows_per_seq == 1
    tiles_per_seq = max(rows_per_seq // tm, 1)

    def mod_spec(k):
        if per_row:
            return pl.BlockSpec((tm, d), lambda i, k=k: (seq0 // tm + i, k))
        return pl.BlockSpec((None, 1, d), lambda i, k=k: ((seq0 + i // tiles_per_seq) * N_MOD + k, 0, 0))

    mod_arr = mod2 if per_row else mod3
    row_spec = pl.BlockSpec((tm, d), lambda i: (i, 0))
    vec_spec = pl.BlockSpec((1, d), lambda i: (0, 0))
    args, specs = [x], [row_spec]
    pre_mix = mix is not None
    mix_slots = pre_mix and isinstance(mix, tuple)
    blocks = 2 * _nbytes((tm, d), F32) + (3 * _nbytes((tm, d), F32) if per_row else 0)
    resident = 3 * _nbytes(wg.shape, BF16)
    if pre_mix:
        if mix_slots:
            for mm in mix:
                args.append(mm)
                specs.append(pl.BlockSpec((N_PAIRS, tm, LANES), lambda i: (0, i, 0)))
        else:
            args.append(mix)
            specs.append(row_spec)
        args += [w_out, mod_arr, ln1_g.reshape(1, d), ln1_b.reshape(1, d)]
        specs += [_resident_spec((d, d)), mod_spec(gate_k), vec_spec, vec_spec]
        blocks += _nbytes((tm, d), F32)
        resident += _nbytes((d, d), BF16)
    args += [mod_arr, mod_arr, mod_arr, wg, wu, wd, ln_g.reshape(1, d), ln_b.reshape(1, d)]
    specs += [mod_spec(mod_ks[0]), mod_spec(mod_ks[1]), mod_spec(mod_ks[2]),
              _resident_spec(wg.shape), _resident_spec(wu.shape), _resident_spec(wd.shape), vec_spec, vec_spec]
    scratch = _nbytes((tm, d), BF16) + _nbytes((tm, d), F32) + resident
    temps = 6 * _nbytes((tm, fc), F32) + 3 * _nbytes((tm, d), F32)
    return pl.pallas_call(
        functools.partial(_ffn_kernel, alpha=alpha, pre_mix=pre_mix, mix_slots=mix_slots,
                          n_chunks=n_chunks, fc=fc),
        grid=(n // tm,),
        in_specs=specs,
        out_specs=row_spec,
        out_shape=jax.ShapeDtypeStruct((n, d), F32),
        scratch_shapes=[pltpu.VMEM((tm, d), BF16), pltpu.VMEM((tm, d), F32)],
        compiler_params=pltpu.CompilerParams(
            dimension_semantics=("arbitrary",),
            vmem_limit_bytes=_vmem_limit(blocks, scratch, temps)),
        name="ffn_mix" if pre_mix else "ffn",
    )(*args)


def _proj_prompt_kernel(x_ref, sh_ref, sc_ref, w_ref, p_ref, p4_ref, p16_ref, ak_ref, av_ref, bk_ref, bv_ref,
                        stage_ref, stage4_ref, *, tm, tiles_per_seq, a_tiles, b_rows):
    t = pl.program_id(0) % tiles_per_seq
    h = (x_ref[...] * (1.0 + sc_ref[...]) + sh_ref[...]).astype(BF16)
    n_groups = w_ref.shape[1] // A_COLS
    lane_lo = lax.broadcasted_iota(jnp.int32, (b_rows, LANES), 1) < HEAD_DIM
    for g in range(n_groups):
        res = jnp.dot(h, w_ref[:, g * A_COLS:(g + 1) * A_COLS], preferred_element_type=F32)
        scaled = res * SCALE if g in (0, 3) else res
        for s in range(N_PAIRS):
            slab = scaled[:, s * LANES:(s + 1) * LANES]
            p_ref[g * N_PAIRS + s] = slab.astype(BF16)
            if g < 3:
                slot = g * N_PAIRS + s
                d4, d16 = p4_ref.shape[1], p16_ref.shape[1]
                stage_ref[slot] = slab
                for r in range(d4):
                    rows = stage_ref[slot, pl.ds(r, tm // d4, stride=d4), :]
                    p4_ref[slot, r] = rows.astype(BF16)
                    stage4_ref[slot, r] = rows
                for r in range(d16):
                    rows = stage4_ref[slot, r % d4, pl.ds(r // d4, tm // d16, stride=d16 // d4), :]
                    p16_ref[slot, r] = rows.astype(BF16)
        if g in (1, 2):
            out_ref = ak_ref if g == 1 else av_ref

            @pl.when(t >= tiles_per_seq - a_tiles)
            def _(res=res, out_ref=out_ref):
                out_ref[...] = res.T.reshape(out_ref.shape)
        if g == 4:
            @pl.when(t == tiles_per_seq - 1)
            def _(res=res):
                tail = res[tm - b_rows:, :]
                kb = jnp.where(lane_lo, tail[:, 0:LANES], tail[:, LANES:2 * LANES])
                vb = jnp.where(lane_lo, tail[:, 2 * LANES:3 * LANES], tail[:, 3 * LANES:4 * LANES])
                bk_ref[...] = kb.T.reshape(bk_ref.shape)
                bv_ref[...] = vb.T.reshape(bv_ref.shape)


def _proj_prompt_call(x, mod3, seq0, seq, w_ext, a_rows, b_rows, tm):
    n, d = x.shape
    batch = n // seq
    tiles_per_seq = seq // tm
    a_tiles = a_rows // tm
    n_slots = w_ext.shape[1] // LANES

    def mod_spec(k):
        return pl.BlockSpec((None, 1, d), lambda i: ((seq0 + i // tiles_per_seq) * N_MOD + k, 0, 0))

    def a_map(i):
        return (i // tiles_per_seq, 0, 0, jnp.maximum(i % tiles_per_seq - (tiles_per_seq - a_tiles), 0))

    a_blk = (None, N_HEADS_A, HEAD_DIM, tm)
    b_blk = (None, N_KV_B, HEAD_DIM, b_rows)
    a_slots = 3 * N_PAIRS
    d4, d16 = A_PATTERNS[1][1], A_PATTERNS[2][1]
    assert tm % (16 * d16) == 0

    def dil_spec(dil):
        return pl.BlockSpec((a_slots, None, dil, tm // dil, LANES),
                            lambda i: (0, i // tiles_per_seq, 0, i % tiles_per_seq, 0))

    blocks = (_nbytes((tm, d), F32) + _nbytes(w_ext.shape, BF16) + _nbytes((n_slots + 2 * a_slots, tm, LANES), BF16)
              + 2 * _nbytes((tm, A_COLS), F32) + 2 * _nbytes((b_rows, LANES), F32))
    return pl.pallas_call(
        functools.partial(_proj_prompt_kernel, tm=tm, tiles_per_seq=tiles_per_seq, a_tiles=a_tiles,
                          b_rows=b_rows),
        grid=(n // tm,),
        in_specs=[pl.BlockSpec((tm, d), lambda i: (i, 0)), mod_spec(3), mod_spec(4),
                  pl.BlockSpec(w_ext.shape, lambda i: (0, 0))],
        out_specs=[pl.BlockSpec((n_slots, tm, LANES), lambda i: (0, i, 0)),
                   dil_spec(d4), dil_spec(d16),
                   pl.BlockSpec(a_blk, a_map),
                   pl.BlockSpec(a_blk, a_map),
                   pl.BlockSpec(b_blk, lambda i: (i // tiles_per_seq, 0, 0, 0)),
                   pl.BlockSpec(b_blk, lambda i: (i // tiles_per_seq, 0, 0, 0))],
        out_shape=[jax.ShapeDtypeStruct((n_slots, n, LANES), BF16),
                   jax.ShapeDtypeStruct((a_slots, batch, d4, seq // d4, LANES), BF16),
                   jax.ShapeDtypeStruct((a_slots, batch, d16, seq // d16, LANES), BF16),
                   jax.ShapeDtypeStruct((batch, N_HEADS_A, HEAD_DIM, a_rows), F32),
                   jax.ShapeDtypeStruct((batch, N_HEADS_A, HEAD_DIM, a_rows), F32),
                   jax.ShapeDtypeStruct((batch, N_KV_B, HEAD_DIM, b_rows), F32),
                   jax.ShapeDtypeStruct((batch, N_KV_B, HEAD_DIM, b_rows), F32)],
        scratch_shapes=[pltpu.VMEM((a_slots, tm, LANES), F32), pltpu.VMEM((a_slots, d4, tm // d4, LANES), F32)],
        compiler_params=pltpu.CompilerParams(
            dimension_semantics=("arbitrary",),
            vmem_limit_bytes=_vmem_limit(blocks, 2 * _nbytes((a_slots, tm, LANES), F32),
                                         4 * _nbytes((tm, A_COLS), F32))),
        name="proj_prompt",
    )(x, mod3, mod3, w_ext)


def _proj_sample_kernel(x_ref, sh_ref, sc_ref, w_ref, o_ref):
    h = (x_ref[...] * (1.0 + sc_ref[...]) + sh_ref[...]).astype(BF16)
    o_ref[...] = jnp.dot(h, w_ref[...], preferred_element_type=F32)


def _proj_sample_call(x, mod2, w_ext):
    n, d = x.shape
    cols = w_ext.shape[1]
    blocks = 3 * _nbytes((n, d), F32) + _nbytes(w_ext.shape, BF16) + _nbytes((n, cols), F32)
    return pl.pallas_call(
        _proj_sample_kernel,
        grid=(1,),
        in_specs=[pl.BlockSpec((n, d), lambda i: (0, 0)),
                  pl.BlockSpec((n, d), lambda i: (0, 3)),
                  pl.BlockSpec((n, d), lambda i: (0, 4)),
                  pl.BlockSpec(w_ext.shape, lambda i: (0, 0))],
        out_specs=pl.BlockSpec((n, cols), lambda i: (0, 0)),
        out_shape=jax.ShapeDtypeStruct((n, cols), F32),
        compiler_params=pltpu.CompilerParams(
            dimension_semantics=("arbitrary",), vmem_limit_bytes=_vmem_limit(blocks)),
        name="proj_sample",
    )(x, mod2, mod2, w_ext)


def _pair_scores(q, kk, bias, lane_lo):
    zero = jnp.zeros_like(q)
    q2 = jnp.concatenate([jnp.where(lane_lo, q, zero), jnp.where(lane_lo, zero, q)], axis=0)
    s = lax.dot_general(q2, kk, (((1,), (1,)), ((), ())), preferred_element_type=F32)
    return s + bias


def _pair_select(a, lane_lo):
    return jnp.where(lane_lo, a[:BLK], a[BLK:])


GROUP = 8


def _band_sequence(q_ref, k_ref, v_ref, n_blocks, load_bias, block, prefetch=lambda jb: None):
    def one(jb, first, pre):
        if first:
            block(jb, q_ref[0:BLK, :], k_ref[0:BLK, :], v_ref[0:BLK, :], load_bias(True), pre)
        else:
            r0 = jb * BLK
            if not isinstance(jb, int):
                r0 = pl.multiple_of(r0, BLK)
            block(jb, q_ref[pl.ds(r0, BLK), :], k_ref[pl.ds(r0 - BLK, 2 * BLK), :],
                  v_ref[pl.ds(r0 - BLK, 2 * BLK), :], load_bias(False), pre)

    def run(jbs, has_first):
        pres = [prefetch(jb) for jb in jbs]
        for j, (jb, pre) in enumerate(zip(jbs, pres)):
            one(jb, has_first and j == 0, pre)

    run(list(range(min(GROUP, n_blocks))), True)
    if n_blocks > GROUP:
        assert n_blocks % GROUP == 0

        def group(g, carry):
            run([g * GROUP + j for j in range(GROUP)], False)
            return carry

        lax.fori_loop(1, n_blocks // GROUP, group, 0)


def _attn_a_kernel(q16_ref, k16_ref, v16_ref, q4_ref, k4_ref, v4_ref, q1_ref, k1_ref, v1_ref,
                   bias_ref, o_ref, acc16_ref, m16_ref, den16_ref, acc4_ref, m4_ref, den4_ref, *, seq):
    lane_lo = lax.broadcasted_iota(jnp.int32, (BLK, LANES), 1) < HEAD_DIM
    state16 = (acc16_ref, m16_ref, den16_ref)
    state4 = (acc4_ref, m4_ref, den4_ref)

    def update(q, kk, vv, bias, rows, old, new):
        s = _pair_scores(q, kk, bias, lane_lo)
        part = s[:, :LANES] if s.shape[1] == LANES else jnp.maximum(s[:, :LANES], s[:, LANES:])
        if old is not None:
            acc_old, m_old, den_old = old
            part = jnp.maximum(part, jnp.concatenate([jnp.where(lane_lo, m_old, NEG),
                                                      jnp.where(lane_lo, NEG, m_old)], axis=0))
        m2 = jnp.max(part, axis=-1, keepdims=True)
        p = jnp.exp(s - m2)
        l2 = jnp.sum(p, axis=-1, keepdims=True)
        pv = _pair_select(jnp.dot(p.astype(BF16), vv, preferred_element_type=F32), lane_lo)
        m_new = _pair_select(m2, lane_lo)
        den = _pair_select(l2, lane_lo)
        if old is None:
            acc = pv
        else:
            a = jnp.exp(m_old - m_new)
            acc = a * acc_old + pv
            den = a * den_old + den
        if new is None:
            o_ref[rows, :] = (acc / den).astype(o_ref.dtype)
        else:
            new[0][rows, :] = acc
            new[1][rows, :] = m_new
            new[2][rows, :] = den

    def branch(q_ref, k_ref, v_ref, table, dil, r, old, new):
        def rows(jb):
            start = r + jb * (BLK * dil)
            return pl.ds(start, BLK, stride=dil) if dil > 1 else pl.ds(start, BLK)

        def prefetch(jb):
            return None if old is None else tuple(ref[rows(jb), :] for ref in old)

        def block(jb, q, kk, vv, bias, pre):
            update(q, kk, vv, bias, rows(jb), pre, new)

        def load_bias(first):
            return bias_ref[table, :, BLK:2 * BLK] if first else bias_ref[table]

        _band_sequence(q_ref, k_ref, v_ref, q_ref.shape[0] // BLK, load_bias, block, prefetch)

    d16 = q16_ref.shape[0]
    per_group = max(GROUP // (q16_ref.shape[1] // BLK), 1)
    assert d16 % per_group == 0

    def group16(g, carry):
        for rr in range(per_group):
            r = g * per_group + rr
            branch(q16_ref.at[r], k16_ref.at[r], v16_ref.at[r], 2, d16, r, None, state16)
        return carry

    lax.fori_loop(0, d16 // per_group, group16, 0)

    d4 = q4_ref.shape[0]

    def residue4(r, carry):
        branch(q4_ref.at[r], k4_ref.at[r], v4_ref.at[r], 1, d4, r, state16, state4)
        return carry

    lax.fori_loop(0, d4, residue4, 0)
    branch(q1_ref, k1_ref, v1_ref, 0, 1, 0, state4, None)


def _attn_a_call(p_slots, p4, p16, bias_a, batch, seq):
    n_slots = p_slots.shape[0]
    dils = [d for _, d in A_PATTERNS]
    assert dils == [1, p4.shape[2], p16.shape[2]] and seq % (dils[2] * BLK) == 0
    view1 = p_slots.reshape(n_slots, batch, seq, LANES)

    in_specs, args = [], []
    for arr in (p16, p4, view1):
        for slot0 in (0, N_PAIRS, 2 * N_PAIRS):
            in_specs.append(pl.BlockSpec((None, None) + arr.shape[2:],
                                         lambda b, hp, slot0=slot0, nd=arr.ndim: (slot0 + hp, b) + (0,) * (nd - 2)))
            args.append(arr)
    in_specs.append(pl.BlockSpec((len(A_PATTERNS), None, 2 * BLK, 2 * BLK), lambda b, hp: (0, hp, 0, 0)))
    args.append(bias_a)
    blocks = 10 * _nbytes((seq, LANES), BF16) + _nbytes((3, 2 * BLK, 2 * BLK), F32)
    scratch = 6 * _nbytes((seq, LANES), F32)
    return pl.pallas_call(
        functools.partial(_attn_a_kernel, seq=seq),
        grid=(batch, N_PAIRS),
        in_specs=in_specs,
        out_specs=pl.BlockSpec((None, seq, LANES), lambda b, hp: (hp, b, 0)),
        out_shape=jax.ShapeDtypeStruct((N_PAIRS, batch * seq, LANES), BF16),
        scratch_shapes=[pltpu.VMEM((seq, LANES), F32)] * 6,
        compiler_params=pltpu.CompilerParams(
            dimension_semantics=("arbitrary", "arbitrary"),
            vmem_limit_bytes=_vmem_limit(blocks, scratch, 4 * GROUP * _nbytes((2 * BLK, 2 * BLK), F32))),
        name="attn_dilated",
    )(*args)


def _attn_b_kernel(sink_ref, q_ref, k_ref, v_ref, bias_ref, o_ref, *, seq):
    hp = pl.program_id(1)
    lane_lo = lax.broadcasted_iota(jnp.int32, (BLK, LANES), 1) < HEAD_DIM
    row_lo = lax.broadcasted_iota(jnp.int32, (2 * BLK, 1), 0) < BLK
    sink2 = jnp.where(row_lo, sink_ref[2 * hp], sink_ref[2 * hp + 1])

    def block(jb, q, kk, vv, bias, pre):
        s = _pair_scores(q, kk, bias, lane_lo)
        m2 = jnp.maximum(jnp.max(s, axis=-1, keepdims=True), sink2)
        p = jnp.exp(s - m2)
        l2 = jnp.sum(p, axis=-1, keepdims=True) + jnp.exp(sink2 - m2)
        pv = _pair_select(jnp.dot(p.astype(BF16), vv, preferred_element_type=F32), lane_lo)
        r0 = jb * BLK if isinstance(jb, int) else pl.multiple_of(jb * BLK, BLK)
        o_ref[pl.ds(r0, BLK), :] = (pv / _pair_select(l2, lane_lo)).astype(o_ref.dtype)

    def load_bias(first):
        return bias_ref[:, BLK:2 * BLK] if first else bias_ref[...]

    _band_sequence(q_ref, k_ref, v_ref, seq // BLK, load_bias, block)


def _attn_b_call(p_slots, bias_b, sinks, batch, seq):
    n_slots = p_slots.shape[0]
    view = p_slots.reshape(n_slots, batch, seq, LANES)
    q0, k0, v0 = 3 * N_PAIRS, 4 * N_PAIRS, 4 * N_PAIRS + N_KV_B
    pairs_per_kv = N_PAIRS // N_KV_B
    blk = (None, None, seq, LANES)
    blocks = 4 * _nbytes((seq, LANES), BF16) + _nbytes((2 * BLK, 2 * BLK), F32)
    return pl.pallas_call(
        functools.partial(_attn_b_kernel, seq=seq),
        grid=(batch, N_PAIRS),
        in_specs=[pl.BlockSpec(memory_space=pltpu.SMEM),
                  pl.BlockSpec(blk, lambda b, hp: (q0 + hp, b, 0, 0)),
                  pl.BlockSpec(blk, lambda b, hp: (k0 + hp // pairs_per_kv, b, 0, 0)),
                  pl.BlockSpec(blk, lambda b, hp: (v0 + hp // pairs_per_kv, b, 0, 0)),
                  pl.BlockSpec((None, 2 * BLK, 2 * BLK), lambda b, hp: (hp, 0, 0))],
        out_specs=pl.BlockSpec((None, seq, LANES), lambda b, hp: (hp, b, 0)),
        out_shape=jax.ShapeDtypeStruct((N_PAIRS, batch * seq, LANES), BF16),
        compiler_params=pltpu.CompilerParams(
            dimension_semantics=("arbitrary", "arbitrary"),
            vmem_limit_bytes=_vmem_limit(blocks, temp_bytes=8 * _nbytes((2 * BLK, 2 * BLK), F32))),
        name="attn_window",
    )(sinks, view, view, view, bias_b)


def _attn_s_kernel(q_ref, qt_ref, knt_ref, vn_ref, ak_ref, av_ref, bk_ref, bv_ref, bias_a_ref, bias_b_ref,
                   relt_ref, sink_ref, o_ref, sa_ref, sb_ref, wa_ref, wb_ref, *, bb):
    n_branches = bias_a_ref.shape[0]
    heads_per_kv = N_HEADS_B // N_KV_B
    bias0 = relt_ref[:, 0:1]
    sink = sink_ref[...]

    for i in range(bb):
        q = q_ref[i] * SCALE
        vn = vn_ref[i]
        s0 = jnp.sum(qt_ref[i] * SCALE * knt_ref[i], axis=-1, keepdims=True) + bias0
        for h in range(N_HEADS_A):
            sa_ref[i, h:h + 1, :] = jnp.sum(ak_ref[i, h] * q[:, h:h + 1], axis=0, keepdims=True)
        for hb in range(N_HEADS_B):
            h = N_HEADS_A + hb
            sb_ref[i, hb:hb + 1, :] = jnp.sum(bk_ref[i, hb // heads_per_kv] * q[:, h:h + 1], axis=0,
                                              keepdims=True)

        s0a = s0[:N_HEADS_A]
        ts = [sa_ref[i] + bias_a_ref[t] for t in range(n_branches)]
        m = s0a
        for t_ in ts:
            m = jnp.maximum(m, jnp.max(t_, axis=-1, keepdims=True))
        w = jnp.exp(ts[0] - m)
        for t_ in ts[1:]:
            w = w + jnp.exp(t_ - m)
        p0a = n_branches * jnp.exp(s0a - m)
        inv = 1.0 / (jnp.sum(w, axis=-1, keepdims=True) + p0a)
        wa_ref[i] = w * inv
        p0a = p0a * inv

        s0b = s0[N_HEADS_A:]
        tb = sb_ref[i] + bias_b_ref[...]
        mb = jnp.maximum(jnp.maximum(s0b, sink), jnp.max(tb, axis=-1, keepdims=True))
        wb = jnp.exp(tb - mb)
        p0b = jnp.exp(s0b - mb)
        invb = 1.0 / (jnp.sum(wb, axis=-1, keepdims=True) + p0b + jnp.exp(sink - mb))
        wb_ref[i] = wb * invb
        p0b = p0b * invb

        cols = []
        for h in range(N_HEADS_A):
            acc = jnp.sum(av_ref[i, h] * wa_ref[i, h:h + 1, :], axis=-1, keepdims=True)
            cols.append(acc + p0a[h:h + 1, :] * vn[:, h:h + 1])
        for hb in range(N_HEADS_B):
            h = N_HEADS_A + hb
            acc = jnp.sum(bv_ref[i, hb // heads_per_kv] * wb_ref[i, hb:hb + 1, :], axis=-1, keepdims=True)
            cols.append(acc + p0b[hb:hb + 1, :] * vn[:, h:h + 1])
        o_ref[i] = jnp.concatenate(cols, axis=1)


def _attn_s_call(q3, qt3, knt3, vn3, cak_t, cav_t, cbk_t, cbv_t, bias_sa, bias_sb, rel_t, sinks, bb):
    n, hd, n_heads = q3.shape
    a_blk = (bb,) + cak_t.shape[1:]
    b_blk = (bb,) + cbk_t.shape[1:]
    col_spec = pl.BlockSpec((bb, hd, n_heads), lambda i: (i, 0, 0))
    row_spec = pl.BlockSpec((bb, n_heads, hd), lambda i: (i, 0, 0))

    def whole(a):
        return pl.BlockSpec(a.shape, lambda i: (0,) * a.ndim)

    sa_shape = (bb, N_HEADS_A, cak_t.shape[3])
    sb_shape = (bb, N_HEADS_B, cbk_t.shape[3])
    blocks = (2 * _nbytes(a_blk, F32) + 2 * _nbytes(b_blk, F32) + 5 * _nbytes((bb, hd, LANES), F32)
              + _nbytes(bias_sa.shape, F32))
    scratch = 2 * _nbytes(sa_shape, F32) + 2 * _nbytes(sb_shape, F32)
    return pl.pallas_call(
        functools.partial(_attn_s_kernel, bb=bb),
        grid=(n // bb,),
        in_specs=[col_spec, row_spec, row_spec, col_spec,
                  pl.BlockSpec(a_blk, lambda i: (i, 0, 0, 0)), pl.BlockSpec(a_blk, lambda i: (i, 0, 0, 0)),
                  pl.BlockSpec(b_blk, lambda i: (i, 0, 0, 0)), pl.BlockSpec(b_blk, lambda i: (i, 0, 0, 0)),
                  whole(bias_sa), whole(bias_sb), whole(rel_t), whole(sinks)],
        out_specs=col_spec,
        out_shape=jax.ShapeDtypeStruct((n, hd, n_heads), F32),
        scratch_shapes=[pltpu.VMEM(sa_shape, F32), pltpu.VMEM(sb_shape, F32),
                        pltpu.VMEM(sa_shape, F32), pltpu.VMEM(sb_shape, F32)],
        compiler_params=pltpu.CompilerParams(
            dimension_semantics=("arbitrary",),
            vmem_limit_bytes=_vmem_limit(blocks, scratch, 4 * _nbytes(cak_t.shape[2:], F32))),
        name="attn_cached",
    )(q3, qt3, knt3, vn3, cak_t, cav_t, cbk_t, cbv_t, bias_sa, bias_sb, rel_t, sinks)


def _dup_cols(w):
    parts = []
    for j in range(2 * N_KV_B):
        col = w[:, j * HEAD_DIM:(j + 1) * HEAD_DIM]
        parts += [col, col]
    return jnp.concatenate(parts, axis=1)


def kernel(x_prompt, x_sample, cache_a_k, cache_a_v, cache_b_k, cache_b_v, c_prompt, c_sample, rel_bias, w_ada, b_ada, ffn1_wg, ffn1_wu, ffn1_wd, w_in, w_out, sinks, ffn2_wg, ffn2_wu, ffn2_wd, ln_g, ln_b):
    batch, seq, d = x_prompt.shape
    n_dec, dec_seq, _ = x_sample.shape
    depth = w_ada.shape[0]
    d_ff = ffn1_wg.shape[2]
    assert dec_seq == 1 and d == 2 * A_COLS and all(w // dl == BLK for w, dl in A_PATTERNS)
    alpha = (2 * depth) ** 0.25
    fc = 256
    tm = 512
    a_rows, b_rows = min(A_PATTERNS[-1][0], seq), min(B_WINDOW, seq)
    q_cols = 4 * A_COLS
    kvb = N_KV_B * HEAD_DIM
    heads_per_kv = N_HEADS_B // N_KV_B

    band_map, cached_a, cached_b = _bucket_maps(cache_a_k.shape[2], cache_b_k.shape[2])
    band_bias = _band_bias_call(rel_bias, band_map)
    bias_a, bias_b = band_bias[:len(A_PATTERNS)], band_bias[len(A_PATTERNS)]
    bias_sa, bias_sb = _cached_bias_call(rel_bias, cached_a, cached_b)

    pad = (-(n_dec + batch)) % 16
    c_all = jnp.concatenate([c_sample, c_prompt, jnp.zeros((pad, d), F32)], axis=0)
    xp = x_prompt.reshape(batch * seq, d)
    xs = x_sample.reshape(n_dec, d)
    outs = [[] for _ in range(8)]

    def head_rows(x, n_heads):
        return x.reshape(n_dec, n_heads, HEAD_DIM)

    for l in range(depth):
        mod2 = _ada_call(c_all, w_ada[l], b_ada[l])
        mod3 = mod2.reshape(mod2.shape[0] * N_MOD, 1, d)
        ffn1 = [w[l].astype(BF16) for w in (ffn1_wg, ffn1_wu, ffn1_wd)]
        ffn2 = [w[l].astype(BF16) for w in (ffn2_wg, ffn2_wu, ffn2_wd)]
        w_ext_p = jnp.concatenate([w_in[l][:, :q_cols], _dup_cols(w_in[l][:, q_cols:])], axis=1).astype(BF16)
        w_in_s = w_in[l].astype(BF16)
        wo = w_out[l].astype(BF16)

        x1 = _ffn_call(xp, mod2, mod3, (0, 1, 2), n_dec, seq, *ffn1, ln_g[l, 0], ln_b[l, 0], alpha, tm, fc)
        p_slots, p4, p16, pak, pav, pbk, pbv = _proj_prompt_call(x1, mod3, n_dec, seq, w_ext_p, a_rows, b_rows, tm)
        mix_a = _attn_a_call(p_slots, p4, p16, bias_a, batch, seq)
        mix_b = _attn_b_call(p_slots, bias_b, sinks[l], batch, seq)
        xp = _ffn_call(x1, mod2, mod3, (6, 7, 8), n_dec, seq, *ffn2, ln_g[l, 2], ln_b[l, 2], alpha, tm, fc,
                       mix=(mix_a, mix_b), w_out=wo, gate_k=5, ln1_g=ln_g[l, 1], ln1_b=ln_b[l, 1])

        s1 = _ffn_call(xs, mod2, mod3, (0, 1, 2), 0, 1, *ffn1, ln_g[l, 0], ln_b[l, 0], alpha, n_dec, fc)
        proj_s = _proj_sample_call(s1, mod2, w_in_s)
        c = A_COLS
        sak, sav = proj_s[:, c:2 * c], proj_s[:, 2 * c:3 * c]
        sbk, sbv = proj_s[:, q_cols:q_cols + kvb], proj_s[:, q_cols + kvb:q_cols + 2 * kvb]
        qt3 = jnp.concatenate([head_rows(proj_s[:, :c], N_HEADS_A),
                               head_rows(proj_s[:, 3 * c:q_cols], N_HEADS_B)], axis=1)
        knt3 = jnp.concatenate([head_rows(sak, N_HEADS_A),
                                jnp.repeat(head_rows(sbk, N_KV_B), heads_per_kv, axis=1)], axis=1)
        vnt3 = jnp.concatenate([head_rows(sav, N_HEADS_A),
                                jnp.repeat(head_rows(sbv, N_KV_B), heads_per_kv, axis=1)], axis=1)
        to_cols = (0, 2, 3, 1)
        mix3 = _attn_s_call(qt3.transpose(0, 2, 1), qt3, knt3, vnt3.transpose(0, 2, 1),
                            cache_a_k[l].transpose(to_cols), cache_a_v[l].transpose(to_cols),
                            cache_b_k[l].transpose(to_cols), cache_b_v[l].transpose(to_cols),
                            bias_sa, bias_sb, rel_bias.T, sinks[l].reshape(N_HEADS_B, 1), 2)
        mix_s = mix3.transpose(0, 2, 1).reshape(n_dec, d)
        xs = _ffn_call(s1, mod2, mod3, (6, 7, 8), 0, 1, *ffn2, ln_g[l, 2], ln_b[l, 2], alpha, n_dec, fc,
                       mix=mix_s, w_out=wo, gate_k=5, ln1_g=ln_g[l, 1], ln1_b=ln_b[l, 1])

        to_rows = (0, 3, 1, 2)
        new = [pak.transpose(to_rows), pav.transpose(to_rows), pbk.transpose(to_rows), pbv.transpose(to_rows),
               sak.reshape(n_dec, 1, N_HEADS_A, HEAD_DIM), sav.reshape(n_dec, 1, N_HEADS_A, HEAD_DIM),
               sbk.reshape(n_dec, 1, N_KV_B, HEAD_DIM), sbv.reshape(n_dec, 1, N_KV_B, HEAD_DIM)]
        for acc, arr in zip(outs, new):
            acc.append(arr)

    return (xp.reshape(batch, seq, d), xs.reshape(n_dec, 1, d)) + tuple(jnp.stack(o) for o in outs)
```

```python
import functools
import math

import jax
import jax.numpy as jnp
from jax import lax
from jax.experimental import pallas as pl
from jax.experimental.pallas import tpu as pltpu

HEAD_DIM = 64
N_HEADS_A = 8
N_HEADS_B = 8
N_KV_B = 2
A_PATTERNS = ((128, 1), (512, 4), (2048, 16))
B_WINDOW = 128
BLK = 128
NUM_BUCKETS = 32
MAX_DISTANCE = 2048
LN_EPS = 1e-5
NEG = -1e30
SCALE = HEAD_DIM ** -0.5

LANES = 128
V7X_VMEM_BYTES = 64 * 1024 * 1024
A_COLS = N_HEADS_A * HEAD_DIM
N_PAIRS = A_COLS // LANES
N_MOD = 9

F32 = jnp.float32
BF16 = jnp.bfloat16


def _vmem_limit(pipelined_bytes, scratch_bytes=0, temp_bytes=0):
    want = 2 * pipelined_bytes + scratch_bytes + temp_bytes + (4 << 20)
    return int(min(want, V7X_VMEM_BYTES - (6 << 20)))


def _resident_spec(shape):
    return pl.BlockSpec(shape, lambda *_: (0,) * len(shape), pipeline_mode=pl.Buffered(1))


def _nbytes(shape, dtype):
    return math.prod(shape) * jnp.dtype(dtype).itemsize


def _layer_norm(x, g, b):
    mu = jnp.mean(x, axis=-1, keepdims=True)
    xc = x - mu
    var = jnp.mean(xc * xc, axis=-1, keepdims=True)
    return xc * lax.rsqrt(var + LN_EPS) * g + b


def _silu(x):
    return x * jax.nn.sigmoid(x)


def _ada_kernel(c_ref, w_ref, b_ref, o_ref):
    a = _silu(c_ref[...]).astype(BF16)
    o_ref[...] = jnp.dot(a, w_ref[...].astype(BF16), preferred_element_type=F32) + b_ref[...]


def _ada_call(c_all, w_ada, b_ada):
    m, d = c_all.shape
    n = w_ada.shape[1]
    tn = n // 8
    blocks = _nbytes((m, d), F32) + _nbytes((d, tn), F32) + _nbytes((m, tn), F32)
    return pl.pallas_call(
        _ada_kernel,
        grid=(n // tn,),
        in_specs=[pl.BlockSpec((m, d), lambda j: (0, 0)),
                  pl.BlockSpec((d, tn), lambda j: (0, j)),
                  pl.BlockSpec((1, tn), lambda j: (0, j))],
        out_specs=pl.BlockSpec((m, tn), lambda j: (0, j)),
        out_shape=jax.ShapeDtypeStruct((m, n), F32),
        compiler_params=pltpu.CompilerParams(
            dimension_semantics=("arbitrary",),
            vmem_limit_bytes=_vmem_limit(blocks, temp_bytes=_nbytes((d, tn), BF16))),
        name="ada_mod",
    )(c_all, w_ada, b_ada.reshape(1, n))


def _t5_bucket(n):
    max_exact = NUM_BUCKETS // 2
    nf = jnp.maximum(n, 1).astype(F32)
    large = max_exact + (jnp.log(nf / max_exact) / math.log(MAX_DISTANCE / max_exact)
                         * (NUM_BUCKETS - max_exact)).astype(jnp.int32)
    return jnp.where(n < max_exact, n, jnp.minimum(large, NUM_BUCKETS - 1))


def _bucket_maps(a_rows, b_rows):
    steps = [d for _, d in A_PATTERNS] + [1]
    q = jnp.arange(BLK)[:, None]
    k = jnp.arange(2 * BLK)[None, :]
    dist = q + BLK - k
    valid = (dist >= 0) & (dist <= BLK)
    band = jnp.stack([jnp.where(valid, _t5_bucket(jnp.maximum(dist, 0) * s), -1) for s in steps])

    def cached(rows, step):
        back = rows - jnp.arange(rows)
        hit = (back % step == 0) & (back // step <= BLK)
        return jnp.where(hit, _t5_bucket(back), -1)[None, :].astype(jnp.int32)

    cached_a = jnp.stack([cached(a_rows, s) for s in steps[:-1]])
    return band.astype(jnp.int32), cached_a, cached(b_rows, 1)


def _band_bias_kernel(rel_ref, bmap_ref, o_ref):
    t = pl.program_id(0)
    hp = pl.program_id(1)
    bm = bmap_ref[...]
    head0 = jnp.where(t == len(A_PATTERNS), N_HEADS_A, 0) + 2 * hp
    for half in range(2):
        acc = jnp.full(bm.shape, NEG, F32)
        for b in range(NUM_BUCKETS):
            acc = jnp.where(bm == b, rel_ref[b, head0 + half], acc)
        o_ref[half * BLK:(half + 1) * BLK, :] = acc


def _band_bias_call(rel_bias, band_map):
    n_tab = band_map.shape[0]
    return pl.pallas_call(
        _band_bias_kernel,
        grid=(n_tab, N_PAIRS),
        in_specs=[pl.BlockSpec(memory_space=pltpu.SMEM),
                  pl.BlockSpec((None, BLK, 2 * BLK), lambda t, hp: (t, 0, 0))],
        out_specs=pl.BlockSpec((None, None, 2 * BLK, 2 * BLK), lambda t, hp: (t, hp, 0, 0)),
        out_shape=jax.ShapeDtypeStruct((n_tab, N_PAIRS, 2 * BLK, 2 * BLK), F32),
        compiler_params=pltpu.CompilerParams(dimension_semantics=("arbitrary", "arbitrary")),
        name="band_bias",
    )(rel_bias, band_map)


def _cached_bias_kernel(relt_ref, amap_ref, bmap_ref, oa_ref, ob_ref):
    relt = relt_ref[...]

    def table(bm, heads):
        acc = jnp.full(bm.shape, NEG, F32)
        for b in range(NUM_BUCKETS):
            acc = jnp.where(bm == b, heads[:, b:b + 1], acc)
        return acc

    for t in range(amap_ref.shape[0]):
        oa_ref[t] = table(jnp.broadcast_to(amap_ref[t], oa_ref.shape[1:]), relt[:N_HEADS_A])
    ob_ref[...] = table(jnp.broadcast_to(bmap_ref[...], ob_ref.shape), relt[N_HEADS_A:])


def _cached_bias_call(rel_bias, cached_a, cached_b):
    return pl.pallas_call(
        _cached_bias_kernel,
        out_shape=(jax.ShapeDtypeStruct((cached_a.shape[0], N_HEADS_A, cached_a.shape[2]), F32),
                   jax.ShapeDtypeStruct((N_HEADS_B, cached_b.shape[1]), F32)),
        name="cached_bias",
    )(rel_bias.T, cached_a, cached_b)


def _ffn_kernel(*refs, alpha, pre_mix, mix_slots, n_chunks, fc, n_side):
    refs = list(refs)
    x_ref = refs.pop(0)
    if pre_mix:
        mix_refs = [refs.pop(0) for _ in range(2 if mix_slots else 1)]
        wo_ref, gm_ref, ln1g_ref, ln1b_ref = (refs.pop(0) for _ in range(4))
    sh_ref, sc_ref, gt_ref, wg_ref, wu_ref, wd_ref, lng_ref, lnb_ref = (refs.pop(0) for _ in range(8))
    if n_side:
        side_in = [refs.pop(0) for _ in range(10)]
        ak_hbm, av_hbm, o_ref, oc_ref, h_ref, acc_ref, kbuf, vbuf, sem = (refs.pop(0) for _ in range(9))
        side_scratch = refs
        n_samples = ak_hbm.shape[0]
        first = pl.program_id(0) * n_side

        def cache_copies(g, slot):
            return (pltpu.make_async_copy(ak_hbm.at[g], kbuf.at[slot], sem.at[0, slot]),
                    pltpu.make_async_copy(av_hbm.at[g], vbuf.at[slot], sem.at[1, slot]))

        @pl.when(pl.program_id(0) == 0)
        def _():
            for cp in cache_copies(0, 0):
                cp.start()

        def side(j):
            g, slot = first + j, j % 2
            for cp in cache_copies(g, slot):
                cp.wait()

            @pl.when(g + 1 < n_samples)
            def _():
                for cp in cache_copies(g + 1, 1 - slot):
                    cp.start()

            _cached_sample(j, kbuf.at[slot], vbuf.at[slot], *side_in, oc_ref, *side_scratch)

        assert n_side % 2 == 0
        side_at = {(j * n_chunks) // n_side: j for j in range(n_side)}
    else:
        o_ref, h_ref, acc_ref = refs
        side_at = {}

    x = x_ref[...]
    if pre_mix:
        if mix_slots:
            mixed = jnp.concatenate([r[j] for r in mix_refs for j in range(N_PAIRS)], axis=-1)
        else:
            mixed = mix_refs[0][...].astype(BF16)
        y = jnp.dot(mixed, wo_ref[...], preferred_element_type=F32)
        x = _layer_norm(alpha * x + gm_ref[...] * y, ln1g_ref[...], ln1b_ref[...])
    h_ref[...] = (x * (1.0 + sc_ref[...]) + sh_ref[...]).astype(BF16)
    for c in range(n_chunks):
        if c in side_at:
            side(side_at[c])
        cols = slice(c * fc, (c + 1) * fc)
        g = jnp.dot(h_ref[...], wg_ref[:, cols], preferred_element_type=F32)
        u = jnp.dot(h_ref[...], wu_ref[:, cols], preferred_element_type=F32)
        y = jnp.dot((_silu(g) * u).astype(BF16), wd_ref[cols, :], preferred_element_type=F32)
        if c == 0:
            acc_ref[...] = y
        else:
            acc_ref[...] += y
    o_ref[...] = _layer_norm(alpha * x + 0.5 * gt_ref[...] * acc_ref[...], lng_ref[...], lnb_ref[...])


def _ffn_call(x, mod2, mod3, mod_ks, seq0, rows_per_seq, wg, wu, wd, ln_g, ln_b, alpha, tm, fc,
              mix=None, w_out=None, gate_k=None, ln1_g=None, ln1_b=None, side=None):
    n, d = x.shape
    d_ff = wg.shape[1]
    n_chunks = d_ff // fc
    assert n_chunks * fc == d_ff
    per_row = rows_per_seq == 1
    tiles_per_seq = max(rows_per_seq // tm, 1)

    def mod_spec(k):
        if per_row:
            return pl.BlockSpec((tm, d), lambda i, k=k: (seq0 // tm + i, k))
        return pl.BlockSpec((None, 1, d), lambda i, k=k: ((seq0 + i // tiles_per_seq) * N_MOD + k, 0, 0))

    mod_arr = mod2 if per_row else mod3
    row_spec = pl.BlockSpec((tm, d), lambda i: (i, 0))
    vec_spec = pl.BlockSpec((1, d), lambda i: (0, 0))
    args, specs = [x], [row_spec]
    pre_mix = mix is not None
    mix_slots = pre_mix and isinstance(mix, tuple)
    blocks = 2 * _nbytes((tm, d), F32) + (3 * _nbytes((tm, d), F32) if per_row else 0)
    resident = 3 * _nbytes(wg.shape, BF16)
    if pre_mix:
        if mix_slots:
            for mm in mix:
                args.append(mm)
                specs.append(pl.BlockSpec((N_PAIRS, tm, LANES), lambda i: (0, i, 0)))
        else:
            args.append(mix)
            specs.append(row_spec)
        args += [w_out, mod_arr, ln1_g.reshape(1, d), ln1_b.reshape(1, d)]
        specs += [_resident_spec((d, d)), mod_spec(gate_k), vec_spec, vec_spec]
        blocks += _nbytes((tm, d), F32)
        resident += _nbytes((d, d), BF16)
    args += [mod_arr, mod_arr, mod_arr, wg, wu, wd, ln_g.reshape(1, d), ln_b.reshape(1, d)]
    specs += [mod_spec(mod_ks[0]), mod_spec(mod_ks[1]), mod_spec(mod_ks[2]),
              _resident_spec(wg.shape), _resident_spec(wu.shape), _resident_spec(wd.shape), vec_spec, vec_spec]
    scratch = _nbytes((tm, d), BF16) + _nbytes((tm, d), F32) + resident
    temps = 6 * _nbytes((tm, fc), F32) + 3 * _nbytes((tm, d), F32)
    out_specs, out_shape = row_spec, jax.ShapeDtypeStruct((n, d), F32)
    scratch_shapes = [pltpu.VMEM((tm, d), BF16), pltpu.VMEM((tm, d), F32)]
    n_side = 0
    if side is not None:
        small, (ak, av) = side[:-2], side[-2:]
        n_steps = n // tm
        n_side = ak.shape[0] // n_steps
        assert n_side * n_steps == ak.shape[0]
        for a in small[:6]:
            args.append(a)
            specs.append(pl.BlockSpec((n_side,) + a.shape[1:], lambda i, nd=a.ndim: (i,) + (0,) * (nd - 1)))
            blocks += n_side * _nbytes(a.shape[1:-1] + (max(a.shape[-1], LANES),), F32)
        for a in small[6:]:
            args.append(a)
            specs.append(pl.BlockSpec(a.shape, lambda i, nd=a.ndim: (0,) * nd))
            blocks += _nbytes(a.shape, F32)
        args += [ak, av]
        specs += [pl.BlockSpec(memory_space=pl.ANY)] * 2
        q3 = small[0]
        out_specs = [row_spec, pl.BlockSpec((n_side,) + q3.shape[1:], lambda i: (i, 0, 0))]
        out_shape = [out_shape, jax.ShapeDtypeStruct(q3.shape, F32)]
        sa_shape = (n_side, N_HEADS_A, ak.shape[3])
        sb_shape = (n_side, N_HEADS_B, small[4].shape[3])
        scratch_shapes += [pltpu.VMEM((2,) + ak.shape[1:], F32), pltpu.VMEM((2,) + av.shape[1:], F32),
                           pltpu.SemaphoreType.DMA((2, 2)),
                           pltpu.VMEM(sa_shape, F32), pltpu.VMEM(sb_shape, F32),
                           pltpu.VMEM(sa_shape, F32), pltpu.VMEM(sb_shape, F32)]
        scratch += 4 * _nbytes(ak.shape[1:], F32) + 2 * _nbytes(sa_shape, F32) + 2 * _nbytes(sb_shape, F32)
    return pl.pallas_call(
        functools.partial(_ffn_kernel, alpha=alpha, pre_mix=pre_mix, mix_slots=mix_slots,
                          n_chunks=n_chunks, fc=fc, n_side=n_side),
        grid=(n // tm,),
        in_specs=specs,
        out_specs=out_specs,
        out_shape=out_shape,
        scratch_shapes=scratch_shapes,
        compiler_params=pltpu.CompilerParams(
            dimension_semantics=("arbitrary",),
            vmem_limit_bytes=_vmem_limit(blocks, scratch, temps)),
        name=("ffn_mix" if pre_mix else "ffn") + ("_cached" if n_side else ""),
    )(*args)


def _proj_prompt_kernel(x_ref, sh_ref, sc_ref, w_ref, p_ref, p4_ref, p16_ref, ak_ref, av_ref, bk_ref, bv_ref,
                        stage_ref, stage4_ref, *, tm, tiles_per_seq, a_tiles, b_rows):
    t = pl.program_id(0) % tiles_per_seq
    h = (x_ref[...] * (1.0 + sc_ref[...]) + sh_ref[...]).astype(BF16)
    n_groups = w_ref.shape[1] // A_COLS
    lane_lo = lax.broadcasted_iota(jnp.int32, (b_rows, LANES), 1) < HEAD_DIM
    for g in range(n_groups):
        res = jnp.dot(h, w_ref[:, g * A_COLS:(g + 1) * A_COLS], preferred_element_type=F32)
        scaled = res * SCALE if g in (0, 3) else res
        for s in range(N_PAIRS):
            slab = scaled[:, s * LANES:(s + 1) * LANES]
            p_ref[g * N_PAIRS + s] = slab.astype(BF16)
            if g < 3:
                slot = g * N_PAIRS + s
                d4, d16 = p4_ref.shape[1], p16_ref.shape[1]
                stage_ref[slot] = slab
                for r in range(d4):
                    rows = stage_ref[slot, pl.ds(r, tm // d4, stride=d4), :]
                    p4_ref[slot, r] = rows.astype(BF16)
                    stage4_ref[slot, r] = rows
                for r in range(d16):
                    rows = stage4_ref[slot, r % d4, pl.ds(r // d4, tm // d16, stride=d16 // d4), :]
                    p16_ref[slot, r] = rows.astype(BF16)
        if g in (1, 2):
            out_ref = ak_ref if g == 1 else av_ref

            @pl.when(t >= tiles_per_seq - a_tiles)
            def _(res=res, out_ref=out_ref):
                out_ref[...] = res.T.reshape(out_ref.shape)
        if g == 4:
            @pl.when(t == tiles_per_seq - 1)
            def _(res=res):
                tail = res[tm - b_rows:, :]
                kb = jnp.where(lane_lo, tail[:, 0:LANES], tail[:, LANES:2 * LANES])
                vb = jnp.where(lane_lo, tail[:, 2 * LANES:3 * LANES], tail[:, 3 * LANES:4 * LANES])
                bk_ref[...] = kb.T.reshape(bk_ref.shape)
                bv_ref[...] = vb.T.reshape(bv_ref.shape)


def _proj_prompt_call(x, mod3, seq0, seq, w_ext, a_rows, b_rows, tm):
    n, d = x.shape
    batch = n // seq
    tiles_per_seq = seq // tm
    a_tiles = a_rows // tm
    n_slots = w_ext.shape[1] // LANES

    def mod_spec(k):
        return pl.BlockSpec((None, 1, d), lambda i: ((seq0 + i // tiles_per_seq) * N_MOD + k, 0, 0))

    def a_map(i):
        return (i // tiles_per_seq, 0, 0, jnp.maximum(i % tiles_per_seq - (tiles_per_seq - a_tiles), 0))

    a_blk = (None, N_HEADS_A, HEAD_DIM, tm)
    b_blk = (None, N_KV_B, HEAD_DIM, b_rows)
    a_slots = 3 * N_PAIRS
    d4, d16 = A_PATTERNS[1][1], A_PATTERNS[2][1]
    assert tm % (16 * d16) == 0

    def dil_spec(dil):
        return pl.BlockSpec((a_slots, None, dil, tm // dil, LANES),
                            lambda i: (0, i // tiles_per_seq, 0, i % tiles_per_seq, 0))

    blocks = (_nbytes((tm, d), F32) + _nbytes(w_ext.shape, BF16) + _nbytes((n_slots + 2 * a_slots, tm, LANES), BF16)
              + 2 * _nbytes((tm, A_COLS), F32) + 2 * _nbytes((b_rows, LANES), F32))
    return pl.pallas_call(
        functools.partial(_proj_prompt_kernel, tm=tm, tiles_per_seq=tiles_per_seq, a_tiles=a_tiles,
                          b_rows=b_rows),
        grid=(n // tm,),
        in_specs=[pl.BlockSpec((tm, d), lambda i: (i, 0)), mod_spec(3), mod_spec(4),
                  pl.BlockSpec(w_ext.shape, lambda i: (0, 0))],
        out_specs=[pl.BlockSpec((n_slots, tm, LANES), lambda i: (0, i, 0)),
                   dil_spec(d4), dil_spec(d16),
                   pl.BlockSpec(a_blk, a_map),
                   pl.BlockSpec(a_blk, a_map),
                   pl.BlockSpec(b_blk, lambda i: (i // tiles_per_seq, 0, 0, 0)),
                   pl.BlockSpec(b_blk, lambda i: (i // tiles_per_seq, 0, 0, 0))],
        out_shape=[jax.ShapeDtypeStruct((n_slots, n, LANES), BF16),
                   jax.ShapeDtypeStruct((a_slots, batch, d4, seq // d4, LANES), BF16),
                   jax.ShapeDtypeStruct((a_slots, batch, d16, seq // d16, LANES), BF16),
                   jax.ShapeDtypeStruct((batch, N_HEADS_A, HEAD_DIM, a_rows), F32),
                   jax.ShapeDtypeStruct((batch, N_HEADS_A, HEAD_DIM, a_rows), F32),
                   jax.ShapeDtypeStruct((batch, N_KV_B, HEAD_DIM, b_rows), F32),
                   jax.ShapeDtypeStruct((batch, N_KV_B, HEAD_DIM, b_rows), F32)],
        scratch_shapes=[pltpu.VMEM((a_slots, tm, LANES), F32), pltpu.VMEM((a_slots, d4, tm // d4, LANES), F32)],
        compiler_params=pltpu.CompilerParams(
            dimension_semantics=("arbitrary",),
            vmem_limit_bytes=_vmem_limit(blocks, 2 * _nbytes((a_slots, tm, LANES), F32),
                                         4 * _nbytes((tm, A_COLS), F32))),
        name="proj_prompt",
    )(x, mod3, mod3, w_ext)


def _proj_sample_kernel(x_ref, sh_ref, sc_ref, w_ref, o_ref):
    h = (x_ref[...] * (1.0 + sc_ref[...]) + sh_ref[...]).astype(BF16)
    o_ref[...] = jnp.dot(h, w_ref[...], preferred_element_type=F32)


def _proj_sample_call(x, mod2, w_ext):
    n, d = x.shape
    cols = w_ext.shape[1]
    blocks = 3 * _nbytes((n, d), F32) + _nbytes(w_ext.shape, BF16) + _nbytes((n, cols), F32)
    return pl.pallas_call(
        _proj_sample_kernel,
        grid=(1,),
        in_specs=[pl.BlockSpec((n, d), lambda i: (0, 0)),
                  pl.BlockSpec((n, d), lambda i: (0, 3)),
                  pl.BlockSpec((n, d), lambda i: (0, 4)),
                  pl.BlockSpec(w_ext.shape, lambda i: (0, 0))],
        out_specs=pl.BlockSpec((n, cols), lambda i: (0, 0)),
        out_shape=jax.ShapeDtypeStruct((n, cols), F32),
        compiler_params=pltpu.CompilerParams(
            dimension_semantics=("arbitrary",), vmem_limit_bytes=_vmem_limit(blocks)),
        name="proj_sample",
    )(x, mod2, mod2, w_ext)


def _pair_scores(q, kk, bias, lane_lo):
    zero = jnp.zeros_like(q)
    q2 = jnp.concatenate([jnp.where(lane_lo, q, zero), jnp.where(lane_lo, zero, q)], axis=0)
    s = lax.dot_general(q2, kk, (((1,), (1,)), ((), ())), preferred_element_type=F32)
    return s + bias


def _pair_select(a, lane_lo):
    return jnp.where(lane_lo, a[:BLK], a[BLK:])


GROUP = 8


def _band_sequence(q_ref, k_ref, v_ref, n_blocks, load_bias, block, prefetch=lambda jb: None):
    def one(jb, first, pre):
        if first:
            block(jb, q_ref[0:BLK, :], k_ref[0:BLK, :], v_ref[0:BLK, :], load_bias(True), pre)
        else:
            r0 = jb * BLK
            if not isinstance(jb, int):
                r0 = pl.multiple_of(r0, BLK)
            block(jb, q_ref[pl.ds(r0, BLK), :], k_ref[pl.ds(r0 - BLK, 2 * BLK), :],
                  v_ref[pl.ds(r0 - BLK, 2 * BLK), :], load_bias(False), pre)

    def run(jbs, has_first):
        pres = [prefetch(jb) for jb in jbs]
        for j, (jb, pre) in enumerate(zip(jbs, pres)):
            one(jb, has_first and j == 0, pre)

    run(list(range(min(GROUP, n_blocks))), True)
    if n_blocks > GROUP:
        assert n_blocks % GROUP == 0

        def group(g, carry):
            run([g * GROUP + j for j in range(GROUP)], False)
            return carry

        lax.fori_loop(1, n_blocks // GROUP, group, 0)


def _attn_a_kernel(q16_ref, k16_ref, v16_ref, q4_ref, k4_ref, v4_ref, q1_ref, k1_ref, v1_ref,
                   bias_ref, o_ref, acc16_ref, m16_ref, den16_ref, acc4_ref, m4_ref, den4_ref, *, seq):
    lane_lo = lax.broadcasted_iota(jnp.int32, (BLK, LANES), 1) < HEAD_DIM
    state16 = (acc16_ref, m16_ref, den16_ref)
    state4 = (acc4_ref, m4_ref, den4_ref)

    def update(q, kk, vv, bias, rows, old, new):
        s = _pair_scores(q, kk, bias, lane_lo)
        part = s[:, :LANES] if s.shape[1] == LANES else jnp.maximum(s[:, :LANES], s[:, LANES:])
        if old is not None:
            acc_old, m_old, den_old = old
            part = jnp.maximum(part, jnp.concatenate([jnp.where(lane_lo, m_old, NEG),
                                                      jnp.where(lane_lo, NEG, m_old)], axis=0))
        m2 = jnp.max(part, axis=-1, keepdims=True)
        p = jnp.exp(s - m2)
        l2 = jnp.sum(p, axis=-1, keepdims=True)
        pv = _pair_select(jnp.dot(p.astype(BF16), vv, preferred_element_type=F32), lane_lo)
        m_new = _pair_select(m2, lane_lo)
        den = _pair_select(l2, lane_lo)
        if old is None:
            acc = pv
        else:
            a = jnp.exp(m_old - m_new)
            acc = a * acc_old + pv
            den = a * den_old + den
        if new is None:
            o_ref[rows, :] = (acc / den).astype(o_ref.dtype)
        else:
            new[0][rows, :] = acc
            new[1][rows, :] = m_new
            new[2][rows, :] = den

    def branch(q_ref, k_ref, v_ref, table, dil, r, old, new):
        def rows(jb):
            start = r + jb * (BLK * dil)
            return pl.ds(start, BLK, stride=dil) if dil > 1 else pl.ds(start, BLK)

        def prefetch(jb):
            return None if old is None else tuple(ref[rows(jb), :] for ref in old)

        def block(jb, q, kk, vv, bias, pre):
            update(q, kk, vv, bias, rows(jb), pre, new)

        def load_bias(first):
            return bias_ref[table, :, BLK:2 * BLK] if first else bias_ref[table]

        _band_sequence(q_ref, k_ref, v_ref, q_ref.shape[0] // BLK, load_bias, block, prefetch)

    d16 = q16_ref.shape[0]
    per_group = max(GROUP // (q16_ref.shape[1] // BLK), 1)
    assert d16 % per_group == 0

    def group16(g, carry):
        for rr in range(per_group):
            r = g * per_group + rr
            branch(q16_ref.at[r], k16_ref.at[r], v16_ref.at[r], 2, d16, r, None, state16)
        return carry

    lax.fori_loop(0, d16 // per_group, group16, 0)

    d4 = q4_ref.shape[0]

    def residue4(r, carry):
        branch(q4_ref.at[r], k4_ref.at[r], v4_ref.at[r], 1, d4, r, state16, state4)
        return carry

    lax.fori_loop(0, d4, residue4, 0)
    branch(q1_ref, k1_ref, v1_ref, 0, 1, 0, state4, None)


def _attn_a_call(p_slots, p4, p16, bias_a, batch, seq):
    n_slots = p_slots.shape[0]
    dils = [d for _, d in A_PATTERNS]
    assert dils == [1, p4.shape[2], p16.shape[2]] and seq % (dils[2] * BLK) == 0
    view1 = p_slots.reshape(n_slots, batch, seq, LANES)

    in_specs, args = [], []
    for arr in (p16, p4, view1):
        for slot0 in (0, N_PAIRS, 2 * N_PAIRS):
            in_specs.append(pl.BlockSpec((None, None) + arr.shape[2:],
                                         lambda b, hp, slot0=slot0, nd=arr.ndim: (slot0 + hp, b) + (0,) * (nd - 2)))
            args.append(arr)
    in_specs.append(pl.BlockSpec((len(A_PATTERNS), None, 2 * BLK, 2 * BLK), lambda b, hp: (0, hp, 0, 0)))
    args.append(bias_a)
    blocks = 10 * _nbytes((seq, LANES), BF16) + _nbytes((3, 2 * BLK, 2 * BLK), F32)
    scratch = 6 * _nbytes((seq, LANES), F32)
    return pl.pallas_call(
        functools.partial(_attn_a_kernel, seq=seq),
        grid=(batch, N_PAIRS),
        in_specs=in_specs,
        out_specs=pl.BlockSpec((None, seq, LANES), lambda b, hp: (hp, b, 0)),
        out_shape=jax.ShapeDtypeStruct((N_PAIRS, batch * seq, LANES), BF16),
        scratch_shapes=[pltpu.VMEM((seq, LANES), F32)] * 6,
        compiler_params=pltpu.CompilerParams(
            dimension_semantics=("arbitrary", "arbitrary"),
            vmem_limit_bytes=_vmem_limit(blocks, scratch, 4 * GROUP * _nbytes((2 * BLK, 2 * BLK), F32))),
        name="attn_dilated",
    )(*args)


def _attn_b_kernel(sink_ref, q_ref, k_ref, v_ref, bias_ref, o_ref, *, seq):
    hp = pl.program_id(1)
    lane_lo = lax.broadcasted_iota(jnp.int32, (BLK, LANES), 1) < HEAD_DIM
    row_lo = lax.broadcasted_iota(jnp.int32, (2 * BLK, 1), 0) < BLK
    sink2 = jnp.where(row_lo, sink_ref[2 * hp], sink_ref[2 * hp + 1])

    def block(jb, q, kk, vv, bias, pre):
        s = _pair_scores(q, kk, bias, lane_lo)
        m2 = jnp.maximum(jnp.max(s, axis=-1, keepdims=True), sink2)
        p = jnp.exp(s - m2)
        l2 = jnp.sum(p, axis=-1, keepdims=True) + jnp.exp(sink2 - m2)
        pv = _pair_select(jnp.dot(p.astype(BF16), vv, preferred_element_type=F32), lane_lo)
        r0 = jb * BLK if isinstance(jb, int) else pl.multiple_of(jb * BLK, BLK)
        o_ref[pl.ds(r0, BLK), :] = (pv / _pair_select(l2, lane_lo)).astype(o_ref.dtype)

    def load_bias(first):
        return bias_ref[:, BLK:2 * BLK] if first else bias_ref[...]

    _band_sequence(q_ref, k_ref, v_ref, seq // BLK, load_bias, block)


def _attn_b_call(p_slots, bias_b, sinks, batch, seq):
    n_slots = p_slots.shape[0]
    view = p_slots.reshape(n_slots, batch, seq, LANES)
    q0, k0, v0 = 3 * N_PAIRS, 4 * N_PAIRS, 4 * N_PAIRS + N_KV_B
    pairs_per_kv = N_PAIRS // N_KV_B
    blk = (None, None, seq, LANES)
    blocks = 4 * _nbytes((seq, LANES), BF16) + _nbytes((2 * BLK, 2 * BLK), F32)
    return pl.pallas_call(
        functools.partial(_attn_b_kernel, seq=seq),
        grid=(batch, N_PAIRS),
        in_specs=[pl.BlockSpec(memory_space=pltpu.SMEM),
                  pl.BlockSpec(blk, lambda b, hp: (q0 + hp, b, 0, 0)),
                  pl.BlockSpec(blk, lambda b, hp: (k0 + hp // pairs_per_kv, b, 0, 0)),
                  pl.BlockSpec(blk, lambda b, hp: (v0 + hp // pairs_per_kv, b, 0, 0)),
                  pl.BlockSpec((None, 2 * BLK, 2 * BLK), lambda b, hp: (hp, 0, 0))],
        out_specs=pl.BlockSpec((None, seq, LANES), lambda b, hp: (hp, b, 0)),
        out_shape=jax.ShapeDtypeStruct((N_PAIRS, batch * seq, LANES), BF16),
        compiler_params=pltpu.CompilerParams(
            dimension_semantics=("arbitrary", "arbitrary"),
            vmem_limit_bytes=_vmem_limit(blocks, temp_bytes=8 * _nbytes((2 * BLK, 2 * BLK), F32))),
        name="attn_window",
    )(sinks, view, view, view, bias_b)


def _cached_sample(i, ak_ref, av_ref, q_ref, qt_ref, knt_ref, vn_ref, bk_ref, bv_ref, bias_a_ref, bias_b_ref,
                   relt_ref, sink_ref, o_ref, sa_ref, sb_ref, wa_ref, wb_ref):
    n_branches = bias_a_ref.shape[0]
    heads_per_kv = N_HEADS_B // N_KV_B
    bias0 = relt_ref[:, 0:1]
    sink = sink_ref[...]
    q = q_ref[i] * SCALE
    vn = vn_ref[i]
    s0 = jnp.sum(qt_ref[i] * SCALE * knt_ref[i], axis=-1, keepdims=True) + bias0
    for h in range(N_HEADS_A):
        sa_ref[i, h:h + 1, :] = jnp.sum(ak_ref[h] * q[:, h:h + 1], axis=0, keepdims=True)
    for hb in range(N_HEADS_B):
        h = N_HEADS_A + hb
        sb_ref[i, hb:hb + 1, :] = jnp.sum(bk_ref[i, hb // heads_per_kv] * q[:, h:h + 1], axis=0, keepdims=True)

    s0a = s0[:N_HEADS_A]
    ts = [sa_ref[i] + bias_a_ref[t] for t in range(n_branches)]
    m = s0a
    for t_ in ts:
        m = jnp.maximum(m, jnp.max(t_, axis=-1, keepdims=True))
    w = jnp.exp(ts[0] - m)
    for t_ in ts[1:]:
        w = w + jnp.exp(t_ - m)
    p0a = n_branches * jnp.exp(s0a - m)
    inv = 1.0 / (jnp.sum(w, axis=-1, keepdims=True) + p0a)
    wa_ref[i] = w * inv
    p0a = p0a * inv

    s0b = s0[N_HEADS_A:]
    tb = sb_ref[i] + bias_b_ref[...]
    mb = jnp.maximum(jnp.maximum(s0b, sink), jnp.max(tb, axis=-1, keepdims=True))
    wb = jnp.exp(tb - mb)
    p0b = jnp.exp(s0b - mb)
    invb = 1.0 / (jnp.sum(wb, axis=-1, keepdims=True) + p0b + jnp.exp(sink - mb))
    wb_ref[i] = wb * invb
    p0b = p0b * invb

    cols = []
    for h in range(N_HEADS_A):
        acc = jnp.sum(av_ref[h] * wa_ref[i, h:h + 1, :], axis=-1, keepdims=True)
        cols.append(acc + p0a[h:h + 1, :] * vn[:, h:h + 1])
    for hb in range(N_HEADS_B):
        h = N_HEADS_A + hb
        acc = jnp.sum(bv_ref[i, hb // heads_per_kv] * wb_ref[i, hb:hb + 1, :], axis=-1, keepdims=True)
        cols.append(acc + p0b[hb:hb + 1, :] * vn[:, h:h + 1])
    o_ref[i] = jnp.concatenate(cols, axis=1)


def _attn_s_kernel(q_ref, qt_ref, knt_ref, vn_ref, ak_ref, av_ref, *rest, bb):
    for i in range(bb):
        _cached_sample(i, ak_ref.at[i], av_ref.at[i], q_ref, qt_ref, knt_ref, vn_ref, *rest)


def _attn_s_call(q3, qt3, knt3, vn3, cak_t, cav_t, cbk_t, cbv_t, bias_sa, bias_sb, rel_t, sinks, bb):
    n, hd, n_heads = q3.shape
    a_blk = (bb,) + cak_t.shape[1:]
    b_blk = (bb,) + cbk_t.shape[1:]
    col_spec = pl.BlockSpec((bb, hd, n_heads), lambda i: (i, 0, 0))
    row_spec = pl.BlockSpec((bb, n_heads, hd), lambda i: (i, 0, 0))

    def whole(a):
        return pl.BlockSpec(a.shape, lambda i: (0,) * a.ndim)

    sa_shape = (bb, N_HEADS_A, cak_t.shape[3])
    sb_shape = (bb, N_HEADS_B, cbk_t.shape[3])
    blocks = (2 * _nbytes(a_blk, F32) + 2 * _nbytes(b_blk, F32) + 5 * _nbytes((bb, hd, LANES), F32)
              + _nbytes(bias_sa.shape, F32))
    scratch = 2 * _nbytes(sa_shape, F32) + 2 * _nbytes(sb_shape, F32)
    return pl.pallas_call(
        functools.partial(_attn_s_kernel, bb=bb),
        grid=(n // bb,),
        in_specs=[col_spec, row_spec, row_spec, col_spec,
                  pl.BlockSpec(a_blk, lambda i: (i, 0, 0, 0)), pl.BlockSpec(a_blk, lambda i: (i, 0, 0, 0)),
                  pl.BlockSpec(b_blk, lambda i: (i, 0, 0, 0)), pl.BlockSpec(b_blk, lambda i: (i, 0, 0, 0)),
                  whole(bias_sa), whole(bias_sb), whole(rel_t), whole(sinks)],
        out_specs=col_spec,
        out_shape=jax.ShapeDtypeStruct((n, hd, n_heads), F32),
        scratch_shapes=[pltpu.VMEM(sa_shape, F32), pltpu.VMEM(sb_shape, F32),
                        pltpu.VMEM(sa_shape, F32), pltpu.VMEM(sb_shape, F32)],
        compiler_params=pltpu.CompilerParams(
            dimension_semantics=("arbitrary",),
            vmem_limit_bytes=_vmem_limit(blocks, scratch, 4 * _nbytes(cak_t.shape[2:], F32))),
        name="attn_cached",
    )(q3, qt3, knt3, vn3, cak_t, cav_t, cbk_t, cbv_t, bias_sa, bias_sb, rel_t, sinks)


def _dup_cols(w):
    parts = []
    for j in range(2 * N_KV_B):
        col = w[:, j * HEAD_DIM:(j + 1) * HEAD_DIM]
        parts += [col, col]
    return jnp.concatenate(parts, axis=1)


def kernel(x_prompt, x_sample, cache_a_k, cache_a_v, cache_b_k, cache_b_v, c_prompt, c_sample, rel_bias, w_ada, b_ada, ffn1_wg, ffn1_wu, ffn1_wd, w_in, w_out, sinks, ffn2_wg, ffn2_wu, ffn2_wd, ln_g, ln_b):
    batch, seq, d = x_prompt.shape
    n_dec, dec_seq, _ = x_sample.shape
    depth = w_ada.shape[0]
    d_ff = ffn1_wg.shape[2]
    assert dec_seq == 1 and d == 2 * A_COLS and all(w // dl == BLK for w, dl in A_PATTERNS)
    alpha = (2 * depth) ** 0.25
    fc = 256
    tm = 512
    a_rows, b_rows = min(A_PATTERNS[-1][0], seq), min(B_WINDOW, seq)
    q_cols = 4 * A_COLS
    kvb = N_KV_B * HEAD_DIM
    heads_per_kv = N_HEADS_B // N_KV_B

    band_map, cached_a, cached_b = _bucket_maps(cache_a_k.shape[2], cache_b_k.shape[2])
    band_bias = _band_bias_call(rel_bias, band_map)
    bias_a, bias_b = band_bias[:len(A_PATTERNS)], band_bias[len(A_PATTERNS)]
    bias_sa, bias_sb = _cached_bias_call(rel_bias, cached_a, cached_b)

    pad = (-(n_dec + batch)) % 16
    c_all = jnp.concatenate([c_sample, c_prompt, jnp.zeros((pad, d), F32)], axis=0)
    xp = x_prompt.reshape(batch * seq, d)
    xs = x_sample.reshape(n_dec, d)
    outs = [[] for _ in range(8)]

    def head_rows(x, n_heads):
        return x.reshape(n_dec, n_heads, HEAD_DIM)

    for l in range(depth):
        mod2 = _ada_call(c_all, w_ada[l], b_ada[l])
        mod3 = mod2.reshape(mod2.shape[0] * N_MOD, 1, d)
        ffn1 = [w[l].astype(BF16) for w in (ffn1_wg, ffn1_wu, ffn1_wd)]
        ffn2 = [w[l].astype(BF16) for w in (ffn2_wg, ffn2_wu, ffn2_wd)]
        w_ext_p = jnp.concatenate([w_in[l][:, :q_cols], _dup_cols(w_in[l][:, q_cols:])], axis=1).astype(BF16)
        w_in_s = w_in[l].astype(BF16)
        wo = w_out[l].astype(BF16)

        s1 = _ffn_call(xs, mod2, mod3, (0, 1, 2), 0, 1, *ffn1, ln_g[l, 0], ln_b[l, 0], alpha, n_dec, fc)
        proj_s = _proj_sample_call(s1, mod2, w_in_s)
        c = A_COLS
        sak, sav = proj_s[:, c:2 * c], proj_s[:, 2 * c:3 * c]
        sbk, sbv = proj_s[:, q_cols:q_cols + kvb], proj_s[:, q_cols + kvb:q_cols + 2 * kvb]
        qt3 = jnp.concatenate([head_rows(proj_s[:, :c], N_HEADS_A),
                               head_rows(proj_s[:, 3 * c:q_cols], N_HEADS_B)], axis=1)
        knt3 = jnp.concatenate([head_rows(sak, N_HEADS_A),
                                jnp.repeat(head_rows(sbk, N_KV_B), heads_per_kv, axis=1)], axis=1)
        vnt3 = jnp.concatenate([head_rows(sav, N_HEADS_A),
                                jnp.repeat(head_rows(sbv, N_KV_B), heads_per_kv, axis=1)], axis=1)
        to_cols = (0, 2, 3, 1)
        decode_side = (qt3.transpose(0, 2, 1), qt3, knt3, vnt3.transpose(0, 2, 1),
                       cache_b_k[l].transpose(to_cols), cache_b_v[l].transpose(to_cols),
                       bias_sa, bias_sb, rel_bias.T, sinks[l].reshape(N_HEADS_B, 1),
                       cache_a_k[l].transpose(to_cols), cache_a_v[l].transpose(to_cols))

        x1, mix3 = _ffn_call(xp, mod2, mod3, (0, 1, 2), n_dec, seq, *ffn1, ln_g[l, 0], ln_b[l, 0], alpha, tm, fc,
                             side=decode_side)
        p_slots, p4, p16, pak, pav, pbk, pbv = _proj_prompt_call(x1, mod3, n_dec, seq, w_ext_p, a_rows, b_rows, tm)
        mix_a = _attn_a_call(p_slots, p4, p16, bias_a, batch, seq)
        mix_b = _attn_b_call(p_slots, bias_b, sinks[l], batch, seq)
        xp = _ffn_call(x1, mod2, mod3, (6, 7, 8), n_dec, seq, *ffn2, ln_g[l, 2], ln_b[l, 2], alpha, tm, fc,
                       mix=(mix_a, mix_b), w_out=wo, gate_k=5, ln1_g=ln_g[l, 1], ln1_b=ln_b[l, 1])

        mix_s = mix3.transpose(0, 2, 1).reshape(n_dec, d)
        xs = _ffn_call(s1, mod2, mod3, (6, 7, 8), 0, 1, *ffn2, ln_g[l, 2], ln_b[l, 2], alpha, n_dec, fc,
                       mix=mix_s, w_out=wo, gate_k=5, ln1_g=ln_g[l, 1], ln1_b=ln_b[l, 1])

        to_rows = (0, 3, 1, 2)
        new = [pak.transpose(to_rows), pav.transpose(to_rows), pbk.transpose(to_rows), pbv.transpose(to_rows),
               sak.reshape(n_dec, 1, N_HEADS_A, HEAD_DIM), sav.reshape(n_dec, 1, N_HEADS_A, HEAD_DIM),
               sbk.reshape(n_dec, 1, N_KV_B, HEAD_DIM), sbv.reshape(n_dec, 1, N_KV_B, HEAD_DIM)]
        for acc, arr in zip(outs, new):
            acc.append(arr)

    return (xp.reshape(batch, seq, d), xs.reshape(n_dec, 1, d)) + tuple(jnp.stack(o) for o in outs)
```

```python
import functools
import math

import jax
import jax.numpy as jnp
import numpy as np
from jax import lax
from jax.experimental import pallas as pl
from jax.experimental.pallas import tpu as pltpu

HEAD_DIM = 64
N_HEADS_A = 8
N_HEADS_B = 8
N_KV_B = 2
A_PATTERNS = ((128, 1), (512, 4), (2048, 16))
B_WINDOW = 128
BLK = 128
NUM_BUCKETS = 32
MAX_DISTANCE = 2048
LN_EPS = 1e-5
NEG = -1e30
SCALE = HEAD_DIM ** -0.5

LANES = 128
V7X_VMEM_BYTES = 64 * 1024 * 1024
A_COLS = N_HEADS_A * HEAD_DIM
N_PAIRS = A_COLS // LANES
N_MOD = 9

F32 = jnp.float32
BF16 = jnp.bfloat16


def _vmem_limit(pipelined_bytes, scratch_bytes=0, temp_bytes=0):
    want = 2 * pipelined_bytes + scratch_bytes + temp_bytes + (4 << 20)
    return int(min(want, V7X_VMEM_BYTES - (6 << 20)))


def _resident_spec(shape):
    return pl.BlockSpec(shape, lambda *_: (0,) * len(shape), pipeline_mode=pl.Buffered(1))


def _nbytes(shape, dtype):
    return math.prod(shape) * jnp.dtype(dtype).itemsize


def _layer_norm(x, g, b):
    mu = jnp.mean(x, axis=-1, keepdims=True)
    xc = x - mu
    var = jnp.mean(xc * xc, axis=-1, keepdims=True)
    return xc * lax.rsqrt(var + LN_EPS) * g + b


def _silu(x):
    return x * jax.nn.sigmoid(x)


def _ada_kernel(c_ref, w_ref, b_ref, o_ref):
    a = _silu(c_ref[...]).astype(BF16)
    o_ref[...] = jnp.dot(a, w_ref[...].astype(BF16), preferred_element_type=F32) + b_ref[...]


def _ada_call(c_all, w_ada, b_ada):
    m, d = c_all.shape
    n = w_ada.shape[1]
    tn = n // 8
    blocks = _nbytes((m, d), F32) + _nbytes((d, tn), F32) + _nbytes((m, tn), F32)
    return pl.pallas_call(
        _ada_kernel,
        grid=(n // tn,),
        in_specs=[pl.BlockSpec((m, d), lambda j: (0, 0)),
                  pl.BlockSpec((d, tn), lambda j: (0, j)),
                  pl.BlockSpec((1, tn), lambda j: (0, j))],
        out_specs=pl.BlockSpec((m, tn), lambda j: (0, j)),
        out_shape=jax.ShapeDtypeStruct((m, n), F32),
        compiler_params=pltpu.CompilerParams(
            dimension_semantics=("arbitrary",),
            vmem_limit_bytes=_vmem_limit(blocks, temp_bytes=_nbytes((d, tn), BF16))),
        name="ada_mod",
    )(c_all, w_ada, b_ada.reshape(1, n))


def _t5_bucket(n):
    max_exact = NUM_BUCKETS // 2
    nf = np.maximum(n, 1).astype(np.float32)
    large = max_exact + (np.log(nf / max_exact) / math.log(MAX_DISTANCE / max_exact)
                         * (NUM_BUCKETS - max_exact)).astype(np.int32)
    return np.where(n < max_exact, n, np.minimum(large, NUM_BUCKETS - 1))


def _bucket_maps(a_rows, b_rows):
    steps = [d for _, d in A_PATTERNS] + [1]
    q = np.arange(BLK)[:, None]
    k = np.arange(2 * BLK)[None, :]
    dist = q + BLK - k
    valid = (dist >= 0) & (dist <= BLK)
    band = np.stack([np.where(valid, _t5_bucket(np.maximum(dist, 0) * s), -1) for s in steps])

    def cached(rows, step):
        back = rows - np.arange(rows)
        hit = (back % step == 0) & (back // step <= BLK)
        return np.where(hit, _t5_bucket(back), -1)[None, :].astype(np.int32)

    cached_a = np.stack([cached(a_rows, s) for s in steps[:-1]])
    return band.astype(np.int32), cached_a, cached(b_rows, 1)


def _band_bias_kernel(rel_ref, bmap_ref, o_ref):
    t = pl.program_id(0)
    hp = pl.program_id(1)
    bm = bmap_ref[...]
    head0 = jnp.where(t == len(A_PATTERNS), N_HEADS_A, 0) + 2 * hp
    for half in range(2):
        acc = jnp.full(bm.shape, NEG, F32)
        for b in range(NUM_BUCKETS):
            acc = jnp.where(bm == b, rel_ref[b, head0 + half], acc)
        o_ref[half * BLK:(half + 1) * BLK, :] = acc


def _band_bias_call(rel_bias, band_map):
    n_tab = band_map.shape[0]
    return pl.pallas_call(
        _band_bias_kernel,
        grid=(n_tab, N_PAIRS),
        in_specs=[pl.BlockSpec(memory_space=pltpu.SMEM),
                  pl.BlockSpec((None, BLK, 2 * BLK), lambda t, hp: (t, 0, 0))],
        out_specs=pl.BlockSpec((None, None, 2 * BLK, 2 * BLK), lambda t, hp: (t, hp, 0, 0)),
        out_shape=jax.ShapeDtypeStruct((n_tab, N_PAIRS, 2 * BLK, 2 * BLK), F32),
        compiler_params=pltpu.CompilerParams(dimension_semantics=("arbitrary", "arbitrary")),
        name="band_bias",
    )(rel_bias, band_map)


def _cached_bias_kernel(relt_ref, amap_ref, bmap_ref, oa_ref, ob_ref):
    relt = relt_ref[...]

    def table(bm, heads):
        acc = jnp.full(bm.shape, NEG, F32)
        for b in range(NUM_BUCKETS):
            acc = jnp.where(bm == b, heads[:, b:b + 1], acc)
        return acc

    for t in range(amap_ref.shape[0]):
        oa_ref[t] = table(jnp.broadcast_to(amap_ref[t], oa_ref.shape[1:]), relt[:N_HEADS_A])
    ob_ref[...] = table(jnp.broadcast_to(bmap_ref[...], ob_ref.shape), relt[N_HEADS_A:])


def _cached_bias_call(rel_bias, cached_a, cached_b):
    return pl.pallas_call(
        _cached_bias_kernel,
        out_shape=(jax.ShapeDtypeStruct((cached_a.shape[0], N_HEADS_A, cached_a.shape[2]), F32),
                   jax.ShapeDtypeStruct((N_HEADS_B, cached_b.shape[1]), F32)),
        name="cached_bias",
    )(rel_bias.T, cached_a, cached_b)


def _ffn_kernel(*refs, alpha, pre_mix, mix_slots, n_chunks, fc, n_side, sample0):
    refs = list(refs)
    x_ref = refs.pop(0)
    if pre_mix:
        mix_refs = [refs.pop(0) for _ in range(2 if mix_slots else 1)]
        wo_ref, gm_ref, ln1g_ref, ln1b_ref = (refs.pop(0) for _ in range(4))
    sh_ref, sc_ref, gt_ref, wg_ref, wu_ref, wd_ref, lng_ref, lnb_ref = (refs.pop(0) for _ in range(8))
    if n_side:
        side_in = [refs.pop(0) for _ in range(10)]
        ak_hbm, av_hbm, o_ref, oc_ref, h_ref, acc_ref, kbuf, vbuf, sem = (refs.pop(0) for _ in range(9))
        side_scratch = refs
        end = sample0 + pl.num_programs(0) * n_side
        first = sample0 + pl.program_id(0) * n_side

        def cache_copies(g, slot):
            return (pltpu.make_async_copy(ak_hbm.at[g], kbuf.at[slot], sem.at[0, slot]),
                    pltpu.make_async_copy(av_hbm.at[g], vbuf.at[slot], sem.at[1, slot]))

        @pl.when(pl.program_id(0) == 0)
        def _():
            for cp in cache_copies(sample0, 0):
                cp.start()

        def side(j):
            g, slot = first + j, j % 2
            for cp in cache_copies(g, slot):
                cp.wait()

            @pl.when(g + 1 < end)
            def _():
                for cp in cache_copies(g + 1, 1 - slot):
                    cp.start()

            _cached_sample(j, kbuf.at[slot], vbuf.at[slot], *side_in, oc_ref, *side_scratch)

        assert n_side % 2 == 0
        side_at = {(j * n_chunks) // n_side: j for j in range(n_side)}
    else:
        o_ref, h_ref, acc_ref = refs
        side_at = {}

    x = x_ref[...]
    if pre_mix:
        if mix_slots:
            mixed = jnp.concatenate([r[j] for r in mix_refs for j in range(N_PAIRS)], axis=-1)
        else:
            mixed = mix_refs[0][...].astype(BF16)
        y = jnp.dot(mixed, wo_ref[...], preferred_element_type=F32)
        x = _layer_norm(alpha * x + gm_ref[...] * y, ln1g_ref[...], ln1b_ref[...])
    h_ref[...] = (x * (1.0 + sc_ref[...]) + sh_ref[...]).astype(BF16)
    for c in range(n_chunks):
        if c in side_at:
            side(side_at[c])
        cols = slice(c * fc, (c + 1) * fc)
        g = jnp.dot(h_ref[...], wg_ref[:, cols], preferred_element_type=F32)
        u = jnp.dot(h_ref[...], wu_ref[:, cols], preferred_element_type=F32)
        y = jnp.dot((_silu(g) * u).astype(BF16), wd_ref[cols, :], preferred_element_type=F32)
        if c == 0:
            acc_ref[...] = y
        else:
            acc_ref[...] += y
    o_ref[...] = _layer_norm(alpha * x + 0.5 * gt_ref[...] * acc_ref[...], lng_ref[...], lnb_ref[...])


def _ffn_call(x, mod2, mod3, mod_ks, seq0, rows_per_seq, wg, wu, wd, ln_g, ln_b, alpha, tm, fc,
              mix=None, w_out=None, gate_k=None, ln1_g=None, ln1_b=None, side=None):
    n, d = x.shape
    d_ff = wg.shape[1]
    n_chunks = d_ff // fc
    assert n_chunks * fc == d_ff
    per_row = rows_per_seq == 1
    tiles_per_seq = max(rows_per_seq // tm, 1)

    def mod_spec(k):
        if per_row:
            return pl.BlockSpec((tm, d), lambda i, k=k: (seq0 // tm + i, k))
        return pl.BlockSpec((None, 1, d), lambda i, k=k: ((seq0 + i // tiles_per_seq) * N_MOD + k, 0, 0))

    mod_arr = mod2 if per_row else mod3
    row_spec = pl.BlockSpec((tm, d), lambda i: (i, 0))
    vec_spec = pl.BlockSpec((1, d), lambda i: (0, 0))
    args, specs = [x], [row_spec]
    pre_mix = mix is not None
    mix_slots = pre_mix and isinstance(mix, tuple)
    blocks = 2 * _nbytes((tm, d), F32) + (3 * _nbytes((tm, d), F32) if per_row else 0)
    resident = 3 * _nbytes(wg.shape, BF16)
    if pre_mix:
        if mix_slots:
            for mm in mix:
                args.append(mm)
                specs.append(pl.BlockSpec((N_PAIRS, tm, LANES), lambda i: (0, i, 0)))
        else:
            args.append(mix)
            specs.append(row_spec)
        args += [w_out, mod_arr, ln1_g.reshape(1, d), ln1_b.reshape(1, d)]
        specs += [_resident_spec((d, d)), mod_spec(gate_k), vec_spec, vec_spec]
        blocks += _nbytes((tm, d), F32)
        resident += _nbytes((d, d), BF16)
    args += [mod_arr, mod_arr, mod_arr, wg, wu, wd, ln_g.reshape(1, d), ln_b.reshape(1, d)]
    specs += [mod_spec(mod_ks[0]), mod_spec(mod_ks[1]), mod_spec(mod_ks[2]),
              _resident_spec(wg.shape), _resident_spec(wu.shape), _resident_spec(wd.shape), vec_spec, vec_spec]
    scratch = _nbytes((tm, d), BF16) + _nbytes((tm, d), F32) + resident
    temps = 6 * _nbytes((tm, fc), F32) + 3 * _nbytes((tm, d), F32)
    out_specs, out_shape = row_spec, jax.ShapeDtypeStruct((n, d), F32)
    scratch_shapes = [pltpu.VMEM((tm, d), BF16), pltpu.VMEM((tm, d), F32)]
    n_side = sample0 = 0
    if side is not None:
        (*small, ak, av), sample0, count = side
        n_steps = n // tm
        n_side = count // n_steps
        assert n_side * n_steps == count and sample0 % n_side == 0
        blk0 = sample0 // n_side
        for a in small[:6]:
            args.append(a)
            specs.append(pl.BlockSpec((n_side,) + a.shape[1:],
                                      lambda i, nd=a.ndim: (blk0 + i,) + (0,) * (nd - 1)))
            blocks += n_side * _nbytes(a.shape[1:-1] + (max(a.shape[-1], LANES),), F32)
        for a in small[6:]:
            args.append(a)
            specs.append(pl.BlockSpec(a.shape, lambda i, nd=a.ndim: (0,) * nd))
            blocks += _nbytes(a.shape, F32)
        args += [ak, av]
        specs += [pl.BlockSpec(memory_space=pl.ANY)] * 2
        q3 = small[0]
        out_specs = [row_spec, pl.BlockSpec((n_side,) + q3.shape[1:], lambda i: (i, 0, 0))]
        out_shape = [out_shape, jax.ShapeDtypeStruct((count,) + q3.shape[1:], F32)]
        sa_shape = (n_side, N_HEADS_A, ak.shape[3])
        sb_shape = (n_side, N_HEADS_B, small[4].shape[3])
        scratch_shapes += [pltpu.VMEM((2,) + ak.shape[1:], F32), pltpu.VMEM((2,) + av.shape[1:], F32),
                           pltpu.SemaphoreType.DMA((2, 2)),
                           pltpu.VMEM(sa_shape, F32), pltpu.VMEM(sb_shape, F32),
                           pltpu.VMEM(sa_shape, F32), pltpu.VMEM(sb_shape, F32)]
        scratch += 4 * _nbytes(ak.shape[1:], F32) + 2 * _nbytes(sa_shape, F32) + 2 * _nbytes(sb_shape, F32)
    return pl.pallas_call(
        functools.partial(_ffn_kernel, alpha=alpha, pre_mix=pre_mix, mix_slots=mix_slots,
                          n_chunks=n_chunks, fc=fc, n_side=n_side, sample0=sample0),
        grid=(n // tm,),
        in_specs=specs,
        out_specs=out_specs,
        out_shape=out_shape,
        scratch_shapes=scratch_shapes,
        compiler_params=pltpu.CompilerParams(
            dimension_semantics=("arbitrary",),
            vmem_limit_bytes=_vmem_limit(blocks, scratch, temps)),
        name=("ffn_mix" if pre_mix else "ffn") + ("_cached" if n_side else ""),
    )(*args)


def _proj_prompt_kernel(x_ref, sh_ref, sc_ref, w_ref, p_ref, p4_ref, p16_ref, ak_ref, av_ref, bk_ref, bv_ref,
                        stage_ref, stage4_ref, *, tm, tiles_per_seq, a_tiles, b_rows):
    t = pl.program_id(0) % tiles_per_seq
    h = (x_ref[...] * (1.0 + sc_ref[...]) + sh_ref[...]).astype(BF16)
    n_groups = w_ref.shape[1] // A_COLS
    lane_lo = lax.broadcasted_iota(jnp.int32, (b_rows, LANES), 1) < HEAD_DIM
    for g in range(n_groups):
        res = jnp.dot(h, w_ref[:, g * A_COLS:(g + 1) * A_COLS], preferred_element_type=F32)
        scaled = res * SCALE if g in (0, 3) else res
        for s in range(N_PAIRS):
            slab = scaled[:, s * LANES:(s + 1) * LANES]
            p_ref[g * N_PAIRS + s] = slab.astype(BF16)
            if g < 3:
                slot = g * N_PAIRS + s
                d4, d16 = p4_ref.shape[1], p16_ref.shape[1]
                stage_ref[slot] = slab
                for r in range(d4):
                    rows = stage_ref[slot, pl.ds(r, tm // d4, stride=d4), :]
                    p4_ref[slot, r] = rows.astype(BF16)
                    stage4_ref[slot, r] = rows
                for r in range(d16):
                    rows = stage4_ref[slot, r % d4, pl.ds(r // d4, tm // d16, stride=d16 // d4), :]
                    p16_ref[slot, r] = rows.astype(BF16)
        if g in (1, 2):
            out_ref = ak_ref if g == 1 else av_ref

            @pl.when(t >= tiles_per_seq - a_tiles)
            def _(res=res, out_ref=out_ref):
                out_ref[...] = res.T.reshape(out_ref.shape)
        if g == 4:
            @pl.when(t == tiles_per_seq - 1)
            def _(res=res):
                tail = res[tm - b_rows:, :]
                kb = jnp.where(lane_lo, tail[:, 0:LANES], tail[:, LANES:2 * LANES])
                vb = jnp.where(lane_lo, tail[:, 2 * LANES:3 * LANES], tail[:, 3 * LANES:4 * LANES])
                bk_ref[...] = kb.T.reshape(bk_ref.shape)
                bv_ref[...] = vb.T.reshape(bv_ref.shape)


def _proj_prompt_call(x, mod3, seq0, seq, w_ext, a_rows, b_rows, tm):
    n, d = x.shape
    batch = n // seq
    tiles_per_seq = seq // tm
    a_tiles = a_rows // tm
    n_slots = w_ext.shape[1] // LANES

    def mod_spec(k):
        return pl.BlockSpec((None, 1, d), lambda i: ((seq0 + i // tiles_per_seq) * N_MOD + k, 0, 0))

    def a_map(i):
        return (i // tiles_per_seq, 0, 0, jnp.maximum(i % tiles_per_seq - (tiles_per_seq - a_tiles), 0))

    a_blk = (None, N_HEADS_A, HEAD_DIM, tm)
    b_blk = (None, N_KV_B, HEAD_DIM, b_rows)
    a_slots = 3 * N_PAIRS
    d4, d16 = A_PATTERNS[1][1], A_PATTERNS[2][1]
    assert tm % (16 * d16) == 0

    def dil_spec(dil):
        return pl.BlockSpec((a_slots, None, dil, tm // dil, LANES),
                            lambda i: (0, i // tiles_per_seq, 0, i % tiles_per_seq, 0))

    blocks = (_nbytes((tm, d), F32) + _nbytes(w_ext.shape, BF16) + _nbytes((n_slots + 2 * a_slots, tm, LANES), BF16)
              + 2 * _nbytes((tm, A_COLS), F32) + 2 * _nbytes((b_rows, LANES), F32))
    return pl.pallas_call(
        functools.partial(_proj_prompt_kernel, tm=tm, tiles_per_seq=tiles_per_seq, a_tiles=a_tiles,
                          b_rows=b_rows),
        grid=(n // tm,),
        in_specs=[pl.BlockSpec((tm, d), lambda i: (i, 0)), mod_spec(3), mod_spec(4),
                  pl.BlockSpec(w_ext.shape, lambda i: (0, 0))],
        out_specs=[pl.BlockSpec((n_slots, tm, LANES), lambda i: (0, i, 0)),
                   dil_spec(d4), dil_spec(d16),
                   pl.BlockSpec(a_blk, a_map),
                   pl.BlockSpec(a_blk, a_map),
                   pl.BlockSpec(b_blk, lambda i: (i // tiles_per_seq, 0, 0, 0)),
                   pl.BlockSpec(b_blk, lambda i: (i // tiles_per_seq, 0, 0, 0))],
        out_shape=[jax.ShapeDtypeStruct((n_slots, n, LANES), BF16),
                   jax.ShapeDtypeStruct((a_slots, batch, d4, seq // d4, LANES), BF16),
                   jax.ShapeDtypeStruct((a_slots, batch, d16, seq // d16, LANES), BF16),
                   jax.ShapeDtypeStruct((batch, N_HEADS_A, HEAD_DIM, a_rows), F32),
                   jax.ShapeDtypeStruct((batch, N_HEADS_A, HEAD_DIM, a_rows), F32),
                   jax.ShapeDtypeStruct((batch, N_KV_B, HEAD_DIM, b_rows), F32),
                   jax.ShapeDtypeStruct((batch, N_KV_B, HEAD_DIM, b_rows), F32)],
        scratch_shapes=[pltpu.VMEM((a_slots, tm, LANES), F32), pltpu.VMEM((a_slots, d4, tm // d4, LANES), F32)],
        compiler_params=pltpu.CompilerParams(
            dimension_semantics=("arbitrary",),
            vmem_limit_bytes=_vmem_limit(blocks, 2 * _nbytes((a_slots, tm, LANES), F32),
                                         4 * _nbytes((tm, A_COLS), F32))),
        name="proj_prompt",
    )(x, mod3, mod3, w_ext)


def _proj_sample_kernel(x_ref, sh_ref, sc_ref, w_ref, o_ref):
    h = (x_ref[...] * (1.0 + sc_ref[...]) + sh_ref[...]).astype(BF16)
    o_ref[...] = jnp.dot(h, w_ref[...], preferred_element_type=F32)


def _proj_sample_call(x, mod2, w_ext):
    n, d = x.shape
    cols = w_ext.shape[1]
    blocks = 3 * _nbytes((n, d), F32) + _nbytes(w_ext.shape, BF16) + _nbytes((n, cols), F32)
    return pl.pallas_call(
        _proj_sample_kernel,
        grid=(1,),
        in_specs=[pl.BlockSpec((n, d), lambda i: (0, 0)),
                  pl.BlockSpec((n, d), lambda i: (0, 3)),
                  pl.BlockSpec((n, d), lambda i: (0, 4)),
                  pl.BlockSpec(w_ext.shape, lambda i: (0, 0))],
        out_specs=pl.BlockSpec((n, cols), lambda i: (0, 0)),
        out_shape=jax.ShapeDtypeStruct((n, cols), F32),
        compiler_params=pltpu.CompilerParams(
            dimension_semantics=("arbitrary",), vmem_limit_bytes=_vmem_limit(blocks)),
        name="proj_sample",
    )(x, mod2, mod2, w_ext)


def _pair_scores(q, kk, bias, lane_lo):
    zero = jnp.zeros_like(q)
    q2 = jnp.concatenate([jnp.where(lane_lo, q, zero), jnp.where(lane_lo, zero, q)], axis=0)
    s = lax.dot_general(q2, kk, (((1,), (1,)), ((), ())), preferred_element_type=F32)
    return s + bias


def _pair_select(a, lane_lo):
    return jnp.where(lane_lo, a[:BLK], a[BLK:])


GROUP = 8


def _band_sequence(q_ref, k_ref, v_ref, n_blocks, load_bias, block, prefetch=lambda jb: None):
    def one(jb, first, pre):
        if first:
            block(jb, q_ref[0:BLK, :], k_ref[0:BLK, :], v_ref[0:BLK, :], load_bias(True), pre)
        else:
            r0 = jb * BLK
            if not isinstance(jb, int):
                r0 = pl.multiple_of(r0, BLK)
            block(jb, q_ref[pl.ds(r0, BLK), :], k_ref[pl.ds(r0 - BLK, 2 * BLK), :],
                  v_ref[pl.ds(r0 - BLK, 2 * BLK), :], load_bias(False), pre)

    def run(jbs, has_first):
        pres = [prefetch(jb) for jb in jbs]
        for j, (jb, pre) in enumerate(zip(jbs, pres)):
            one(jb, has_first and j == 0, pre)

    run(list(range(min(GROUP, n_blocks))), True)
    if n_blocks > GROUP:
        assert n_blocks % GROUP == 0

        def group(g, carry):
            run([g * GROUP + j for j in range(GROUP)], False)
            return carry

        lax.fori_loop(1, n_blocks // GROUP, group, 0)


def _attn_a_kernel(q16_ref, k16_ref, v16_ref, q4_ref, k4_ref, v4_ref, q1_ref, k1_ref, v1_ref,
                   bias_ref, o_ref, acc16_ref, m16_ref, den16_ref, acc4_ref, m4_ref, den4_ref, *, seq):
    lane_lo = lax.broadcasted_iota(jnp.int32, (BLK, LANES), 1) < HEAD_DIM
    state16 = (acc16_ref, m16_ref, den16_ref)
    state4 = (acc4_ref, m4_ref, den4_ref)

    def update(q, kk, vv, bias, rows, old, new):
        s = _pair_scores(q, kk, bias, lane_lo)
        part = s[:, :LANES] if s.shape[1] == LANES else jnp.maximum(s[:, :LANES], s[:, LANES:])
        if old is not None:
            acc_old, m_old, den_old = old
            part = jnp.maximum(part, jnp.concatenate([jnp.where(lane_lo, m_old, NEG),
                                                      jnp.where(lane_lo, NEG, m_old)], axis=0))
        m2 = jnp.max(part, axis=-1, keepdims=True)
        p = jnp.exp(s - m2)
        l2 = jnp.sum(p, axis=-1, keepdims=True)
        pv = _pair_select(jnp.dot(p.astype(BF16), vv, preferred_element_type=F32), lane_lo)
        m_new = _pair_select(m2, lane_lo)
        den = _pair_select(l2, lane_lo)
        if old is None:
            acc = pv
        else:
            a = jnp.exp(m_old - m_new)
            acc = a * acc_old + pv
            den = a * den_old + den
        if new is None:
            o_ref[rows, :] = (acc / den).astype(o_ref.dtype)
        else:
            new[0][rows, :] = acc
            new[1][rows, :] = m_new
            new[2][rows, :] = den

    def branch(q_ref, k_ref, v_ref, table, dil, r, old, new):
        def rows(jb):
            start = r + jb * (BLK * dil)
            return pl.ds(start, BLK, stride=dil) if dil > 1 else pl.ds(start, BLK)

        def prefetch(jb):
            return None if old is None else tuple(ref[rows(jb), :] for ref in old)

        def block(jb, q, kk, vv, bias, pre):
            update(q, kk, vv, bias, rows(jb), pre, new)

        def load_bias(first):
            return bias_ref[table, :, BLK:2 * BLK] if first else bias_ref[table]

        _band_sequence(q_ref, k_ref, v_ref, q_ref.shape[0] // BLK, load_bias, block, prefetch)

    d16 = q16_ref.shape[0]
    per_group = max(GROUP // (q16_ref.shape[1] // BLK), 1)
    assert d16 % per_group == 0

    def group16(g, carry):
        for rr in range(per_group):
            r = g * per_group + rr
            branch(q16_ref.at[r], k16_ref.at[r], v16_ref.at[r], 2, d16, r, None, state16)
        return carry

    lax.fori_loop(0, d16 // per_group, group16, 0)

    d4 = q4_ref.shape[0]

    def residue4(r, carry):
        branch(q4_ref.at[r], k4_ref.at[r], v4_ref.at[r], 1, d4, r, state16, state4)
        return carry

    lax.fori_loop(0, d4, residue4, 0)
    branch(q1_ref, k1_ref, v1_ref, 0, 1, 0, state4, None)


def _attn_a_call(p_slots, p4, p16, bias_a, batch, seq):
    n_slots = p_slots.shape[0]
    dils = [d for _, d in A_PATTERNS]
    assert dils == [1, p4.shape[2], p16.shape[2]] and seq % (dils[2] * BLK) == 0
    view1 = p_slots.reshape(n_slots, batch, seq, LANES)

    in_specs, args = [], []
    for arr in (p16, p4, view1):
        for slot0 in (0, N_PAIRS, 2 * N_PAIRS):
            in_specs.append(pl.BlockSpec((None, None) + arr.shape[2:],
                                         lambda b, hp, slot0=slot0, nd=arr.ndim: (slot0 + hp, b) + (0,) * (nd - 2)))
            args.append(arr)
    in_specs.append(pl.BlockSpec((len(A_PATTERNS), None, 2 * BLK, 2 * BLK), lambda b, hp: (0, hp, 0, 0)))
    args.append(bias_a)
    blocks = 10 * _nbytes((seq, LANES), BF16) + _nbytes((3, 2 * BLK, 2 * BLK), F32)
    scratch = 6 * _nbytes((seq, LANES), F32)
    return pl.pallas_call(
        functools.partial(_attn_a_kernel, seq=seq),
        grid=(batch, N_PAIRS),
        in_specs=in_specs,
        out_specs=pl.BlockSpec((None, seq, LANES), lambda b, hp: (hp, b, 0)),
        out_shape=jax.ShapeDtypeStruct((N_PAIRS, batch * seq, LANES), BF16),
        scratch_shapes=[pltpu.VMEM((seq, LANES), F32)] * 6,
        compiler_params=pltpu.CompilerParams(
            dimension_semantics=("arbitrary", "arbitrary"),
            vmem_limit_bytes=_vmem_limit(blocks, scratch, 4 * GROUP * _nbytes((2 * BLK, 2 * BLK), F32))),
        name="attn_dilated",
    )(*args)


def _attn_b_kernel(sink_ref, q_ref, k_ref, v_ref, bias_ref, o_ref, *, seq):
    hp = pl.program_id(1)
    lane_lo = lax.broadcasted_iota(jnp.int32, (BLK, LANES), 1) < HEAD_DIM
    row_lo = lax.broadcasted_iota(jnp.int32, (2 * BLK, 1), 0) < BLK
    sink2 = jnp.where(row_lo, sink_ref[2 * hp], sink_ref[2 * hp + 1])

    def block(jb, q, kk, vv, bias, pre):
        s = _pair_scores(q, kk, bias, lane_lo)
        m2 = jnp.maximum(jnp.max(s, axis=-1, keepdims=True), sink2)
        p = jnp.exp(s - m2)
        l2 = jnp.sum(p, axis=-1, keepdims=True) + jnp.exp(sink2 - m2)
        pv = _pair_select(jnp.dot(p.astype(BF16), vv, preferred_element_type=F32), lane_lo)
        r0 = jb * BLK if isinstance(jb, int) else pl.multiple_of(jb * BLK, BLK)
        o_ref[pl.ds(r0, BLK), :] = (pv / _pair_select(l2, lane_lo)).astype(o_ref.dtype)

    def load_bias(first):
        return bias_ref[:, BLK:2 * BLK] if first else bias_ref[...]

    _band_sequence(q_ref, k_ref, v_ref, seq // BLK, load_bias, block)


def _attn_b_call(p_slots, bias_b, sinks, batch, seq):
    n_slots = p_slots.shape[0]
    view = p_slots.reshape(n_slots, batch, seq, LANES)
    q0, k0, v0 = 3 * N_PAIRS, 4 * N_PAIRS, 4 * N_PAIRS + N_KV_B
    pairs_per_kv = N_PAIRS // N_KV_B
    blk = (None, None, seq, LANES)
    blocks = 4 * _nbytes((seq, LANES), BF16) + _nbytes((2 * BLK, 2 * BLK), F32)
    return pl.pallas_call(
        functools.partial(_attn_b_kernel, seq=seq),
        grid=(batch, N_PAIRS),
        in_specs=[pl.BlockSpec(memory_space=pltpu.SMEM),
                  pl.BlockSpec(blk, lambda b, hp: (q0 + hp, b, 0, 0)),
                  pl.BlockSpec(blk, lambda b, hp: (k0 + hp // pairs_per_kv, b, 0, 0)),
                  pl.BlockSpec(blk, lambda b, hp: (v0 + hp // pairs_per_kv, b, 0, 0)),
                  pl.BlockSpec((None, 2 * BLK, 2 * BLK), lambda b, hp: (hp, 0, 0))],
        out_specs=pl.BlockSpec((None, seq, LANES), lambda b, hp: (hp, b, 0)),
        out_shape=jax.ShapeDtypeStruct((N_PAIRS, batch * seq, LANES), BF16),
        compiler_params=pltpu.CompilerParams(
            dimension_semantics=("arbitrary", "arbitrary"),
            vmem_limit_bytes=_vmem_limit(blocks, temp_bytes=8 * _nbytes((2 * BLK, 2 * BLK), F32))),
        name="attn_window",
    )(sinks, view, view, view, bias_b)


def _cached_sample(i, ak_ref, av_ref, q_ref, qt_ref, knt_ref, vn_ref, bk_ref, bv_ref, bias_a_ref, bias_b_ref,
                   relt_ref, sink_ref, o_ref, sa_ref, sb_ref, wa_ref, wb_ref):
    n_branches = bias_a_ref.shape[0]
    heads_per_kv = N_HEADS_B // N_KV_B
    bias0 = relt_ref[:, 0:1]
    sink = sink_ref[...]
    q = q_ref[i] * SCALE
    vn = vn_ref[i]
    s0 = jnp.sum(qt_ref[i] * SCALE * knt_ref[i], axis=-1, keepdims=True) + bias0
    for h in range(N_HEADS_A):
        sa_ref[i, h:h + 1, :] = jnp.sum(ak_ref[h] * q[:, h:h + 1], axis=0, keepdims=True)
    for hb in range(N_HEADS_B):
        h = N_HEADS_A + hb
        sb_ref[i, hb:hb + 1, :] = jnp.sum(bk_ref[i, hb // heads_per_kv] * q[:, h:h + 1], axis=0, keepdims=True)

    s0a = s0[:N_HEADS_A]
    ts = [sa_ref[i] + bias_a_ref[t] for t in range(n_branches)]
    m = s0a
    for t_ in ts:
        m = jnp.maximum(m, jnp.max(t_, axis=-1, keepdims=True))
    w = jnp.exp(ts[0] - m)
    for t_ in ts[1:]:
        w = w + jnp.exp(t_ - m)
    p0a = n_branches * jnp.exp(s0a - m)
    inv = 1.0 / (jnp.sum(w, axis=-1, keepdims=True) + p0a)
    wa_ref[i] = w * inv
    p0a = p0a * inv

    s0b = s0[N_HEADS_A:]
    tb = sb_ref[i] + bias_b_ref[...]
    mb = jnp.maximum(jnp.maximum(s0b, sink), jnp.max(tb, axis=-1, keepdims=True))
    wb = jnp.exp(tb - mb)
    p0b = jnp.exp(s0b - mb)
    invb = 1.0 / (jnp.sum(wb, axis=-1, keepdims=True) + p0b + jnp.exp(sink - mb))
    wb_ref[i] = wb * invb
    p0b = p0b * invb

    cols = []
    for h in range(N_HEADS_A):
        acc = jnp.sum(av_ref[h] * wa_ref[i, h:h + 1, :], axis=-1, keepdims=True)
        cols.append(acc + p0a[h:h + 1, :] * vn[:, h:h + 1])
    for hb in range(N_HEADS_B):
        h = N_HEADS_A + hb
        acc = jnp.sum(bv_ref[i, hb // heads_per_kv] * wb_ref[i, hb:hb + 1, :], axis=-1, keepdims=True)
        cols.append(acc + p0b[hb:hb + 1, :] * vn[:, h:h + 1])
    o_ref[i] = jnp.concatenate(cols, axis=1)


def _dup_cols(w):
    parts = []
    for j in range(2 * N_KV_B):
        col = w[:, j * HEAD_DIM:(j + 1) * HEAD_DIM]
        parts += [col, col]
    return jnp.concatenate(parts, axis=1)


def kernel(x_prompt, x_sample, cache_a_k, cache_a_v, cache_b_k, cache_b_v, c_prompt, c_sample, rel_bias, w_ada, b_ada, ffn1_wg, ffn1_wu, ffn1_wd, w_in, w_out, sinks, ffn2_wg, ffn2_wu, ffn2_wd, ln_g, ln_b):
    batch, seq, d = x_prompt.shape
    n_dec, dec_seq, _ = x_sample.shape
    depth = w_ada.shape[0]
    d_ff = ffn1_wg.shape[2]
    assert dec_seq == 1 and d == 2 * A_COLS and all(w // dl == BLK for w, dl in A_PATTERNS)
    alpha = (2 * depth) ** 0.25
    fc = 256
    tm = 512
    a_rows, b_rows = min(A_PATTERNS[-1][0], seq), min(B_WINDOW, seq)
    q_cols = 4 * A_COLS
    kvb = N_KV_B * HEAD_DIM
    heads_per_kv = N_HEADS_B // N_KV_B

    band_map, cached_a, cached_b = _bucket_maps(cache_a_k.shape[2], cache_b_k.shape[2])
    band_bias = _band_bias_call(rel_bias, band_map)
    bias_a, bias_b = band_bias[:len(A_PATTERNS)], band_bias[len(A_PATTERNS)]
    bias_sa, bias_sb = _cached_bias_call(rel_bias, cached_a, cached_b)

    pad = (-(n_dec + batch)) % 16
    c_all = jnp.concatenate([c_sample, c_prompt, jnp.zeros((pad, d), F32)], axis=0)
    xp = x_prompt.reshape(batch * seq, d)
    xs = x_sample.reshape(n_dec, d)
    outs = [[] for _ in range(8)]

    def head_rows(x, n_heads):
        return x.reshape(n_dec, n_heads, HEAD_DIM)

    for l in range(depth):
        mod2 = _ada_call(c_all, w_ada[l], b_ada[l])
        mod3 = mod2.reshape(mod2.shape[0] * N_MOD, 1, d)
        ffn1 = [w[l].astype(BF16) for w in (ffn1_wg, ffn1_wu, ffn1_wd)]
        ffn2 = [w[l].astype(BF16) for w in (ffn2_wg, ffn2_wu, ffn2_wd)]
        w_ext_p = jnp.concatenate([w_in[l][:, :q_cols], _dup_cols(w_in[l][:, q_cols:])], axis=1).astype(BF16)
        w_in_s = w_in[l].astype(BF16)
        wo = w_out[l].astype(BF16)

        s1 = _ffn_call(xs, mod2, mod3, (0, 1, 2), 0, 1, *ffn1, ln_g[l, 0], ln_b[l, 0], alpha, n_dec, fc)
        proj_s = _proj_sample_call(s1, mod2, w_in_s)
        c = A_COLS
        sak, sav = proj_s[:, c:2 * c], proj_s[:, 2 * c:3 * c]
        sbk, sbv = proj_s[:, q_cols:q_cols + kvb], proj_s[:, q_cols + kvb:q_cols + 2 * kvb]
        qt3 = jnp.concatenate([head_rows(proj_s[:, :c], N_HEADS_A),
                               head_rows(proj_s[:, 3 * c:q_cols], N_HEADS_B)], axis=1)
        knt3 = jnp.concatenate([head_rows(sak, N_HEADS_A),
                                jnp.repeat(head_rows(sbk, N_KV_B), heads_per_kv, axis=1)], axis=1)
        vnt3 = jnp.concatenate([head_rows(sav, N_HEADS_A),
                                jnp.repeat(head_rows(sbv, N_KV_B), heads_per_kv, axis=1)], axis=1)
        to_cols = (0, 2, 3, 1)
        decode_side = (qt3.transpose(0, 2, 1), qt3, knt3, vnt3.transpose(0, 2, 1),
                       cache_b_k[l].transpose(to_cols), cache_b_v[l].transpose(to_cols),
                       bias_sa, bias_sb, rel_bias.T, sinks[l].reshape(N_HEADS_B, 1),
                       cache_a_k[l].transpose(to_cols), cache_a_v[l].transpose(to_cols))

        n_first = n_dec // 2 if (n_dec // 2) % (2 * (batch * seq // tm)) == 0 else n_dec
        x1, mix3 = _ffn_call(xp, mod2, mod3, (0, 1, 2), n_dec, seq, *ffn1, ln_g[l, 0], ln_b[l, 0], alpha, tm, fc,
                             side=(decode_side, 0, n_first))
        p_slots, p4, p16, pak, pav, pbk, pbv = _proj_prompt_call(x1, mod3, n_dec, seq, w_ext_p, a_rows, b_rows, tm)
        mix_a = _attn_a_call(p_slots, p4, p16, bias_a, batch, seq)
        mix_b = _attn_b_call(p_slots, bias_b, sinks[l], batch, seq)
        xp = _ffn_call(x1, mod2, mod3, (6, 7, 8), n_dec, seq, *ffn2, ln_g[l, 2], ln_b[l, 2], alpha, tm, fc,
                       mix=(mix_a, mix_b), w_out=wo, gate_k=5, ln1_g=ln_g[l, 1], ln1_b=ln_b[l, 1],
                       side=(decode_side, n_first, n_dec - n_first) if n_first < n_dec else None)
        if n_first < n_dec:
            xp, mix3_rest = xp
            mix3 = jnp.concatenate([mix3, mix3_rest], axis=0)

        mix_s = mix3.transpose(0, 2, 1).reshape(n_dec, d)
        xs = _ffn_call(s1, mod2, mod3, (6, 7, 8), 0, 1, *ffn2, ln_g[l, 2], ln_b[l, 2], alpha, n_dec, fc,
                       mix=mix_s, w_out=wo, gate_k=5, ln1_g=ln_g[l, 1], ln1_b=ln_b[l, 1])

        to_rows = (0, 3, 1, 2)
        new = [pak.transpose(to_rows), pav.transpose(to_rows), pbk.transpose(to_rows), pbv.transpose(to_rows),
               sak.reshape(n_dec, 1, N_HEADS_A, HEAD_DIM), sav.reshape(n_dec, 1, N_HEADS_A, HEAD_DIM),
               sbk.reshape(n_dec, 1, N_KV_B, HEAD_DIM), sbv.reshape(n_dec, 1, N_KV_B, HEAD_DIM)]
        for acc, arr in zip(outs, new):
            acc.append(arr)

    return (xp.reshape(batch, seq, d), xs.reshape(n_dec, 1, d)) + tuple(jnp.stack(o) for o in outs)
```

```python
import functools
import math

import jax
import jax.numpy as jnp
import numpy as np
from jax import lax
from jax.experimental import pallas as pl
from jax.experimental.pallas import tpu as pltpu

HEAD_DIM = 64
N_HEADS_A = 8
N_HEADS_B = 8
N_KV_B = 2
A_PATTERNS = ((128, 1), (512, 4), (2048, 16))
B_WINDOW = 128
BLK = 128
NUM_BUCKETS = 32
MAX_DISTANCE = 2048
LN_EPS = 1e-5
NEG = -1e30
SCALE = HEAD_DIM ** -0.5

LANES = 128
SUBLANES = 8
V7X_VMEM_BYTES = 64 * 1024 * 1024
A_COLS = N_HEADS_A * HEAD_DIM
N_PAIRS = A_COLS // LANES
N_MOD = 9

F32 = jnp.float32
BF16 = jnp.bfloat16


def _vmem_limit(pipelined_bytes, scratch_bytes=0, temp_bytes=0):
    want = 2 * pipelined_bytes + scratch_bytes + temp_bytes + (4 << 20)
    return int(min(want, V7X_VMEM_BYTES - (6 << 20)))


def _resident_spec(shape):
    return pl.BlockSpec(shape, lambda *_: (0,) * len(shape), pipeline_mode=pl.Buffered(1))


def _nbytes(shape, dtype):
    return math.prod(shape) * jnp.dtype(dtype).itemsize


def _layer_norm(x, g, b):
    mu = jnp.mean(x, axis=-1, keepdims=True)
    xc = x - mu
    var = jnp.mean(xc * xc, axis=-1, keepdims=True)
    return xc * lax.rsqrt(var + LN_EPS) * g + b


def _silu(x):
    return x * jax.nn.sigmoid(x)


def _seq_mod_spec(seq0, tiles_per_seq, k, d):
    return pl.BlockSpec((SUBLANES, d), lambda i: ((seq0 + i // tiles_per_seq) // SUBLANES, k))


def _seq_mod(ref, mod_row):
    if mod_row is None:
        return ref[...]
    seq0, tiles_per_seq = mod_row
    return ref[pl.ds((seq0 + pl.program_id(0) // tiles_per_seq) % SUBLANES, 1), :]


def _ada_kernel(c_ref, w_ref, b_ref, o_ref):
    a = _silu(c_ref[...]).astype(BF16)
    o_ref[...] = jnp.dot(a, w_ref[...].astype(BF16), preferred_element_type=F32) + b_ref[...]


def _ada_call(c_all, w_ada, b_ada):
    m, d = c_all.shape
    n = w_ada.shape[1]
    tn = n // 8
    blocks = _nbytes((m, d), F32) + _nbytes((d, tn), F32) + _nbytes((m, tn), F32)
    return pl.pallas_call(
        _ada_kernel,
        grid=(n // tn,),
        in_specs=[pl.BlockSpec((m, d), lambda j: (0, 0)),
                  pl.BlockSpec((d, tn), lambda j: (0, j)),
                  pl.BlockSpec((1, tn), lambda j: (0, j))],
        out_specs=pl.BlockSpec((m, tn), lambda j: (0, j)),
        out_shape=jax.ShapeDtypeStruct((m, n), F32),
        compiler_params=pltpu.CompilerParams(
            dimension_semantics=("arbitrary",),
            vmem_limit_bytes=_vmem_limit(blocks, temp_bytes=_nbytes((d, tn), BF16))),
        name="ada_mod",
    )(c_all, w_ada, b_ada.reshape(1, n))


def _t5_bucket(n):
    max_exact = NUM_BUCKETS // 2
    nf = np.maximum(n, 1).astype(np.float32)
    large = max_exact + (np.log(nf / max_exact) / math.log(MAX_DISTANCE / max_exact)
                         * (NUM_BUCKETS - max_exact)).astype(np.int32)
    return np.where(n < max_exact, n, np.minimum(large, NUM_BUCKETS - 1))


def _bucket_maps(a_rows, b_rows):
    steps = [d for _, d in A_PATTERNS] + [1]
    q = np.arange(BLK)[:, None]
    k = np.arange(2 * BLK)[None, :]
    dist = q + BLK - k
    valid = (dist >= 0) & (dist <= BLK)
    band = np.stack([np.where(valid, _t5_bucket(np.maximum(dist, 0) * s), -1) for s in steps])

    def cached(rows, step):
        back = rows - np.arange(rows)
        hit = (back % step == 0) & (back // step <= BLK)
        return np.where(hit, _t5_bucket(back), -1)[None, :].astype(np.int32)

    cached_a = np.stack([cached(a_rows, s) for s in steps[:-1]])
    return band.astype(np.int32), cached_a, cached(b_rows, 1)


def _band_bias_kernel(rel_ref, bmap_ref, o_ref):
    t = pl.program_id(0)
    hp = pl.program_id(1)
    bm = bmap_ref[...]
    head0 = jnp.where(t == len(A_PATTERNS), N_HEADS_A, 0) + 2 * hp
    for half in range(2):
        acc = jnp.full(bm.shape, NEG, F32)
        for b in range(NUM_BUCKETS):
            acc = jnp.where(bm == b, rel_ref[b, head0 + half], acc)
        o_ref[half * BLK:(half + 1) * BLK, :] = acc


def _band_bias_call(rel_bias, band_map):
    n_tab = band_map.shape[0]
    return pl.pallas_call(
        _band_bias_kernel,
        grid=(n_tab, N_PAIRS),
        in_specs=[pl.BlockSpec(memory_space=pltpu.SMEM),
                  pl.BlockSpec((None, BLK, 2 * BLK), lambda t, hp: (t, 0, 0))],
        out_specs=pl.BlockSpec((None, None, 2 * BLK, 2 * BLK), lambda t, hp: (t, hp, 0, 0)),
        out_shape=jax.ShapeDtypeStruct((n_tab, N_PAIRS, 2 * BLK, 2 * BLK), F32),
        compiler_params=pltpu.CompilerParams(dimension_semantics=("arbitrary", "arbitrary")),
        name="band_bias",
    )(rel_bias, band_map)


def _cached_bias_kernel(relt_ref, amap_ref, bmap_ref, oa_ref, ob_ref):
    relt = relt_ref[...]

    def table(bm, heads):
        acc = jnp.full(bm.shape, NEG, F32)
        for b in range(NUM_BUCKETS):
            acc = jnp.where(bm == b, heads[:, b:b + 1], acc)
        return acc

    for t in range(amap_ref.shape[0]):
        oa_ref[t] = table(jnp.broadcast_to(amap_ref[t], oa_ref.shape[1:]), relt[:N_HEADS_A])
    ob_ref[...] = table(jnp.broadcast_to(bmap_ref[...], ob_ref.shape), relt[N_HEADS_A:])


def _cached_bias_call(rel_bias, cached_a, cached_b):
    return pl.pallas_call(
        _cached_bias_kernel,
        out_shape=(jax.ShapeDtypeStruct((cached_a.shape[0], N_HEADS_A, cached_a.shape[2]), F32),
                   jax.ShapeDtypeStruct((N_HEADS_B, cached_b.shape[1]), F32)),
        name="cached_bias",
    )(rel_bias.T, cached_a, cached_b)


def _ffn_kernel(*refs, alpha, pre_mix, mix_slots, n_chunks, fc, n_side, sample0, mod_row):
    refs = list(refs)
    x_ref = refs.pop(0)
    if pre_mix:
        mix_refs = [refs.pop(0) for _ in range(2 if mix_slots else 1)]
        wo_ref, gm_ref, ln1g_ref, ln1b_ref = (refs.pop(0) for _ in range(4))
    sh_ref, sc_ref, gt_ref, wg_ref, wu_ref, wd_ref, lng_ref, lnb_ref = (refs.pop(0) for _ in range(8))
    if n_side:
        side_in = [refs.pop(0) for _ in range(10)]
        ak_hbm, av_hbm, o_ref, oc_ref, h_ref, acc_ref, kbuf, vbuf, sem = (refs.pop(0) for _ in range(9))
        side_scratch = refs
        end = sample0 + pl.num_programs(0) * n_side
        first = sample0 + pl.program_id(0) * n_side

        def cache_copies(g, slot):
            return (pltpu.make_async_copy(ak_hbm.at[g], kbuf.at[slot], sem.at[0, slot]),
                    pltpu.make_async_copy(av_hbm.at[g], vbuf.at[slot], sem.at[1, slot]))

        @pl.when(pl.program_id(0) == 0)
        def _():
            for cp in cache_copies(sample0, 0):
                cp.start()

        def side(j):
            g, slot = first + j, j % 2
            for cp in cache_copies(g, slot):
                cp.wait()

            @pl.when(g + 1 < end)
            def _():
                for cp in cache_copies(g + 1, 1 - slot):
                    cp.start()

            _cached_sample(j, kbuf.at[slot], vbuf.at[slot], *side_in, oc_ref, *side_scratch)

        assert n_side % 2 == 0
        side_at = {(j * n_chunks) // n_side: j for j in range(n_side)}
    else:
        o_ref, h_ref, acc_ref = refs
        side_at = {}

    x = x_ref[...]
    if pre_mix:
        if mix_slots:
            mixed = jnp.concatenate([r[j] for r in mix_refs for j in range(N_PAIRS)], axis=-1)
        else:
            mixed = mix_refs[0][...].astype(BF16)
        y = jnp.dot(mixed, wo_ref[...], preferred_element_type=F32)
        x = _layer_norm(alpha * x + _seq_mod(gm_ref, mod_row) * y, ln1g_ref[...], ln1b_ref[...])
    h_ref[...] = (x * (1.0 + _seq_mod(sc_ref, mod_row)) + _seq_mod(sh_ref, mod_row)).astype(BF16)
    for c in range(n_chunks):
        if c in side_at:
            side(side_at[c])
        cols = slice(c * fc, (c + 1) * fc)
        g = jnp.dot(h_ref[...], wg_ref[:, cols], preferred_element_type=F32)
        u = jnp.dot(h_ref[...], wu_ref[:, cols], preferred_element_type=F32)
        y = jnp.dot((_silu(g) * u).astype(BF16), wd_ref[cols, :], preferred_element_type=F32)
        if c == 0:
            acc_ref[...] = y
        else:
            acc_ref[...] += y
    o_ref[...] = _layer_norm(alpha * x + 0.5 * _seq_mod(gt_ref, mod_row) * acc_ref[...], lng_ref[...], lnb_ref[...])


def _ffn_call(x, mod2, mod_ks, seq0, rows_per_seq, wg, wu, wd, ln_g, ln_b, alpha, tm, fc,
              mix=None, w_out=None, gate_k=None, ln1_g=None, ln1_b=None, side=None):
    n, d = x.shape
    d_ff = wg.shape[1]
    n_chunks = d_ff // fc
    assert n_chunks * fc == d_ff
    per_row = rows_per_seq == 1
    tiles_per_seq = max(rows_per_seq // tm, 1)

    def mod_spec(k):
        if per_row:
            return pl.BlockSpec((tm, d), lambda i, k=k: (seq0 // tm + i, k))
        return _seq_mod_spec(seq0, tiles_per_seq, k, d)

    mod_arr = mod2
    row_spec = pl.BlockSpec((tm, d), lambda i: (i, 0))
    vec_spec = pl.BlockSpec((1, d), lambda i: (0, 0))
    args, specs = [x], [row_spec]
    pre_mix = mix is not None
    mix_slots = pre_mix and isinstance(mix, tuple)
    blocks = 2 * _nbytes((tm, d), F32) + (3 * _nbytes((tm, d), F32) if per_row else 0)
    resident = 3 * _nbytes(wg.shape, BF16)
    if pre_mix:
        if mix_slots:
            for mm in mix:
                args.append(mm)
                specs.append(pl.BlockSpec((N_PAIRS, tm, LANES), lambda i: (0, i, 0)))
        else:
            args.append(mix)
            specs.append(row_spec)
        args += [w_out, mod_arr, ln1_g.reshape(1, d), ln1_b.reshape(1, d)]
        specs += [_resident_spec((d, d)), mod_spec(gate_k), vec_spec, vec_spec]
        blocks += _nbytes((tm, d), F32)
        resident += _nbytes((d, d), BF16)
    args += [mod_arr, mod_arr, mod_arr, wg, wu, wd, ln_g.reshape(1, d), ln_b.reshape(1, d)]
    specs += [mod_spec(mod_ks[0]), mod_spec(mod_ks[1]), mod_spec(mod_ks[2]),
              _resident_spec(wg.shape), _resident_spec(wu.shape), _resident_spec(wd.shape), vec_spec, vec_spec]
    scratch = _nbytes((tm, d), BF16) + _nbytes((tm, d), F32) + resident
    temps = 6 * _nbytes((tm, fc), F32) + 3 * _nbytes((tm, d), F32)
    out_specs, out_shape = row_spec, jax.ShapeDtypeStruct((n, d), F32)
    scratch_shapes = [pltpu.VMEM((tm, d), BF16), pltpu.VMEM((tm, d), F32)]
    n_side = sample0 = 0
    if side is not None:
        (*small, ak, av), sample0, count = side
        n_steps = n // tm
        n_side = count // n_steps
        assert n_side * n_steps == count and sample0 % n_side == 0
        blk0 = sample0 // n_side
        for a in small[:6]:
            args.append(a)
            specs.append(pl.BlockSpec((n_side,) + a.shape[1:],
                                      lambda i, nd=a.ndim: (blk0 + i,) + (0,) * (nd - 1)))
            blocks += n_side * _nbytes(a.shape[1:-1] + (max(a.shape[-1], LANES),), F32)
        for a in small[6:]:
            args.append(a)
            specs.append(pl.BlockSpec(a.shape, lambda i, nd=a.ndim: (0,) * nd))
            blocks += _nbytes(a.shape, F32)
        args += [ak, av]
        specs += [pl.BlockSpec(memory_space=pl.ANY)] * 2
        q3 = small[0]
        out_specs = [row_spec, pl.BlockSpec((n_side,) + q3.shape[1:], lambda i: (i, 0, 0))]
        out_shape = [out_shape, jax.ShapeDtypeStruct((count,) + q3.shape[1:], F32)]
        sa_shape = (n_side, N_HEADS_A, ak.shape[3])
        sb_shape = (n_side, N_HEADS_B, small[4].shape[3])
        scratch_shapes += [pltpu.VMEM((2,) + ak.shape[1:], F32), pltpu.VMEM((2,) + av.shape[1:], F32),
                           pltpu.SemaphoreType.DMA((2, 2)),
                           pltpu.VMEM(sa_shape, F32), pltpu.VMEM(sb_shape, F32),
                           pltpu.VMEM(sa_shape, F32), pltpu.VMEM(sb_shape, F32)]
        scratch += 4 * _nbytes(ak.shape[1:], F32) + 2 * _nbytes(sa_shape, F32) + 2 * _nbytes(sb_shape, F32)
    return pl.pallas_call(
        functools.partial(_ffn_kernel, alpha=alpha, pre_mix=pre_mix, mix_slots=mix_slots,
                          n_chunks=n_chunks, fc=fc, n_side=n_side, sample0=sample0,
                          mod_row=None if per_row else (seq0, tiles_per_seq)),
        grid=(n // tm,),
        in_specs=specs,
        out_specs=out_specs,
        out_shape=out_shape,
        scratch_shapes=scratch_shapes,
        compiler_params=pltpu.CompilerParams(
            dimension_semantics=("arbitrary",),
            vmem_limit_bytes=_vmem_limit(blocks, scratch, temps)),
        name=("ffn_mix" if pre_mix else "ffn") + ("_cached" if n_side else ""),
    )(*args)


def _proj_prompt_kernel(x_ref, sh_ref, sc_ref, w_ref, p_ref, p4_ref, p16_ref, ak_ref, av_ref, bk_ref, bv_ref,
                        stage_ref, stage4_ref, *, tm, b_rows, mod_row):
    h = (x_ref[...] * (1.0 + _seq_mod(sc_ref, mod_row)) + _seq_mod(sh_ref, mod_row)).astype(BF16)
    lane_lo = lax.broadcasted_iota(jnp.int32, (tm, LANES), 1) < HEAD_DIM
    for g in range(4):
        res = jnp.dot(h, w_ref[:, g * A_COLS:(g + 1) * A_COLS], preferred_element_type=F32)
        scaled = res * SCALE if g in (0, 3) else res
        for s in range(N_PAIRS):
            slab = scaled[:, s * LANES:(s + 1) * LANES]
            p_ref[g * N_PAIRS + s] = slab.astype(BF16)
            if g < 3:
                slot = g * N_PAIRS + s
                d4, d16 = p4_ref.shape[1], p16_ref.shape[1]
                stage_ref[slot] = slab
                for r in range(d4):
                    rows = stage_ref[slot, pl.ds(r, tm // d4, stride=d4), :]
                    p4_ref[slot, r] = rows.astype(BF16)
                    stage4_ref[slot, r] = rows
                for r in range(d16):
                    rows = stage4_ref[slot, r % d4, pl.ds(r // d4, tm // d16, stride=d16 // d4), :]
                    p16_ref[slot, r] = rows.astype(BF16)
        if g in (1, 2):
            out_ref = ak_ref if g == 1 else av_ref
            out_ref[...] = res.T.reshape(out_ref.shape)
    res = jnp.dot(h, w_ref[:, 4 * A_COLS:], preferred_element_type=F32)
    for j, out_ref in enumerate((bk_ref, bv_ref)):
        slab = res[:, j * LANES:(j + 1) * LANES]
        swapped = pltpu.roll(slab, HEAD_DIM, axis=1)
        p_ref[4 * N_PAIRS + N_KV_B * j] = jnp.where(lane_lo, slab, swapped).astype(BF16)
        p_ref[4 * N_PAIRS + N_KV_B * j + 1] = jnp.where(lane_lo, swapped, slab).astype(BF16)

        out_ref[...] = slab[tm - b_rows:, :].T.reshape(out_ref.shape)


def _proj_prompt_call(x, mod2, seq0, seq, w_ext, a_rows, b_rows, tm):
    n, d = x.shape
    batch = n // seq
    tiles_per_seq = seq // tm
    a_tiles = a_rows // tm
    n_slots = 4 * N_PAIRS + 2 * N_KV_B
    assert w_ext.shape[1] == 4 * A_COLS + 2 * LANES and N_KV_B * HEAD_DIM == LANES

    def mod_spec(k):
        return _seq_mod_spec(seq0, tiles_per_seq, k, d)

    def a_map(i):
        return (i // tiles_per_seq, 0, 0, jnp.maximum(i % tiles_per_seq - (tiles_per_seq - a_tiles), 0))

    a_blk = (None, N_HEADS_A, HEAD_DIM, tm)
    b_blk = (None, N_KV_B, HEAD_DIM, b_rows)
    a_slots = 3 * N_PAIRS
    d4, d16 = A_PATTERNS[1][1], A_PATTERNS[2][1]
    assert tm % (16 * d16) == 0

    def dil_spec(dil):
        return pl.BlockSpec((a_slots, None, dil, tm // dil, LANES),
                            lambda i: (0, i // tiles_per_seq, 0, i % tiles_per_seq, 0))

    blocks = (_nbytes((tm, d), F32) + _nbytes(w_ext.shape, BF16) + _nbytes((n_slots + 2 * a_slots, tm, LANES), BF16)
              + 2 * _nbytes((tm, A_COLS), F32) + 2 * _nbytes((b_rows, LANES), F32))
    return pl.pallas_call(
        functools.partial(_proj_prompt_kernel, tm=tm, b_rows=b_rows, mod_row=(seq0, tiles_per_seq)),
        grid=(n // tm,),
        in_specs=[pl.BlockSpec((tm, d), lambda i: (i, 0)), mod_spec(3), mod_spec(4),
                  pl.BlockSpec(w_ext.shape, lambda i: (0, 0))],
        out_specs=[pl.BlockSpec((n_slots, tm, LANES), lambda i: (0, i, 0)),
                   dil_spec(d4), dil_spec(d16),
                   pl.BlockSpec(a_blk, a_map),
                   pl.BlockSpec(a_blk, a_map),
                   pl.BlockSpec(b_blk, lambda i: (i // tiles_per_seq, 0, 0, 0)),
                   pl.BlockSpec(b_blk, lambda i: (i // tiles_per_seq, 0, 0, 0))],
        out_shape=[jax.ShapeDtypeStruct((n_slots, n, LANES), BF16),
                   jax.ShapeDtypeStruct((a_slots, batch, d4, seq // d4, LANES), BF16),
                   jax.ShapeDtypeStruct((a_slots, batch, d16, seq // d16, LANES), BF16),
                   jax.ShapeDtypeStruct((batch, N_HEADS_A, HEAD_DIM, a_rows), F32),
                   jax.ShapeDtypeStruct((batch, N_HEADS_A, HEAD_DIM, a_rows), F32),
                   jax.ShapeDtypeStruct((batch, N_KV_B, HEAD_DIM, b_rows), F32),
                   jax.ShapeDtypeStruct((batch, N_KV_B, HEAD_DIM, b_rows), F32)],
        scratch_shapes=[pltpu.VMEM((a_slots, tm, LANES), F32), pltpu.VMEM((a_slots, d4, tm // d4, LANES), F32)],
        compiler_params=pltpu.CompilerParams(
            dimension_semantics=("arbitrary",),
            vmem_limit_bytes=_vmem_limit(blocks, 2 * _nbytes((a_slots, tm, LANES), F32),
                                         4 * _nbytes((tm, A_COLS), F32))),
        name="proj_prompt",
    )(x, mod2, mod2, w_ext)


def _proj_sample_kernel(x_ref, sh_ref, sc_ref, w_ref, o_ref):
    h = (x_ref[...] * (1.0 + sc_ref[...]) + sh_ref[...]).astype(BF16)
    o_ref[...] = jnp.dot(h, w_ref[...], preferred_element_type=F32)


def _proj_sample_call(x, mod2, w_ext):
    n, d = x.shape
    cols = w_ext.shape[1]
    blocks = 3 * _nbytes((n, d), F32) + _nbytes(w_ext.shape, BF16) + _nbytes((n, cols), F32)
    return pl.pallas_call(
        _proj_sample_kernel,
        grid=(1,),
        in_specs=[pl.BlockSpec((n, d), lambda i: (0, 0)),
                  pl.BlockSpec((n, d), lambda i: (0, 3)),
                  pl.BlockSpec((n, d), lambda i: (0, 4)),
                  pl.BlockSpec(w_ext.shape, lambda i: (0, 0))],
        out_specs=pl.BlockSpec((n, cols), lambda i: (0, 0)),
        out_shape=jax.ShapeDtypeStruct((n, cols), F32),
        compiler_params=pltpu.CompilerParams(
            dimension_semantics=("arbitrary",), vmem_limit_bytes=_vmem_limit(blocks)),
        name="proj_sample",
    )(x, mod2, mod2, w_ext)


def _pair_scores(q, kk, bias, lane_lo):
    zero = jnp.zeros_like(q)
    q2 = jnp.concatenate([jnp.where(lane_lo, q, zero), jnp.where(lane_lo, zero, q)], axis=0)
    s = lax.dot_general(q2, kk, (((1,), (1,)), ((), ())), preferred_element_type=F32)
    return s + bias


def _pair_select(a, lane_lo):
    return jnp.where(lane_lo, a[:BLK], a[BLK:])


GROUP = 16


def _band_sequence(q_ref, k_ref, v_ref, n_blocks, load_bias, block, prefetch=lambda jb: None):
    def one(jb, first, pre):
        if first:
            block(jb, q_ref[0:BLK, :], k_ref[0:BLK, :], v_ref[0:BLK, :], load_bias(True), pre)
        else:
            r0 = jb * BLK
            if not isinstance(jb, int):
                r0 = pl.multiple_of(r0, BLK)
            block(jb, q_ref[pl.ds(r0, BLK), :], k_ref[pl.ds(r0 - BLK, 2 * BLK), :],
                  v_ref[pl.ds(r0 - BLK, 2 * BLK), :], load_bias(False), pre)

    def run(jbs, has_first):
        pres = [prefetch(jb) for jb in jbs]
        for j, (jb, pre) in enumerate(zip(jbs, pres)):
            one(jb, has_first and j == 0, pre)

    run(list(range(min(GROUP, n_blocks))), True)
    if n_blocks > GROUP:
        assert n_blocks % GROUP == 0

        def group(g, carry):
            run([g * GROUP + j for j in range(GROUP)], False)
            return carry

        lax.fori_loop(1, n_blocks // GROUP, group, 0)


def _attn_a_kernel(q16_ref, k16_ref, v16_ref, q4_ref, k4_ref, v4_ref, q1_ref, k1_ref, v1_ref,
                   bias_ref, o_ref, acc16_ref, m16_ref, den16_ref, acc4_ref, m4_ref, den4_ref, *, seq):
    lane_lo = lax.broadcasted_iota(jnp.int32, (BLK, LANES), 1) < HEAD_DIM
    state16 = (acc16_ref, m16_ref, den16_ref)
    state4 = (acc4_ref, m4_ref, den4_ref)

    def update(q, kk, vv, bias, rows, old, new):
        s = _pair_scores(q, kk, bias, lane_lo)
        part = s[:, :LANES] if s.shape[1] == LANES else jnp.maximum(s[:, :LANES], s[:, LANES:])
        if old is not None:
            acc_old, m_old, den_old = old
            part = jnp.maximum(part, jnp.concatenate([jnp.where(lane_lo, m_old, NEG),
                                                      jnp.where(lane_lo, NEG, m_old)], axis=0))
        m2 = jnp.max(part, axis=-1, keepdims=True)
        p = jnp.exp(s - m2)
        l2 = jnp.sum(p, axis=-1, keepdims=True)
        pv = _pair_select(jnp.dot(p.astype(BF16), vv, preferred_element_type=F32), lane_lo)
        m_new = _pair_select(m2, lane_lo)
        den = _pair_select(l2, lane_lo)
        if old is None:
            acc = pv
        else:
            a = jnp.exp(m_old - m_new)
            acc = a * acc_old + pv
            den = a * den_old + den
        if new is None:
            o_ref[rows, :] = (acc / den).astype(o_ref.dtype)
        else:
            new[0][rows, :] = acc
            new[1][rows, :] = m_new
            new[2][rows, :] = den

    def branch(q_ref, k_ref, v_ref, table, dil, r, old, new):
        def rows(jb):
            start = r + jb * (BLK * dil)
            return pl.ds(start, BLK, stride=dil) if dil > 1 else pl.ds(start, BLK)

        def prefetch(jb):
            return None if old is None else tuple(ref[rows(jb), :] for ref in old)

        def block(jb, q, kk, vv, bias, pre):
            update(q, kk, vv, bias, rows(jb), pre, new)

        def load_bias(first):
            return bias_ref[table, :, BLK:2 * BLK] if first else bias_ref[table]

        _band_sequence(q_ref, k_ref, v_ref, q_ref.shape[0] // BLK, load_bias, block, prefetch)

    def dilated(q_ref, k_ref, v_ref, table, old, new):
        dil = q_ref.shape[0]
        per_group = min(max(GROUP // (q_ref.shape[1] // BLK), 1), dil)
        assert dil % per_group == 0

        def group(g, carry):
            for rr in range(per_group):
                r = g * per_group + rr
                branch(q_ref.at[r], k_ref.at[r], v_ref.at[r], table, dil, r, old, new)
            return carry

        lax.fori_loop(0, dil // per_group, group, 0)

    dilated(q16_ref, k16_ref, v16_ref, 2, None, state16)
    dilated(q4_ref, k4_ref, v4_ref, 1, state16, state4)
    branch(q1_ref, k1_ref, v1_ref, 0, 1, 0, state4, None)


def _attn_a_call(p_slots, p4, p16, bias_a, batch, seq):
    n_slots = p_slots.shape[0]
    dils = [d for _, d in A_PATTERNS]
    assert dils == [1, p4.shape[2], p16.shape[2]] and seq % (dils[2] * BLK) == 0
    view1 = p_slots.reshape(n_slots, batch, seq, LANES)

    in_specs, args = [], []
    for arr in (p16, p4, view1):
        for slot0 in (0, N_PAIRS, 2 * N_PAIRS):
            in_specs.append(pl.BlockSpec((None, None) + arr.shape[2:],
                                         lambda b, hp, slot0=slot0, nd=arr.ndim: (slot0 + hp, b) + (0,) * (nd - 2)))
            args.append(arr)
    in_specs.append(pl.BlockSpec((len(A_PATTERNS), None, 2 * BLK, 2 * BLK), lambda b, hp: (0, hp, 0, 0)))
    args.append(bias_a)
    blocks = 10 * _nbytes((seq, LANES), BF16) + _nbytes((3, 2 * BLK, 2 * BLK), F32)
    scratch = 6 * _nbytes((seq, LANES), F32)
    return pl.pallas_call(
        functools.partial(_attn_a_kernel, seq=seq),
        grid=(batch, N_PAIRS),
        in_specs=in_specs,
        out_specs=pl.BlockSpec((None, seq, LANES), lambda b, hp: (hp, b, 0)),
        out_shape=jax.ShapeDtypeStruct((N_PAIRS, batch * seq, LANES), BF16),
        scratch_shapes=[pltpu.VMEM((seq, LANES), F32)] * 6,
        compiler_params=pltpu.CompilerParams(
            dimension_semantics=("arbitrary", "arbitrary"),
            vmem_limit_bytes=_vmem_limit(blocks, scratch, 4 * GROUP * _nbytes((2 * BLK, 2 * BLK), F32))),
        name="attn_dilated",
    )(*args)


def _attn_b_kernel(sink_ref, q_ref, k_ref, v_ref, bias_ref, o_ref, *, seq):
    hp = pl.program_id(1)
    lane_lo = lax.broadcasted_iota(jnp.int32, (BLK, LANES), 1) < HEAD_DIM
    row_lo = lax.broadcasted_iota(jnp.int32, (2 * BLK, 1), 0) < BLK
    sink2 = jnp.where(row_lo, sink_ref[2 * hp], sink_ref[2 * hp + 1])

    def block(jb, q, kk, vv, bias, pre):
        s = _pair_scores(q, kk, bias, lane_lo)
        m2 = jnp.maximum(jnp.max(s, axis=-1, keepdims=True), sink2)
        p = jnp.exp(s - m2)
        l2 = jnp.sum(p, axis=-1, keepdims=True) + jnp.exp(sink2 - m2)
        pv = _pair_select(jnp.dot(p.astype(BF16), vv, preferred_element_type=F32), lane_lo)
        r0 = jb * BLK if isinstance(jb, int) else pl.multiple_of(jb * BLK, BLK)
        o_ref[pl.ds(r0, BLK), :] = (pv / _pair_select(l2, lane_lo)).astype(o_ref.dtype)

    def load_bias(first):
        return bias_ref[:, BLK:2 * BLK] if first else bias_ref[...]

    _band_sequence(q_ref, k_ref, v_ref, seq // BLK, load_bias, block)


def _attn_b_call(p_slots, bias_b, sinks, batch, seq):
    n_slots = p_slots.shape[0]
    view = p_slots.reshape(n_slots, batch, seq, LANES)
    q0, k0, v0 = 3 * N_PAIRS, 4 * N_PAIRS, 4 * N_PAIRS + N_KV_B
    pairs_per_kv = N_PAIRS // N_KV_B
    blk = (None, None, seq, LANES)
    blocks = 4 * _nbytes((seq, LANES), BF16) + _nbytes((2 * BLK, 2 * BLK), F32)
    return pl.pallas_call(
        functools.partial(_attn_b_kernel, seq=seq),
        grid=(batch, N_PAIRS),
        in_specs=[pl.BlockSpec(memory_space=pltpu.SMEM),
                  pl.BlockSpec(blk, lambda b, hp: (q0 + hp, b, 0, 0)),
                  pl.BlockSpec(blk, lambda b, hp: (k0 + hp // pairs_per_kv, b, 0, 0)),
                  pl.BlockSpec(blk, lambda b, hp: (v0 + hp // pairs_per_kv, b, 0, 0)),
                  pl.BlockSpec((None, 2 * BLK, 2 * BLK), lambda b, hp: (hp, 0, 0))],
        out_specs=pl.BlockSpec((None, seq, LANES), lambda b, hp: (hp, b, 0)),
        out_shape=jax.ShapeDtypeStruct((N_PAIRS, batch * seq, LANES), BF16),
        compiler_params=pltpu.CompilerParams(
            dimension_semantics=("arbitrary", "arbitrary"),
            vmem_limit_bytes=_vmem_limit(blocks, temp_bytes=8 * _nbytes((2 * BLK, 2 * BLK), F32))),
        name="attn_window",
    )(sinks, view, view, view, bias_b)


def _cached_sample(i, ak_ref, av_ref, q_ref, qt_ref, knt_ref, vn_ref, bk_ref, bv_ref, bias_a_ref, bias_b_ref,
                   relt_ref, sink_ref, o_ref, sa_ref, sb_ref, wa_ref, wb_ref):
    n_branches = bias_a_ref.shape[0]
    heads_per_kv = N_HEADS_B // N_KV_B
    bias0 = relt_ref[:, 0:1]
    sink = sink_ref[...]
    q = q_ref[i] * SCALE
    vn = vn_ref[i]
    s0 = jnp.sum(qt_ref[i] * SCALE * knt_ref[i], axis=-1, keepdims=True) + bias0
    for h in range(N_HEADS_A):
        sa_ref[i, h:h + 1, :] = jnp.sum(ak_ref[h] * q[:, h:h + 1], axis=0, keepdims=True)
    for hb in range(N_HEADS_B):
        h = N_HEADS_A + hb
        sb_ref[i, hb:hb + 1, :] = jnp.sum(bk_ref[i, hb // heads_per_kv] * q[:, h:h + 1], axis=0, keepdims=True)

    s0a = s0[:N_HEADS_A]
    ts = [sa_ref[i] + bias_a_ref[t] for t in range(n_branches)]
    m = s0a
    for t_ in ts:
        m = jnp.maximum(m, jnp.max(t_, axis=-1, keepdims=True))
    w = jnp.exp(ts[0] - m)
    for t_ in ts[1:]:
        w = w + jnp.exp(t_ - m)
    p0a = n_branches * jnp.exp(s0a - m)
    inv = 1.0 / (jnp.sum(w, axis=-1, keepdims=True) + p0a)
    wa_ref[i] = w * inv
    p0a = p0a * inv

    s0b = s0[N_HEADS_A:]
    tb = sb_ref[i] + bias_b_ref[...]
    mb = jnp.maximum(jnp.maximum(s0b, sink), jnp.max(tb, axis=-1, keepdims=True))
    wb = jnp.exp(tb - mb)
    p0b = jnp.exp(s0b - mb)
    invb = 1.0 / (jnp.sum(wb, axis=-1, keepdims=True) + p0b + jnp.exp(sink - mb))
    wb_ref[i] = wb * invb
    p0b = p0b * invb

    cols = []
    for h in range(N_HEADS_A):
        acc = jnp.sum(av_ref[h] * wa_ref[i, h:h + 1, :], axis=-1, keepdims=True)
        cols.append(acc + p0a[h:h + 1, :] * vn[:, h:h + 1])
    for hb in range(N_HEADS_B):
        h = N_HEADS_A + hb
        acc = jnp.sum(bv_ref[i, hb // heads_per_kv] * wb_ref[i, hb:hb + 1, :], axis=-1, keepdims=True)
        cols.append(acc + p0b[hb:hb + 1, :] * vn[:, h:h + 1])
    o_ref[i] = jnp.concatenate(cols, axis=1)


def kernel(x_prompt, x_sample, cache_a_k, cache_a_v, cache_b_k, cache_b_v, c_prompt, c_sample, rel_bias, w_ada, b_ada, ffn1_wg, ffn1_wu, ffn1_wd, w_in, w_out, sinks, ffn2_wg, ffn2_wu, ffn2_wd, ln_g, ln_b):
    batch, seq, d = x_prompt.shape
    n_dec, dec_seq, _ = x_sample.shape
    depth = w_ada.shape[0]
    assert dec_seq == 1 and d == 2 * A_COLS and all(w // dl == BLK for w, dl in A_PATTERNS)
    alpha = (2 * depth) ** 0.25
    fc = 256
    tm = 512
    a_rows, b_rows = min(A_PATTERNS[-1][0], seq), min(B_WINDOW, seq)
    q_cols = 4 * A_COLS
    kvb = N_KV_B * HEAD_DIM
    heads_per_kv = N_HEADS_B // N_KV_B

    band_map, cached_a, cached_b = _bucket_maps(cache_a_k.shape[2], cache_b_k.shape[2])
    band_bias = _band_bias_call(rel_bias, band_map)
    bias_a, bias_b = band_bias[:len(A_PATTERNS)], band_bias[len(A_PATTERNS)]
    bias_sa, bias_sb = _cached_bias_call(rel_bias, cached_a, cached_b)

    pad = (-(n_dec + batch)) % 16
    c_all = jnp.concatenate([c_sample, c_prompt, jnp.zeros((pad, d), F32)], axis=0)
    xp = x_prompt.reshape(batch * seq, d)
    xs = x_sample.reshape(n_dec, d)
    outs = [[] for _ in range(8)]

    def head_rows(x, n_heads):
        return x.reshape(n_dec, n_heads, HEAD_DIM)

    for l in range(depth):
        mod2 = _ada_call(c_all, w_ada[l], b_ada[l])
        ffn1 = [w[l].astype(BF16) for w in (ffn1_wg, ffn1_wu, ffn1_wd)]
        ffn2 = [w[l].astype(BF16) for w in (ffn2_wg, ffn2_wu, ffn2_wd)]
        w_in_s = w_in[l].astype(BF16)
        wo = w_out[l].astype(BF16)

        s1 = _ffn_call(xs, mod2, (0, 1, 2), 0, 1, *ffn1, ln_g[l, 0], ln_b[l, 0], alpha, n_dec, fc)
        proj_s = _proj_sample_call(s1, mod2, w_in_s)
        c = A_COLS
        sak, sav = proj_s[:, c:2 * c], proj_s[:, 2 * c:3 * c]
        sbk, sbv = proj_s[:, q_cols:q_cols + kvb], proj_s[:, q_cols + kvb:q_cols + 2 * kvb]
        qt3 = jnp.concatenate([head_rows(proj_s[:, :c], N_HEADS_A),
                               head_rows(proj_s[:, 3 * c:q_cols], N_HEADS_B)], axis=1)
        knt3 = jnp.concatenate([head_rows(sak, N_HEADS_A),
                                jnp.repeat(head_rows(sbk, N_KV_B), heads_per_kv, axis=1)], axis=1)
        vnt3 = jnp.concatenate([head_rows(sav, N_HEADS_A),
                                jnp.repeat(head_rows(sbv, N_KV_B), heads_per_kv, axis=1)], axis=1)
        to_cols = (0, 2, 3, 1)
        decode_side = (qt3.transpose(0, 2, 1), qt3, knt3, vnt3.transpose(0, 2, 1),
                       cache_b_k[l].transpose(to_cols), cache_b_v[l].transpose(to_cols),
                       bias_sa, bias_sb, rel_bias.T, sinks[l].reshape(N_HEADS_B, 1),
                       cache_a_k[l].transpose(to_cols), cache_a_v[l].transpose(to_cols))

        n_first = n_dec // 2 if (n_dec // 2) % (2 * (batch * seq // tm)) == 0 else n_dec
        x1, mix3 = _ffn_call(xp, mod2, (0, 1, 2), n_dec, seq, *ffn1, ln_g[l, 0], ln_b[l, 0], alpha, tm, fc,
                             side=(decode_side, 0, n_first))
        p_slots, p4, p16, pak, pav, pbk, pbv = _proj_prompt_call(x1, mod2, n_dec, seq, w_in_s, a_rows, b_rows, tm)
        mix_a = _attn_a_call(p_slots, p4, p16, bias_a, batch, seq)
        mix_b = _attn_b_call(p_slots, bias_b, sinks[l], batch, seq)
        xp = _ffn_call(x1, mod2, (6, 7, 8), n_dec, seq, *ffn2, ln_g[l, 2], ln_b[l, 2], alpha, tm, fc,
                       mix=(mix_a, mix_b), w_out=wo, gate_k=5, ln1_g=ln_g[l, 1], ln1_b=ln_b[l, 1],
                       side=(decode_side, n_first, n_dec - n_first) if n_first < n_dec else None)
        if n_first < n_dec:
            xp, mix3_rest = xp
            mix3 = jnp.concatenate([mix3, mix3_rest], axis=0)

        mix_s = mix3.transpose(0, 2, 1).reshape(n_dec, d)
        xs = _ffn_call(s1, mod2, (6, 7, 8), 0, 1, *ffn2, ln_g[l, 2], ln_b[l, 2], alpha, n_dec, fc,
                       mix=mix_s, w_out=wo, gate_k=5, ln1_g=ln_g[l, 1], ln1_b=ln_b[l, 1])

        to_rows = (0, 3, 1, 2)
        new = [pak.transpose(to_rows), pav.transpose(to_rows), pbk.transpose(to_rows), pbv.transpose(to_rows),
               sak.reshape(n_dec, 1, N_HEADS_A, HEAD_DIM), sav.reshape(n_dec, 1, N_HEADS_A, HEAD_DIM),
               sbk.reshape(n_dec, 1, N_KV_B, HEAD_DIM), sbv.reshape(n_dec, 1, N_KV_B, HEAD_DIM)]
        for acc, arr in zip(outs, new):
            acc.append(arr)

    return (xp.reshape(batch, seq, d), xs.reshape(n_dec, 1, d)) + tuple(jnp.stack(o) for o in outs)
```

```python
import functools
import math

import jax
import jax.numpy as jnp
import numpy as np
from jax import lax
from jax.experimental import pallas as pl
from jax.experimental.pallas import tpu as pltpu

HEAD_DIM = 64
N_HEADS_A = 8
N_HEADS_B = 8
N_KV_B = 2
A_PATTERNS = ((128, 1), (512, 4), (2048, 16))
B_WINDOW = 128
BLK = 128
NUM_BUCKETS = 32
MAX_DISTANCE = 2048
LN_EPS = 1e-5
NEG = -1e30
SCALE = HEAD_DIM ** -0.5

LANES = 128
SUBLANES = 8
V7X_VMEM_BYTES = 64 * 1024 * 1024
A_COLS = N_HEADS_A * HEAD_DIM
N_PAIRS = A_COLS // LANES
N_MOD = 9

F32 = jnp.float32
BF16 = jnp.bfloat16


def _vmem_limit(pipelined_bytes, scratch_bytes=0, temp_bytes=0):
    want = 2 * pipelined_bytes + scratch_bytes + temp_bytes + (4 << 20)
    return int(min(want, V7X_VMEM_BYTES - (6 << 20)))


def _resident_spec(shape):
    return pl.BlockSpec(shape, lambda *_: (0,) * len(shape), pipeline_mode=pl.Buffered(1))


def _nbytes(shape, dtype):
    return math.prod(shape) * jnp.dtype(dtype).itemsize


def _layer_norm(x, g, b):
    mu = jnp.mean(x, axis=-1, keepdims=True)
    xc = x - mu
    var = jnp.mean(xc * xc, axis=-1, keepdims=True)
    return xc * lax.rsqrt(var + LN_EPS) * g + b


def _silu(x):
    return x * jax.nn.sigmoid(x)


def _seq_mod_spec(seq0, tiles_per_seq, k, d):
    return pl.BlockSpec((SUBLANES, d), lambda i: ((seq0 + i // tiles_per_seq) // SUBLANES, k))


def _seq_mod(ref, mod_row):
    if mod_row is None:
        return ref[...]
    seq0, tiles_per_seq = mod_row
    return ref[pl.ds((seq0 + pl.program_id(0) // tiles_per_seq) % SUBLANES, 1), :]


def _ada_kernel(c_ref, w_ref, b_ref, o_ref):
    a = _silu(c_ref[...]).astype(BF16)
    o_ref[...] = jnp.dot(a, w_ref[...].astype(BF16), preferred_element_type=F32) + b_ref[...]


def _ada_call(c_all, w_ada, b_ada):
    m, d = c_all.shape
    n = w_ada.shape[1]
    tn = n // 8
    blocks = _nbytes((m, d), F32) + _nbytes((d, tn), F32) + _nbytes((m, tn), F32)
    return pl.pallas_call(
        _ada_kernel,
        grid=(n // tn,),
        in_specs=[pl.BlockSpec((m, d), lambda j: (0, 0)),
                  pl.BlockSpec((d, tn), lambda j: (0, j)),
                  pl.BlockSpec((1, tn), lambda j: (0, j))],
        out_specs=pl.BlockSpec((m, tn), lambda j: (0, j)),
        out_shape=jax.ShapeDtypeStruct((m, n), F32),
        compiler_params=pltpu.CompilerParams(
            dimension_semantics=("arbitrary",),
            vmem_limit_bytes=_vmem_limit(blocks, temp_bytes=_nbytes((d, tn), BF16))),
        name="ada_mod",
    )(c_all, w_ada, b_ada.reshape(1, n))


def _t5_bucket(n):
    max_exact = NUM_BUCKETS // 2
    nf = np.maximum(n, 1).astype(np.float32)
    large = max_exact + (np.log(nf / max_exact) / math.log(MAX_DISTANCE / max_exact)
                         * (NUM_BUCKETS - max_exact)).astype(np.int32)
    return np.where(n < max_exact, n, np.minimum(large, NUM_BUCKETS - 1))


def _bucket_maps(a_rows, b_rows):
    steps = [d for _, d in A_PATTERNS] + [1]
    q = np.arange(BLK)[:, None]
    k = np.arange(2 * BLK)[None, :]
    dist = q + BLK - k
    valid = (dist >= 0) & (dist <= BLK)
    band = np.stack([np.where(valid, _t5_bucket(np.maximum(dist, 0) * s), -1) for s in steps])

    def cached(rows, step):
        back = rows - np.arange(rows)
        hit = (back % step == 0) & (back // step <= BLK)
        return np.where(hit, _t5_bucket(back), -1)[None, :].astype(np.int32)

    cached_a = np.stack([cached(a_rows, s) for s in steps[:-1]])
    return band.astype(np.int32), cached_a, cached(b_rows, 1)


def _band_bias_kernel(rel_ref, bmap_ref, o_ref):
    head0 = jnp.where(pl.program_id(0) == len(A_PATTERNS), N_HEADS_A, 0)
    rows = 2 * SUBLANES
    for r0 in range(0, BLK, rows):
        bm = bmap_ref[r0:r0 + rows, :]
        accs = [jnp.full(bm.shape, NEG, F32)] * (2 * N_PAIRS)
        for b in range(NUM_BUCKETS):
            hit = bm == b
            accs = [jnp.where(hit, rel_ref[b, head0 + h], acc) for h, acc in enumerate(accs)]
        for h, acc in enumerate(accs):
            o_ref[h // 2, (h % 2) * BLK + r0:(h % 2) * BLK + r0 + rows, :] = acc


def _band_bias_call(rel_bias, band_map):
    n_tab = band_map.shape[0]
    return pl.pallas_call(
        _band_bias_kernel,
        grid=(n_tab,),
        in_specs=[pl.BlockSpec(memory_space=pltpu.SMEM),
                  pl.BlockSpec((None, BLK, 2 * BLK), lambda t: (t, 0, 0))],
        out_specs=pl.BlockSpec((None, N_PAIRS, 2 * BLK, 2 * BLK), lambda t: (t, 0, 0, 0)),
        out_shape=jax.ShapeDtypeStruct((n_tab, N_PAIRS, 2 * BLK, 2 * BLK), F32),
        compiler_params=pltpu.CompilerParams(dimension_semantics=("arbitrary",)),
        name="band_bias",
    )(rel_bias, band_map)


def _cached_bias_kernel(relt_ref, amap_ref, bmap_ref, oa_ref, ob_ref):
    relt = relt_ref[...]

    def table(bm, heads):
        acc = jnp.full(bm.shape, NEG, F32)
        for b in range(NUM_BUCKETS):
            acc = jnp.where(bm == b, heads[:, b:b + 1], acc)
        return acc

    for t in range(amap_ref.shape[0]):
        oa_ref[t] = table(jnp.broadcast_to(amap_ref[t], oa_ref.shape[1:]), relt[:N_HEADS_A])
    ob_ref[...] = table(jnp.broadcast_to(bmap_ref[...], ob_ref.shape), relt[N_HEADS_A:])


def _cached_bias_call(rel_bias, cached_a, cached_b):
    return pl.pallas_call(
        _cached_bias_kernel,
        out_shape=(jax.ShapeDtypeStruct((cached_a.shape[0], N_HEADS_A, cached_a.shape[2]), F32),
                   jax.ShapeDtypeStruct((N_HEADS_B, cached_b.shape[1]), F32)),
        name="cached_bias",
    )(rel_bias.T, cached_a, cached_b)


def _ffn_kernel(*refs, alpha, pre_mix, mix_slots, n_chunks, fc, n_side, sample0, mod_row):
    refs = list(refs)
    x_ref = refs.pop(0)
    if pre_mix:
        mix_refs = [refs.pop(0) for _ in range(2 if mix_slots else 1)]
        wo_ref, gm_ref, ln1g_ref, ln1b_ref = (refs.pop(0) for _ in range(4))
    sh_ref, sc_ref, gt_ref, wg_ref, wu_ref, wd_ref, lng_ref, lnb_ref = (refs.pop(0) for _ in range(8))
    if n_side:
        side_in = [refs.pop(0) for _ in range(10)]
        ak_hbm, av_hbm, o_ref, oc_ref, h_ref, acc_ref, kbuf, vbuf, sem = (refs.pop(0) for _ in range(9))
        side_scratch = refs
        end = sample0 + pl.num_programs(0) * n_side
        first = sample0 + pl.program_id(0) * n_side

        def cache_copies(g, slot):
            return (pltpu.make_async_copy(ak_hbm.at[g], kbuf.at[slot], sem.at[0, slot]),
                    pltpu.make_async_copy(av_hbm.at[g], vbuf.at[slot], sem.at[1, slot]))

        @pl.when(pl.program_id(0) == 0)
        def _():
            for cp in cache_copies(sample0, 0):
                cp.start()

        def side(j):
            g, slot = first + j, j % 2
            for cp in cache_copies(g, slot):
                cp.wait()

            @pl.when(g + 1 < end)
            def _():
                for cp in cache_copies(g + 1, 1 - slot):
                    cp.start()

            _cached_sample(j, kbuf.at[slot], vbuf.at[slot], *side_in, oc_ref, *side_scratch)

        assert n_side % 2 == 0
        side_at = {(j * n_chunks) // n_side if j else -1: j for j in range(n_side)}
    else:
        o_ref, h_ref, acc_ref = refs
        side_at = {}

    if -1 in side_at:
        side(side_at[-1])
    x = x_ref[...]
    if pre_mix:
        if mix_slots:
            mixed = jnp.concatenate([r[j] for r in mix_refs for j in range(N_PAIRS)], axis=-1)
        else:
            mixed = mix_refs[0][...].astype(BF16)
        y = jnp.dot(mixed, wo_ref[...], preferred_element_type=F32)
        x = _layer_norm(alpha * x + _seq_mod(gm_ref, mod_row) * y, ln1g_ref[...], ln1b_ref[...])
    h_ref[...] = (x * (1.0 + _seq_mod(sc_ref, mod_row)) + _seq_mod(sh_ref, mod_row)).astype(BF16)
    for c in range(n_chunks):
        if c in side_at:
            side(side_at[c])
        cols = slice(c * fc, (c + 1) * fc)
        g = jnp.dot(h_ref[...], wg_ref[:, cols], preferred_element_type=F32)
        u = jnp.dot(h_ref[...], wu_ref[:, cols], preferred_element_type=F32)
        y = jnp.dot((_silu(g) * u).astype(BF16), wd_ref[cols, :], preferred_element_type=F32)
        if c == 0:
            acc_ref[...] = y
        else:
            acc_ref[...] += y
    o_ref[...] = _layer_norm(alpha * x + 0.5 * _seq_mod(gt_ref, mod_row) * acc_ref[...], lng_ref[...], lnb_ref[...])


def _ffn_call(x, mod2, mod_ks, seq0, rows_per_seq, wg, wu, wd, ln_g, ln_b, alpha, tm, fc,
              mix=None, w_out=None, gate_k=None, ln1_g=None, ln1_b=None, side=None):
    n, d = x.shape
    d_ff = wg.shape[1]
    n_chunks = d_ff // fc
    assert n_chunks * fc == d_ff
    per_row = rows_per_seq == 1
    tiles_per_seq = max(rows_per_seq // tm, 1)

    def mod_spec(k):
        if per_row:
            return pl.BlockSpec((tm, d), lambda i, k=k: (seq0 // tm + i, k))
        return _seq_mod_spec(seq0, tiles_per_seq, k, d)

    mod_arr = mod2
    row_spec = pl.BlockSpec((tm, d), lambda i: (i, 0))
    vec_spec = pl.BlockSpec((1, d), lambda i: (0, 0))
    args, specs = [x], [row_spec]
    pre_mix = mix is not None
    mix_slots = pre_mix and isinstance(mix, tuple)
    blocks = 2 * _nbytes((tm, d), F32) + (3 * _nbytes((tm, d), F32) if per_row else 0)
    resident = 3 * _nbytes(wg.shape, BF16)
    if pre_mix:
        if mix_slots:
            for mm in mix:
                args.append(mm)
                specs.append(pl.BlockSpec((N_PAIRS, tm, LANES), lambda i: (0, i, 0)))
        else:
            args.append(mix)
            specs.append(row_spec)
        args += [w_out, mod_arr, ln1_g.reshape(1, d), ln1_b.reshape(1, d)]
        specs += [_resident_spec((d, d)), mod_spec(gate_k), vec_spec, vec_spec]
        blocks += _nbytes((tm, d), F32)
        resident += _nbytes((d, d), BF16)
    args += [mod_arr, mod_arr, mod_arr, wg, wu, wd, ln_g.reshape(1, d), ln_b.reshape(1, d)]
    specs += [mod_spec(mod_ks[0]), mod_spec(mod_ks[1]), mod_spec(mod_ks[2]),
              _resident_spec(wg.shape), _resident_spec(wu.shape), _resident_spec(wd.shape), vec_spec, vec_spec]
    scratch = _nbytes((tm, d), BF16) + _nbytes((tm, d), F32) + resident
    temps = 6 * _nbytes((tm, fc), F32) + 3 * _nbytes((tm, d), F32)
    out_specs, out_shape = row_spec, jax.ShapeDtypeStruct((n, d), F32)
    scratch_shapes = [pltpu.VMEM((tm, d), BF16), pltpu.VMEM((tm, d), F32)]
    n_side = sample0 = 0
    if side is not None:
        (*small, ak, av), sample0, count = side
        n_steps = n // tm
        n_side = count // n_steps
        assert n_side * n_steps == count and sample0 % n_side == 0
        blk0 = sample0 // n_side
        for a in small[:6]:
            args.append(a)
            specs.append(pl.BlockSpec((n_side,) + a.shape[1:],
                                      lambda i, nd=a.ndim: (blk0 + i,) + (0,) * (nd - 1)))
            blocks += n_side * _nbytes(a.shape[1:-1] + (max(a.shape[-1], LANES),), F32)
        for a in small[6:]:
            args.append(a)
            specs.append(pl.BlockSpec(a.shape, lambda i, nd=a.ndim: (0,) * nd))
            blocks += _nbytes(a.shape, F32)
        args += [ak, av]
        specs += [pl.BlockSpec(memory_space=pl.ANY)] * 2
        q3 = small[0]
        out_specs = [row_spec, pl.BlockSpec((n_side,) + q3.shape[1:], lambda i: (i, 0, 0))]
        out_shape = [out_shape, jax.ShapeDtypeStruct((count,) + q3.shape[1:], F32)]
        sa_shape = (n_side, N_HEADS_A, ak.shape[3])
        sb_shape = (n_side, N_HEADS_B, small[4].shape[3])
        scratch_shapes += [pltpu.VMEM((2,) + ak.shape[1:], F32), pltpu.VMEM((2,) + av.shape[1:], F32),
                           pltpu.SemaphoreType.DMA((2, 2)),
                           pltpu.VMEM(sa_shape, F32), pltpu.VMEM(sb_shape, F32),
                           pltpu.VMEM(sa_shape, F32), pltpu.VMEM(sb_shape, F32)]
        scratch += 4 * _nbytes(ak.shape[1:], F32) + 2 * _nbytes(sa_shape, F32) + 2 * _nbytes(sb_shape, F32)
    return pl.pallas_call(
        functools.partial(_ffn_kernel, alpha=alpha, pre_mix=pre_mix, mix_slots=mix_slots,
                          n_chunks=n_chunks, fc=fc, n_side=n_side, sample0=sample0,
                          mod_row=None if per_row else (seq0, tiles_per_seq)),
        grid=(n // tm,),
        in_specs=specs,
        out_specs=out_specs,
        out_shape=out_shape,
        scratch_shapes=scratch_shapes,
        compiler_params=pltpu.CompilerParams(
            dimension_semantics=("arbitrary",),
            vmem_limit_bytes=_vmem_limit(blocks, scratch, temps)),
        name=("ffn_mix" if pre_mix else "ffn") + ("_cached" if n_side else ""),
    )(*args)


def _proj_prompt_kernel(x_ref, sh_ref, sc_ref, w_ref, p_ref, p4_ref, p16_ref, ak_ref, av_ref, bk_ref, bv_ref,
                        stage_ref, stage4_ref, *, tm, b_rows, mod_row):
    h = (x_ref[...] * (1.0 + _seq_mod(sc_ref, mod_row)) + _seq_mod(sh_ref, mod_row)).astype(BF16)
    lane_lo = lax.broadcasted_iota(jnp.int32, (tm, LANES), 1) < HEAD_DIM
    for g in range(4):
        res = jnp.dot(h, w_ref[:, g * A_COLS:(g + 1) * A_COLS], preferred_element_type=F32)
        scaled = res * SCALE if g in (0, 3) else res
        for s in range(N_PAIRS):
            slab = scaled[:, s * LANES:(s + 1) * LANES]
            p_ref[g * N_PAIRS + s] = slab.astype(BF16)
            if g < 3:
                slot = g * N_PAIRS + s
                d4, d16 = p4_ref.shape[1], p16_ref.shape[1]
                stage_ref[slot] = slab
                for r in range(d4):
                    rows = stage_ref[slot, pl.ds(r, tm // d4, stride=d4), :]
                    p4_ref[slot, r] = rows.astype(BF16)
                    stage4_ref[slot, r] = rows
                for r in range(d16):
                    rows = stage4_ref[slot, r % d4, pl.ds(r // d4, tm // d16, stride=d16 // d4), :]
                    p16_ref[slot, r] = rows.astype(BF16)
        if g in (1, 2):
            out_ref = ak_ref if g == 1 else av_ref
            out_ref[...] = res.T.reshape(out_ref.shape)
    res = jnp.dot(h, w_ref[:, 4 * A_COLS:], preferred_element_type=F32)
    for j, out_ref in enumerate((bk_ref, bv_ref)):
        slab = res[:, j * LANES:(j + 1) * LANES]
        swapped = pltpu.roll(slab, HEAD_DIM, axis=1)
        p_ref[4 * N_PAIRS + N_KV_B * j] = jnp.where(lane_lo, slab, swapped).astype(BF16)
        p_ref[4 * N_PAIRS + N_KV_B * j + 1] = jnp.where(lane_lo, swapped, slab).astype(BF16)

        out_ref[...] = slab[tm - b_rows:, :].T.reshape(out_ref.shape)


def _proj_prompt_call(x, mod2, seq0, seq, w_ext, a_rows, b_rows, tm):
    n, d = x.shape
    batch = n // seq
    tiles_per_seq = seq // tm
    a_tiles = a_rows // tm
    n_slots = 4 * N_PAIRS + 2 * N_KV_B
    assert w_ext.shape[1] == 4 * A_COLS + 2 * LANES and N_KV_B * HEAD_DIM == LANES

    def mod_spec(k):
        return _seq_mod_spec(seq0, tiles_per_seq, k, d)

    def a_map(i):
        return (i // tiles_per_seq, 0, 0, jnp.maximum(i % tiles_per_seq - (tiles_per_seq - a_tiles), 0))

    a_blk = (None, N_HEADS_A, HEAD_DIM, tm)
    b_blk = (None, N_KV_B, HEAD_DIM, b_rows)
    a_slots = 3 * N_PAIRS
    d4, d16 = A_PATTERNS[1][1], A_PATTERNS[2][1]
    assert tm % (16 * d16) == 0

    def dil_spec(dil):
        return pl.BlockSpec((a_slots, None, dil, tm // dil, LANES),
                            lambda i: (0, i // tiles_per_seq, 0, i % tiles_per_seq, 0))

    blocks = (_nbytes((tm, d), F32) + _nbytes(w_ext.shape, BF16) + _nbytes((n_slots + 2 * a_slots, tm, LANES), BF16)
              + 2 * _nbytes((tm, A_COLS), F32) + 2 * _nbytes((b_rows, LANES), F32))
    return pl.pallas_call(
        functools.partial(_proj_prompt_kernel, tm=tm, b_rows=b_rows, mod_row=(seq0, tiles_per_seq)),
        grid=(n // tm,),
        in_specs=[pl.BlockSpec((tm, d), lambda i: (i, 0)), mod_spec(3), mod_spec(4),
                  pl.BlockSpec(w_ext.shape, lambda i: (0, 0))],
        out_specs=[pl.BlockSpec((n_slots, tm, LANES), lambda i: (0, i, 0)),
                   dil_spec(d4), dil_spec(d16),
                   pl.BlockSpec(a_blk, a_map),
                   pl.BlockSpec(a_blk, a_map),
                   pl.BlockSpec(b_blk, lambda i: (i // tiles_per_seq, 0, 0, 0)),
                   pl.BlockSpec(b_blk, lambda i: (i // tiles_per_seq, 0, 0, 0))],
        out_shape=[jax.ShapeDtypeStruct((n_slots, n, LANES), BF16),
                   jax.ShapeDtypeStruct((a_slots, batch, d4, seq // d4, LANES), BF16),
                   jax.ShapeDtypeStruct((a_slots, batch, d16, seq // d16, LANES), BF16),
                   jax.ShapeDtypeStruct((batch, N_HEADS_A, HEAD_DIM, a_rows), F32),
                   jax.ShapeDtypeStruct((batch, N_HEADS_A, HEAD_DIM, a_rows), F32),
                   jax.ShapeDtypeStruct((batch, N_KV_B, HEAD_DIM, b_rows), F32),
                   jax.ShapeDtypeStruct((batch, N_KV_B, HEAD_DIM, b_rows), F32)],
        scratch_shapes=[pltpu.VMEM((a_slots, tm, LANES), F32), pltpu.VMEM((a_slots, d4, tm // d4, LANES), F32)],
        compiler_params=pltpu.CompilerParams(
            dimension_semantics=("arbitrary",),
            vmem_limit_bytes=_vmem_limit(blocks, 2 * _nbytes((a_slots, tm, LANES), F32),
                                         4 * _nbytes((tm, A_COLS), F32))),
        name="proj_prompt",
    )(x, mod2, mod2, w_ext)


def _proj_sample_kernel(x_ref, sh_ref, sc_ref, w_ref, o_ref):
    h = (x_ref[...] * (1.0 + sc_ref[...]) + sh_ref[...]).astype(BF16)
    o_ref[...] = jnp.dot(h, w_ref[...], preferred_element_type=F32)


def _proj_sample_call(x, mod2, w_ext):
    n, d = x.shape
    cols = w_ext.shape[1]
    blocks = 3 * _nbytes((n, d), F32) + _nbytes(w_ext.shape, BF16) + _nbytes((n, cols), F32)
    return pl.pallas_call(
        _proj_sample_kernel,
        grid=(1,),
        in_specs=[pl.BlockSpec((n, d), lambda i: (0, 0)),
                  pl.BlockSpec((n, d), lambda i: (0, 3)),
                  pl.BlockSpec((n, d), lambda i: (0, 4)),
                  pl.BlockSpec(w_ext.shape, lambda i: (0, 0))],
        out_specs=pl.BlockSpec((n, cols), lambda i: (0, 0)),
        out_shape=jax.ShapeDtypeStruct((n, cols), F32),
        compiler_params=pltpu.CompilerParams(
            dimension_semantics=("arbitrary",), vmem_limit_bytes=_vmem_limit(blocks)),
        name="proj_sample",
    )(x, mod2, mod2, w_ext)


def _pair_scores(q, kk, bias, lane_lo):
    zero = jnp.zeros_like(q)
    q2 = jnp.concatenate([jnp.where(lane_lo, q, zero), jnp.where(lane_lo, zero, q)], axis=0)
    s = lax.dot_general(q2, kk, (((1,), (1,)), ((), ())), preferred_element_type=F32)
    return s + bias


def _pair_select(a, lane_lo):
    return jnp.where(lane_lo, a[:BLK], a[BLK:])


GROUP = 16


def _band_sequence(q_ref, k_ref, v_ref, n_blocks, load_bias, block, prefetch=lambda jb: None):
    def one(jb, first, pre):
        if first:
            block(jb, q_ref[0:BLK, :], k_ref[0:BLK, :], v_ref[0:BLK, :], load_bias(True), pre)
        else:
            r0 = jb * BLK
            if not isinstance(jb, int):
                r0 = pl.multiple_of(r0, BLK)
            block(jb, q_ref[pl.ds(r0, BLK), :], k_ref[pl.ds(r0 - BLK, 2 * BLK), :],
                  v_ref[pl.ds(r0 - BLK, 2 * BLK), :], load_bias(False), pre)

    def run(jbs, has_first):
        pres = [prefetch(jb) for jb in jbs]
        for j, (jb, pre) in enumerate(zip(jbs, pres)):
            one(jb, has_first and j == 0, pre)

    run(list(range(min(GROUP, n_blocks))), True)
    if n_blocks > GROUP:
        assert n_blocks % GROUP == 0

        def group(g, carry):
            run([g * GROUP + j for j in range(GROUP)], False)
            return carry

        lax.fori_loop(1, n_blocks // GROUP, group, 0)


def _attn_a_kernel(q16_ref, k16_ref, v16_ref, q4_ref, k4_ref, v4_ref, q1_ref, k1_ref, v1_ref,
                   bias_ref, o_ref, acc16_ref, m16_ref, den16_ref, acc4_ref, m4_ref, den4_ref, *, seq):
    lane_lo = lax.broadcasted_iota(jnp.int32, (BLK, LANES), 1) < HEAD_DIM
    state16 = (acc16_ref, m16_ref, den16_ref)
    state4 = (acc4_ref, m4_ref, den4_ref)

    def update(q, kk, vv, bias, rows, old, new):
        s = _pair_scores(q, kk, bias, lane_lo)
        part = s[:, :LANES] if s.shape[1] == LANES else jnp.maximum(s[:, :LANES], s[:, LANES:])
        if old is not None:
            acc_old, m_old, den_old = old
            part = jnp.maximum(part, jnp.concatenate([jnp.where(lane_lo, m_old, NEG),
                                                      jnp.where(lane_lo, NEG, m_old)], axis=0))
        m2 = jnp.max(part, axis=-1, keepdims=True)
        p = jnp.exp(s - m2)
        l2 = jnp.sum(p, axis=-1, keepdims=True)
        pv = _pair_select(jnp.dot(p.astype(BF16), vv, preferred_element_type=F32), lane_lo)
        m_new = _pair_select(m2, lane_lo)
        den = _pair_select(l2, lane_lo)
        if old is None:
            acc = pv
        else:
            a = jnp.exp(m_old - m_new)
            acc = a * acc_old + pv
            den = a * den_old + den
        if new is None:
            o_ref[rows, :] = (acc / den).astype(o_ref.dtype)
        else:
            new[0][rows, :] = acc
            new[1][rows, :] = m_new
            new[2][rows, :] = den

    def branch(q_ref, k_ref, v_ref, table, dil, r, old, new):
        def rows(jb):
            start = r + jb * (BLK * dil)
            return pl.ds(start, BLK, stride=dil) if dil > 1 else pl.ds(start, BLK)

        def prefetch(jb):
            return None if old is None else tuple(ref[rows(jb), :] for ref in old)

        def block(jb, q, kk, vv, bias, pre):
            update(q, kk, vv, bias, rows(jb), pre, new)

        def load_bias(first):
            return bias_ref[table, :, BLK:2 * BLK] if first else bias_ref[table]

        _band_sequence(q_ref, k_ref, v_ref, q_ref.shape[0] // BLK, load_bias, block, prefetch)

    def dilated(q_ref, k_ref, v_ref, table, old, new):
        dil = q_ref.shape[0]
        per_group = min(max(GROUP // (q_ref.shape[1] // BLK), 1), dil)
        assert dil % per_group == 0

        def group(g, carry):
            for rr in range(per_group):
                r = g * per_group + rr
                branch(q_ref.at[r], k_ref.at[r], v_ref.at[r], table, dil, r, old, new)
            return carry

        lax.fori_loop(0, dil // per_group, group, 0)

    dilated(q16_ref, k16_ref, v16_ref, 2, None, state16)
    dilated(q4_ref, k4_ref, v4_ref, 1, state16, state4)
    branch(q1_ref, k1_ref, v1_ref, 0, 1, 0, state4, None)


def _attn_a_call(p_slots, p4, p16, bias_a, batch, seq):
    n_slots = p_slots.shape[0]
    dils = [d for _, d in A_PATTERNS]
    assert dils == [1, p4.shape[2], p16.shape[2]] and seq % (dils[2] * BLK) == 0
    view1 = p_slots.reshape(n_slots, batch, seq, LANES)

    in_specs, args = [], []
    for arr in (p16, p4, view1):
        for slot0 in (0, N_PAIRS, 2 * N_PAIRS):
            in_specs.append(pl.BlockSpec((None, None) + arr.shape[2:],
                                         lambda b, hp, slot0=slot0, nd=arr.ndim: (slot0 + hp, b) + (0,) * (nd - 2)))
            args.append(arr)
    in_specs.append(pl.BlockSpec((len(A_PATTERNS), None, 2 * BLK, 2 * BLK), lambda b, hp: (0, hp, 0, 0)))
    args.append(bias_a)
    blocks = 10 * _nbytes((seq, LANES), BF16) + _nbytes((3, 2 * BLK, 2 * BLK), F32)
    scratch = 6 * _nbytes((seq, LANES), F32)
    return pl.pallas_call(
        functools.partial(_attn_a_kernel, seq=seq),
        grid=(batch, N_PAIRS),
        in_specs=in_specs,
        out_specs=pl.BlockSpec((None, seq, LANES), lambda b, hp: (hp, b, 0)),
        out_shape=jax.ShapeDtypeStruct((N_PAIRS, batch * seq, LANES), BF16),
        scratch_shapes=[pltpu.VMEM((seq, LANES), F32)] * 6,
        compiler_params=pltpu.CompilerParams(
            dimension_semantics=("arbitrary", "arbitrary"),
            vmem_limit_bytes=_vmem_limit(blocks, scratch, 4 * GROUP * _nbytes((2 * BLK, 2 * BLK), F32))),
        name="attn_dilated",
    )(*args)


def _attn_b_kernel(sink_ref, q_ref, k_ref, v_ref, bias_ref, o_ref, *, seq):
    hp = pl.program_id(1)
    lane_lo = lax.broadcasted_iota(jnp.int32, (BLK, LANES), 1) < HEAD_DIM
    row_lo = lax.broadcasted_iota(jnp.int32, (2 * BLK, 1), 0) < BLK
    sink2 = jnp.where(row_lo, sink_ref[2 * hp], sink_ref[2 * hp + 1])

    def block(jb, q, kk, vv, bias, pre):
        s = _pair_scores(q, kk, bias, lane_lo)
        m2 = jnp.maximum(jnp.max(s, axis=-1, keepdims=True), sink2)
        p = jnp.exp(s - m2)
        l2 = jnp.sum(p, axis=-1, keepdims=True) + jnp.exp(sink2 - m2)
        pv = _pair_select(jnp.dot(p.astype(BF16), vv, preferred_element_type=F32), lane_lo)
        r0 = jb * BLK if isinstance(jb, int) else pl.multiple_of(jb * BLK, BLK)
        o_ref[pl.ds(r0, BLK), :] = (pv / _pair_select(l2, lane_lo)).astype(o_ref.dtype)

    def load_bias(first):
        return bias_ref[:, BLK:2 * BLK] if first else bias_ref[...]

    _band_sequence(q_ref, k_ref, v_ref, seq // BLK, load_bias, block)


def _attn_b_call(p_slots, bias_b, sinks, batch, seq):
    n_slots = p_slots.shape[0]
    view = p_slots.reshape(n_slots, batch, seq, LANES)
    q0, k0, v0 = 3 * N_PAIRS, 4 * N_PAIRS, 4 * N_PAIRS + N_KV_B
    pairs_per_kv = N_PAIRS // N_KV_B
    blk = (None, None, seq, LANES)
    blocks = 4 * _nbytes((seq, LANES), BF16) + _nbytes((2 * BLK, 2 * BLK), F32)
    return pl.pallas_call(
        functools.partial(_attn_b_kernel, seq=seq),
        grid=(batch, N_PAIRS),
        in_specs=[pl.BlockSpec(memory_space=pltpu.SMEM),
                  pl.BlockSpec(blk, lambda b, hp: (q0 + hp, b, 0, 0)),
                  pl.BlockSpec(blk, lambda b, hp: (k0 + hp // pairs_per_kv, b, 0, 0)),
                  pl.BlockSpec(blk, lambda b, hp: (v0 + hp // pairs_per_kv, b, 0, 0)),
                  pl.BlockSpec((None, 2 * BLK, 2 * BLK), lambda b, hp: (hp, 0, 0))],
        out_specs=pl.BlockSpec((None, seq, LANES), lambda b, hp: (hp, b, 0)),
        out_shape=jax.ShapeDtypeStruct((N_PAIRS, batch * seq, LANES), BF16),
        compiler_params=pltpu.CompilerParams(
            dimension_semantics=("arbitrary", "arbitrary"),
            vmem_limit_bytes=_vmem_limit(blocks, temp_bytes=8 * _nbytes((2 * BLK, 2 * BLK), F32))),
        name="attn_window",
    )(sinks, view, view, view, bias_b)


def _cached_sample(i, ak_ref, av_ref, q_ref, qt_ref, knt_ref, vn_ref, bk_ref, bv_ref, bias_a_ref, bias_b_ref,
                   relt_ref, sink_ref, o_ref, sa_ref, sb_ref, wa_ref, wb_ref):
    n_branches = bias_a_ref.shape[0]
    heads_per_kv = N_HEADS_B // N_KV_B
    bias0 = relt_ref[:, 0:1]
    sink = sink_ref[...]
    q = q_ref[i] * SCALE
    vn = vn_ref[i]
    s0 = jnp.sum(qt_ref[i] * SCALE * knt_ref[i], axis=-1, keepdims=True) + bias0
    for h in range(N_HEADS_A):
        sa_ref[i, h:h + 1, :] = jnp.sum(ak_ref[h] * q[:, h:h + 1], axis=0, keepdims=True)
    for hb in range(N_HEADS_B):
        h = N_HEADS_A + hb
        sb_ref[i, hb:hb + 1, :] = jnp.sum(bk_ref[i, hb // heads_per_kv] * q[:, h:h + 1], axis=0, keepdims=True)

    s0a = s0[:N_HEADS_A]
    ts = [sa_ref[i] + bias_a_ref[t] for t in range(n_branches)]
    m = s0a
    for t_ in ts:
        m = jnp.maximum(m, jnp.max(t_, axis=-1, keepdims=True))
    w = jnp.exp(ts[0] - m)
    for t_ in ts[1:]:
        w = w + jnp.exp(t_ - m)
    p0a = n_branches * jnp.exp(s0a - m)
    inv = 1.0 / (jnp.sum(w, axis=-1, keepdims=True) + p0a)
    wa_ref[i] = w * inv
    p0a = p0a * inv

    s0b = s0[N_HEADS_A:]
    tb = sb_ref[i] + bias_b_ref[...]
    mb = jnp.maximum(jnp.maximum(s0b, sink), jnp.max(tb, axis=-1, keepdims=True))
    wb = jnp.exp(tb - mb)
    p0b = jnp.exp(s0b - mb)
    invb = 1.0 / (jnp.sum(wb, axis=-1, keepdims=True) + p0b + jnp.exp(sink - mb))
    wb_ref[i] = wb * invb
    p0b = p0b * invb

    cols = []
    for h in range(N_HEADS_A):
        acc = jnp.sum(av_ref[h] * wa_ref[i, h:h + 1, :], axis=-1, keepdims=True)
        cols.append(acc + p0a[h:h + 1, :] * vn[:, h:h + 1])
    for hb in range(N_HEADS_B):
        h = N_HEADS_A + hb
        acc = jnp.sum(bv_ref[i, hb // heads_per_kv] * wb_ref[i, hb:hb + 1, :], axis=-1, keepdims=True)
        cols.append(acc + p0b[hb:hb + 1, :] * vn[:, h:h + 1])
    o_ref[i] = jnp.concatenate(cols, axis=1)


def kernel(x_prompt, x_sample, cache_a_k, cache_a_v, cache_b_k, cache_b_v, c_prompt, c_sample, rel_bias, w_ada, b_ada, ffn1_wg, ffn1_wu, ffn1_wd, w_in, w_out, sinks, ffn2_wg, ffn2_wu, ffn2_wd, ln_g, ln_b):
    batch, seq, d = x_prompt.shape
    n_dec, dec_seq, _ = x_sample.shape
    depth = w_ada.shape[0]
    assert dec_seq == 1 and d == 2 * A_COLS and all(w // dl == BLK for w, dl in A_PATTERNS)
    alpha = (2 * depth) ** 0.25
    fc = 256
    tm = 512
    a_rows, b_rows = min(A_PATTERNS[-1][0], seq), min(B_WINDOW, seq)
    q_cols = 4 * A_COLS
    kvb = N_KV_B * HEAD_DIM
    heads_per_kv = N_HEADS_B // N_KV_B

    band_map, cached_a, cached_b = _bucket_maps(cache_a_k.shape[2], cache_b_k.shape[2])
    band_bias = _band_bias_call(rel_bias, band_map)
    bias_a, bias_b = band_bias[:len(A_PATTERNS)], band_bias[len(A_PATTERNS)]
    bias_sa, bias_sb = _cached_bias_call(rel_bias, cached_a, cached_b)

    pad = (-(n_dec + batch)) % 16
    c_all = jnp.concatenate([c_sample, c_prompt, jnp.zeros((pad, d), F32)], axis=0)
    xp = x_prompt.reshape(batch * seq, d)
    xs = x_sample.reshape(n_dec, d)
    outs = [[] for _ in range(8)]

    def head_rows(x, n_heads):
        return x.reshape(n_dec, n_heads, HEAD_DIM)

    for l in range(depth):
        mod2 = _ada_call(c_all, w_ada[l], b_ada[l])
        ffn1 = [w[l].astype(BF16) for w in (ffn1_wg, ffn1_wu, ffn1_wd)]
        ffn2 = [w[l].astype(BF16) for w in (ffn2_wg, ffn2_wu, ffn2_wd)]
        w_in_s = w_in[l].astype(BF16)
        wo = w_out[l].astype(BF16)

        s1 = _ffn_call(xs, mod2, (0, 1, 2), 0, 1, *ffn1, ln_g[l, 0], ln_b[l, 0], alpha, n_dec, fc)
        proj_s = _proj_sample_call(s1, mod2, w_in_s)
        c = A_COLS
        sak, sav = proj_s[:, c:2 * c], proj_s[:, 2 * c:3 * c]
        sbk, sbv = proj_s[:, q_cols:q_cols + kvb], proj_s[:, q_cols + kvb:q_cols + 2 * kvb]
        qt3 = jnp.concatenate([head_rows(proj_s[:, :c], N_HEADS_A),
                               head_rows(proj_s[:, 3 * c:q_cols], N_HEADS_B)], axis=1)
        knt3 = jnp.concatenate([head_rows(sak, N_HEADS_A),
                                jnp.repeat(head_rows(sbk, N_KV_B), heads_per_kv, axis=1)], axis=1)
        vnt3 = jnp.concatenate([head_rows(sav, N_HEADS_A),
                                jnp.repeat(head_rows(sbv, N_KV_B), heads_per_kv, axis=1)], axis=1)
        to_cols = (0, 2, 3, 1)
        decode_side = (qt3.transpose(0, 2, 1), qt3, knt3, vnt3.transpose(0, 2, 1),
                       cache_b_k[l].transpose(to_cols), cache_b_v[l].transpose(to_cols),
                       bias_sa, bias_sb, rel_bias.T, sinks[l].reshape(N_HEADS_B, 1),
                       cache_a_k[l].transpose(to_cols), cache_a_v[l].transpose(to_cols))

        n_first = n_dec // 2 if (n_dec // 2) % (2 * (batch * seq // tm)) == 0 else n_dec
        x1, mix3 = _ffn_call(xp, mod2, (0, 1, 2), n_dec, seq, *ffn1, ln_g[l, 0], ln_b[l, 0], alpha, tm, fc,
                             side=(decode_side, 0, n_first))
        p_slots, p4, p16, pak, pav, pbk, pbv = _proj_prompt_call(x1, mod2, n_dec, seq, w_in_s, a_rows, b_rows, tm)
        mix_a = _attn_a_call(p_slots, p4, p16, bias_a, batch, seq)
        mix_b = _attn_b_call(p_slots, bias_b, sinks[l], batch, seq)
        xp = _ffn_call(x1, mod2, (6, 7, 8), n_dec, seq, *ffn2, ln_g[l, 2], ln_b[l, 2], alpha, tm, fc,
                       mix=(mix_a, mix_b), w_out=wo, gate_k=5, ln1_g=ln_g[l, 1], ln1_b=ln_b[l, 1],
                       side=(decode_side, n_first, n_dec - n_first) if n_first < n_dec else None)
        if n_first < n_dec:
            xp, mix3_rest = xp
            mix3 = jnp.concatenate([mix3, mix3_rest], axis=0)

        mix_s = mix3.transpose(0, 2, 1).reshape(n_dec, d)
        xs = _ffn_call(s1, mod2, (6, 7, 8), 0, 1, *ffn2, ln_g[l, 2], ln_b[l, 2], alpha, n_dec, fc,
                       mix=mix_s, w_out=wo, gate_k=5, ln1_g=ln_g[l, 1], ln1_b=ln_b[l, 1])

        to_rows = (0, 3, 1, 2)
        new = [pak.transpose(to_rows), pav.transpose(to_rows), pbk.transpose(to_rows), pbv.transpose(to_rows),
               sak.reshape(n_dec, 1, N_HEADS_A, HEAD_DIM), sav.reshape(n_dec, 1, N_HEADS_A, HEAD_DIM),
               sbk.reshape(n_dec, 1, N_KV_B, HEAD_DIM), sbv.reshape(n_dec, 1, N_KV_B, HEAD_DIM)]
        for acc, arr in zip(outs, new):
            acc.append(arr)

    return (xp.reshape(batch, seq, d), xs.reshape(n_dec, 1, d)) + tuple(jnp.stack(o) for o in outs)
```

```python
import functools
import math

import jax
import jax.numpy as jnp
import numpy as np
from jax import lax
from jax.experimental import pallas as pl
from jax.experimental.pallas import tpu as pltpu

HEAD_DIM = 64
N_HEADS_A = 8
N_HEADS_B = 8
N_KV_B = 2
A_PATTERNS = ((128, 1), (512, 4), (2048, 16))
B_WINDOW = 128
BLK = 128
NUM_BUCKETS = 32
MAX_DISTANCE = 2048
LN_EPS = 1e-5
NEG = -1e30
SCALE = HEAD_DIM ** -0.5

LANES = 128
SUBLANES = 8
V7X_VMEM_BYTES = 64 * 1024 * 1024
A_COLS = N_HEADS_A * HEAD_DIM
N_PAIRS = A_COLS // LANES
N_MOD = 9

F32 = jnp.float32
BF16 = jnp.bfloat16


def _vmem_limit(pipelined_bytes, scratch_bytes=0, temp_bytes=0):
    want = 2 * pipelined_bytes + scratch_bytes + temp_bytes + (4 << 20)
    return int(min(want, V7X_VMEM_BYTES - (6 << 20)))


def _resident_spec(shape):
    return pl.BlockSpec(shape, lambda *_: (0,) * len(shape), pipeline_mode=pl.Buffered(1))


def _nbytes(shape, dtype):
    return math.prod(shape) * jnp.dtype(dtype).itemsize


def _layer_norm(x, g, b):
    mu = jnp.mean(x, axis=-1, keepdims=True)
    xc = x - mu
    var = jnp.mean(xc * xc, axis=-1, keepdims=True)
    return xc * lax.rsqrt(var + LN_EPS) * g + b


def _silu(x):
    return x * jax.nn.sigmoid(x)


def _seq_mod_spec(seq0, tiles_per_seq, k, d):
    return pl.BlockSpec((SUBLANES, d), lambda i: ((seq0 + i // tiles_per_seq) // SUBLANES, k))


def _seq_mod(ref, mod_row):
    if mod_row is None:
        return ref[...]
    seq0, tiles_per_seq = mod_row
    return ref[pl.ds((seq0 + pl.program_id(0) // tiles_per_seq) % SUBLANES, 1), :]


def _ada_kernel(c_ref, w_ref, b_ref, o_ref):
    a = _silu(c_ref[...]).astype(BF16)
    o_ref[...] = jnp.dot(a, w_ref[...].astype(BF16), preferred_element_type=F32) + b_ref[...]


def _ada_call(c_all, w_ada, b_ada):
    m, d = c_all.shape
    n = w_ada.shape[1]
    tn = n // 8
    blocks = _nbytes((m, d), F32) + _nbytes((d, tn), F32) + _nbytes((m, tn), F32)
    return pl.pallas_call(
        _ada_kernel,
        grid=(n // tn,),
        in_specs=[pl.BlockSpec((m, d), lambda j: (0, 0)),
                  pl.BlockSpec((d, tn), lambda j: (0, j)),
                  pl.BlockSpec((1, tn), lambda j: (0, j))],
        out_specs=pl.BlockSpec((m, tn), lambda j: (0, j)),
        out_shape=jax.ShapeDtypeStruct((m, n), F32),
        compiler_params=pltpu.CompilerParams(
            dimension_semantics=("arbitrary",),
            vmem_limit_bytes=_vmem_limit(blocks, temp_bytes=_nbytes((d, tn), BF16))),
        name="ada_mod",
    )(c_all, w_ada, b_ada.reshape(1, n))


def _t5_bucket(n):
    max_exact = NUM_BUCKETS // 2
    nf = np.maximum(n, 1).astype(np.float32)
    large = max_exact + (np.log(nf / max_exact) / math.log(MAX_DISTANCE / max_exact)
                         * (NUM_BUCKETS - max_exact)).astype(np.int32)
    return np.where(n < max_exact, n, np.minimum(large, NUM_BUCKETS - 1))


def _bucket_maps(a_rows, b_rows):
    steps = [d for _, d in A_PATTERNS] + [1]
    q = np.arange(BLK)[:, None]
    k = np.arange(2 * BLK)[None, :]
    dist = q + BLK - k
    valid = (dist >= 0) & (dist <= BLK)
    band = np.stack([np.where(valid, _t5_bucket(np.maximum(dist, 0) * s), -1) for s in steps])

    def cached(rows, step):
        back = rows - np.arange(rows)
        hit = (back % step == 0) & (back // step <= BLK)
        return np.where(hit, _t5_bucket(back), -1)[None, :].astype(np.int32)

    cached_a = np.stack([cached(a_rows, s) for s in steps[:-1]])
    return band.astype(np.int32), cached_a, cached(b_rows, 1)


def _band_bias_kernel(rel_ref, bmap_ref, o_ref):
    head0 = jnp.where(pl.program_id(0) == len(A_PATTERNS), N_HEADS_A, 0)
    rows = 2 * SUBLANES
    for r0 in range(0, BLK, rows):
        bm = bmap_ref[r0:r0 + rows, :]
        accs = [jnp.full(bm.shape, NEG, F32)] * (2 * N_PAIRS)
        for b in range(NUM_BUCKETS):
            hit = bm == b
            accs = [jnp.where(hit, rel_ref[b, head0 + h], acc) for h, acc in enumerate(accs)]
        for h, acc in enumerate(accs):
            o_ref[h // 2, (h % 2) * BLK + r0:(h % 2) * BLK + r0 + rows, :] = acc


def _band_bias_call(rel_bias, band_map):
    n_tab = band_map.shape[0]
    return pl.pallas_call(
        _band_bias_kernel,
        grid=(n_tab,),
        in_specs=[pl.BlockSpec(memory_space=pltpu.SMEM),
                  pl.BlockSpec((None, BLK, 2 * BLK), lambda t: (t, 0, 0))],
        out_specs=pl.BlockSpec((None, N_PAIRS, 2 * BLK, 2 * BLK), lambda t: (t, 0, 0, 0)),
        out_shape=jax.ShapeDtypeStruct((n_tab, N_PAIRS, 2 * BLK, 2 * BLK), F32),
        compiler_params=pltpu.CompilerParams(dimension_semantics=("arbitrary",)),
        name="band_bias",
    )(rel_bias, band_map)


def _cached_bias_kernel(relt_ref, amap_ref, bmap_ref, oa_ref, ob_ref):
    relt = relt_ref[...]

    def table(bm, heads):
        acc = jnp.full(bm.shape, NEG, F32)
        for b in range(NUM_BUCKETS):
            acc = jnp.where(bm == b, heads[:, b:b + 1], acc)
        return acc

    for t in range(amap_ref.shape[0]):
        oa_ref[t] = table(jnp.broadcast_to(amap_ref[t], oa_ref.shape[1:]), relt[:N_HEADS_A])
    ob_ref[...] = table(jnp.broadcast_to(bmap_ref[...], ob_ref.shape), relt[N_HEADS_A:])


def _cached_bias_call(rel_bias, cached_a, cached_b):
    return pl.pallas_call(
        _cached_bias_kernel,
        out_shape=(jax.ShapeDtypeStruct((cached_a.shape[0], N_HEADS_A, cached_a.shape[2]), F32),
                   jax.ShapeDtypeStruct((N_HEADS_B, cached_b.shape[1]), F32)),
        name="cached_bias",
    )(rel_bias.T, cached_a, cached_b)


def _ffn_kernel(*refs, alpha, pre_mix, mix_slots, n_chunks, fc, n_side, sample0, mod_row):
    refs = list(refs)
    x_ref = refs.pop(0)
    if pre_mix:
        mix_refs = [refs.pop(0) for _ in range(2 if mix_slots else 1)]
        wo_ref, gm_ref, ln1g_ref, ln1b_ref = (refs.pop(0) for _ in range(4))
    sh_ref, sc_ref, gt_ref, wg_ref, wu_ref, wd_ref, lng_ref, lnb_ref = (refs.pop(0) for _ in range(8))
    if n_side:
        side_in = [refs.pop(0) for _ in range(10)]
        ak_hbm, av_hbm, o_ref, oc_ref, h_ref, acc_ref, kbuf, vbuf, sem = (refs.pop(0) for _ in range(9))
        side_scratch = refs
        end = sample0 + pl.num_programs(0) * n_side
        first = sample0 + pl.program_id(0) * n_side

        def cache_copies(g, slot):
            return (pltpu.make_async_copy(ak_hbm.at[g], kbuf.at[slot], sem.at[0, slot]),
                    pltpu.make_async_copy(av_hbm.at[g], vbuf.at[slot], sem.at[1, slot]))

        @pl.when(pl.program_id(0) == 0)
        def _():
            for cp in cache_copies(sample0, 0):
                cp.start()

        def side(j):
            g, slot = first + j, j % 2
            for cp in cache_copies(g, slot):
                cp.wait()

            @pl.when(g + 1 < end)
            def _():
                for cp in cache_copies(g + 1, 1 - slot):
                    cp.start()

            _cached_sample(j, kbuf.at[slot], vbuf.at[slot], *side_in, oc_ref, *side_scratch)

        assert n_side % 2 == 0
        side_at = {(j * n_chunks) // n_side if j else -1: j for j in range(n_side)}
    else:
        o_ref, h_ref, acc_ref = refs
        side_at = {}

    if -1 in side_at:
        side(side_at[-1])
    x = x_ref[...]
    if pre_mix:
        if mix_slots:
            mixed = jnp.concatenate([r[j] for r in mix_refs for j in range(N_PAIRS)], axis=-1)
        else:
            mixed = mix_refs[0][...].astype(BF16)
        y = jnp.dot(mixed, wo_ref[...], preferred_element_type=F32)
        x = _layer_norm(alpha * x + _seq_mod(gm_ref, mod_row) * y, ln1g_ref[...], ln1b_ref[...])
    h_ref[...] = (x * (1.0 + _seq_mod(sc_ref, mod_row)) + _seq_mod(sh_ref, mod_row)).astype(BF16)
    for c in range(n_chunks):
        if c in side_at:
            side(side_at[c])
        cols = slice(c * fc, (c + 1) * fc)
        g = jnp.dot(h_ref[...], wg_ref[:, cols], preferred_element_type=F32)
        u = jnp.dot(h_ref[...], wu_ref[:, cols], preferred_element_type=F32)
        y = jnp.dot((_silu(g) * u).astype(BF16), wd_ref[cols, :], preferred_element_type=F32)
        if c == 0:
            acc_ref[...] = y
        else:
            acc_ref[...] += y
    o_ref[...] = _layer_norm(alpha * x + 0.5 * _seq_mod(gt_ref, mod_row) * acc_ref[...], lng_ref[...], lnb_ref[...])


def _ffn_call(x, mod2, mod_ks, seq0, rows_per_seq, wg, wu, wd, ln_g, ln_b, alpha, tm, fc,
              mix=None, w_out=None, gate_k=None, ln1_g=None, ln1_b=None, side=None):
    n, d = x.shape
    d_ff = wg.shape[1]
    n_chunks = d_ff // fc
    assert n_chunks * fc == d_ff
    per_row = rows_per_seq == 1
    tiles_per_seq = max(rows_per_seq // tm, 1)

    def mod_spec(k):
        if per_row:
            return pl.BlockSpec((tm, d), lambda i, k=k: (seq0 // tm + i, k))
        return _seq_mod_spec(seq0, tiles_per_seq, k, d)

    mod_arr = mod2
    row_spec = pl.BlockSpec((tm, d), lambda i: (i, 0))
    vec_spec = pl.BlockSpec((1, d), lambda i: (0, 0))
    args, specs = [x], [row_spec]
    pre_mix = mix is not None
    mix_slots = pre_mix and isinstance(mix, tuple)
    blocks = 2 * _nbytes((tm, d), F32) + (3 * _nbytes((tm, d), F32) if per_row else 0)
    resident = 3 * _nbytes(wg.shape, BF16)
    if pre_mix:
        if mix_slots:
            for mm in mix:
                args.append(mm)
                specs.append(pl.BlockSpec((N_PAIRS, tm, LANES), lambda i: (0, i, 0)))
        else:
            args.append(mix)
            specs.append(row_spec)
        args += [w_out, mod_arr, ln1_g.reshape(1, d), ln1_b.reshape(1, d)]
        specs += [_resident_spec((d, d)), mod_spec(gate_k), vec_spec, vec_spec]
        blocks += _nbytes((tm, d), F32)
        resident += _nbytes((d, d), BF16)
    args += [mod_arr, mod_arr, mod_arr, wg, wu, wd, ln_g.reshape(1, d), ln_b.reshape(1, d)]
    specs += [mod_spec(mod_ks[0]), mod_spec(mod_ks[1]), mod_spec(mod_ks[2]),
              _resident_spec(wg.shape), _resident_spec(wu.shape), _resident_spec(wd.shape), vec_spec, vec_spec]
    scratch = _nbytes((tm, d), BF16) + _nbytes((tm, d), F32) + resident
    temps = 6 * _nbytes((tm, fc), F32) + 3 * _nbytes((tm, d), F32)
    out_specs, out_shape = row_spec, jax.ShapeDtypeStruct((n, d), F32)
    scratch_shapes = [pltpu.VMEM((tm, d), BF16), pltpu.VMEM((tm, d), F32)]
    n_side = sample0 = 0
    if side is not None:
        (*small, ak, av), sample0, count = side
        n_steps = n // tm
        n_side = count // n_steps
        assert n_side * n_steps == count and sample0 % n_side == 0
        blk0 = sample0 // n_side
        for a in small[:6]:
            args.append(a)
            specs.append(pl.BlockSpec((n_side,) + a.shape[1:],
                                      lambda i, nd=a.ndim: (blk0 + i,) + (0,) * (nd - 1)))
            blocks += n_side * _nbytes(a.shape[1:-1] + (max(a.shape[-1], LANES),), F32)
        for a in small[6:]:
            args.append(a)
            specs.append(pl.BlockSpec(a.shape, lambda i, nd=a.ndim: (0,) * nd))
            blocks += _nbytes(a.shape, F32)
        args += [ak, av]
        specs += [pl.BlockSpec(memory_space=pl.ANY)] * 2
        q3 = small[0]
        out_specs = [row_spec, pl.BlockSpec((n_side,) + q3.shape[1:], lambda i: (i, 0, 0))]
        out_shape = [out_shape, jax.ShapeDtypeStruct((count,) + q3.shape[1:], F32)]
        sa_shape = (n_side, N_HEADS_A, ak.shape[3])
        sb_shape = (n_side, N_HEADS_B, small[4].shape[3])
        scratch_shapes += [pltpu.VMEM((2,) + ak.shape[1:], F32), pltpu.VMEM((2,) + av.shape[1:], F32),
                           pltpu.SemaphoreType.DMA((2, 2)),
                           pltpu.VMEM(sa_shape, F32), pltpu.VMEM(sb_shape, F32),
                           pltpu.VMEM(sa_shape, F32), pltpu.VMEM(sb_shape, F32)]
        scratch += 4 * _nbytes(ak.shape[1:], F32) + 2 * _nbytes(sa_shape, F32) + 2 * _nbytes(sb_shape, F32)
    return pl.pallas_call(
        functools.partial(_ffn_kernel, alpha=alpha, pre_mix=pre_mix, mix_slots=mix_slots,
                          n_chunks=n_chunks, fc=fc, n_side=n_side, sample0=sample0,
                          mod_row=None if per_row else (seq0, tiles_per_seq)),
        grid=(n // tm,),
        in_specs=specs,
        out_specs=out_specs,
        out_shape=out_shape,
        scratch_shapes=scratch_shapes,
        compiler_params=pltpu.CompilerParams(
            dimension_semantics=("arbitrary",),
            vmem_limit_bytes=_vmem_limit(blocks, scratch, temps)),
        name=("ffn_mix" if pre_mix else "ffn") + ("_cached" if n_side else ""),
    )(*args)


def _proj_prompt_kernel(x_ref, sh_ref, sc_ref, w_ref, p_ref, p4_ref, p16_ref, ak_ref, av_ref, bk_ref, bv_ref,
                        stage_ref, stage4_ref, *, tm, b_rows, mod_row):
    h = (x_ref[...] * (1.0 + _seq_mod(sc_ref, mod_row)) + _seq_mod(sh_ref, mod_row)).astype(BF16)
    lane_lo = lax.broadcasted_iota(jnp.int32, (tm, LANES), 1) < HEAD_DIM
    for g in range(4):
        res = jnp.dot(h, w_ref[:, g * A_COLS:(g + 1) * A_COLS], preferred_element_type=F32)
        scaled = res * SCALE if g in (0, 3) else res
        for s in range(N_PAIRS):
            slab = scaled[:, s * LANES:(s + 1) * LANES]
            p_ref[g * N_PAIRS + s] = slab.astype(BF16)
            if g < 3:
                slot = g * N_PAIRS + s
                d4, d16 = p4_ref.shape[1], p16_ref.shape[1]
                stage_ref[slot] = slab
                for r in range(d4):
                    rows = stage_ref[slot, pl.ds(r, tm // d4, stride=d4), :]
                    p4_ref[slot, r] = rows.astype(BF16)
                    stage4_ref[slot, r] = rows
                for r in range(d16):
                    rows = stage4_ref[slot, r % d4, pl.ds(r // d4, tm // d16, stride=d16 // d4), :]
                    p16_ref[slot, r] = rows.astype(BF16)
        if g in (1, 2):
            out_ref = ak_ref if g == 1 else av_ref
            out_ref[...] = res
    res = jnp.dot(h, w_ref[:, 4 * A_COLS:], preferred_element_type=F32)
    for j, out_ref in enumerate((bk_ref, bv_ref)):
        slab = res[:, j * LANES:(j + 1) * LANES]
        swapped = pltpu.roll(slab, HEAD_DIM, axis=1)
        p_ref[4 * N_PAIRS + N_KV_B * j] = jnp.where(lane_lo, slab, swapped).astype(BF16)
        p_ref[4 * N_PAIRS + N_KV_B * j + 1] = jnp.where(lane_lo, swapped, slab).astype(BF16)

        out_ref[...] = slab[tm - b_rows:, :]


def _proj_prompt_call(x, mod2, seq0, seq, w_ext, a_rows, b_rows, tm):
    n, d = x.shape
    batch = n // seq
    tiles_per_seq = seq // tm
    a_tiles = a_rows // tm
    n_slots = 4 * N_PAIRS + 2 * N_KV_B
    assert w_ext.shape[1] == 4 * A_COLS + 2 * LANES and N_KV_B * HEAD_DIM == LANES

    def mod_spec(k):
        return _seq_mod_spec(seq0, tiles_per_seq, k, d)

    def a_map(i):
        return (i // tiles_per_seq, jnp.maximum(i % tiles_per_seq - (tiles_per_seq - a_tiles), 0), 0)

    a_blk = (None, tm, A_COLS)
    b_blk = (None, b_rows, LANES)
    a_slots = 3 * N_PAIRS
    d4, d16 = A_PATTERNS[1][1], A_PATTERNS[2][1]
    assert tm % (16 * d16) == 0

    def dil_spec(dil):
        return pl.BlockSpec((a_slots, None, dil, tm // dil, LANES),
                            lambda i: (0, i // tiles_per_seq, 0, i % tiles_per_seq, 0))

    blocks = (_nbytes((tm, d), F32) + _nbytes(w_ext.shape, BF16) + _nbytes((n_slots + 2 * a_slots, tm, LANES), BF16)
              + 2 * _nbytes((tm, A_COLS), F32) + 2 * _nbytes((b_rows, LANES), F32))
    return pl.pallas_call(
        functools.partial(_proj_prompt_kernel, tm=tm, b_rows=b_rows, mod_row=(seq0, tiles_per_seq)),
        grid=(n // tm,),
        in_specs=[pl.BlockSpec((tm, d), lambda i: (i, 0)), mod_spec(3), mod_spec(4),
                  pl.BlockSpec(w_ext.shape, lambda i: (0, 0))],
        out_specs=[pl.BlockSpec((n_slots, tm, LANES), lambda i: (0, i, 0)),
                   dil_spec(d4), dil_spec(d16),
                   pl.BlockSpec(a_blk, a_map),
                   pl.BlockSpec(a_blk, a_map),
                   pl.BlockSpec(b_blk, lambda i: (i // tiles_per_seq, 0, 0)),
                   pl.BlockSpec(b_blk, lambda i: (i // tiles_per_seq, 0, 0))],
        out_shape=[jax.ShapeDtypeStruct((n_slots, n, LANES), BF16),
                   jax.ShapeDtypeStruct((a_slots, batch, d4, seq // d4, LANES), BF16),
                   jax.ShapeDtypeStruct((a_slots, batch, d16, seq // d16, LANES), BF16),
                   jax.ShapeDtypeStruct((batch, a_rows, A_COLS), F32),
                   jax.ShapeDtypeStruct((batch, a_rows, A_COLS), F32),
                   jax.ShapeDtypeStruct((batch, b_rows, LANES), F32),
                   jax.ShapeDtypeStruct((batch, b_rows, LANES), F32)],
        scratch_shapes=[pltpu.VMEM((a_slots, tm, LANES), F32), pltpu.VMEM((a_slots, d4, tm // d4, LANES), F32)],
        compiler_params=pltpu.CompilerParams(
            dimension_semantics=("arbitrary",),
            vmem_limit_bytes=_vmem_limit(blocks, 2 * _nbytes((a_slots, tm, LANES), F32),
                                         4 * _nbytes((tm, A_COLS), F32))),
        name="proj_prompt",
    )(x, mod2, mod2, w_ext)


def _proj_sample_kernel(x_ref, sh_ref, sc_ref, w_ref, o_ref):
    h = (x_ref[...] * (1.0 + sc_ref[...]) + sh_ref[...]).astype(BF16)
    o_ref[...] = jnp.dot(h, w_ref[...], preferred_element_type=F32)


def _proj_sample_call(x, mod2, w_ext):
    n, d = x.shape
    cols = w_ext.shape[1]
    blocks = 3 * _nbytes((n, d), F32) + _nbytes(w_ext.shape, BF16) + _nbytes((n, cols), F32)
    return pl.pallas_call(
        _proj_sample_kernel,
        grid=(1,),
        in_specs=[pl.BlockSpec((n, d), lambda i: (0, 0)),
                  pl.BlockSpec((n, d), lambda i: (0, 3)),
                  pl.BlockSpec((n, d), lambda i: (0, 4)),
                  pl.BlockSpec(w_ext.shape, lambda i: (0, 0))],
        out_specs=pl.BlockSpec((n, cols), lambda i: (0, 0)),
        out_shape=jax.ShapeDtypeStruct((n, cols), F32),
        compiler_params=pltpu.CompilerParams(
            dimension_semantics=("arbitrary",), vmem_limit_bytes=_vmem_limit(blocks)),
        name="proj_sample",
    )(x, mod2, mod2, w_ext)


def _pair_scores(q, kk, bias, lane_lo):
    zero = jnp.zeros_like(q)
    q2 = jnp.concatenate([jnp.where(lane_lo, q, zero), jnp.where(lane_lo, zero, q)], axis=0)
    s = lax.dot_general(q2, kk, (((1,), (1,)), ((), ())), preferred_element_type=F32)
    return s + bias


def _pair_select(a, lane_lo):
    return jnp.where(lane_lo, a[:BLK], a[BLK:])


GROUP = 16


def _band_sequence(q_ref, k_ref, v_ref, n_blocks, load_bias, block, prefetch=lambda jb: None):
    def one(jb, first, pre):
        if first:
            block(jb, q_ref[0:BLK, :], k_ref[0:BLK, :], v_ref[0:BLK, :], load_bias(True), pre)
        else:
            r0 = jb * BLK
            if not isinstance(jb, int):
                r0 = pl.multiple_of(r0, BLK)
            block(jb, q_ref[pl.ds(r0, BLK), :], k_ref[pl.ds(r0 - BLK, 2 * BLK), :],
                  v_ref[pl.ds(r0 - BLK, 2 * BLK), :], load_bias(False), pre)

    def run(jbs, has_first):
        pres = [prefetch(jb) for jb in jbs]
        for j, (jb, pre) in enumerate(zip(jbs, pres)):
            one(jb, has_first and j == 0, pre)

    run(list(range(min(GROUP, n_blocks))), True)
    if n_blocks > GROUP:
        assert n_blocks % GROUP == 0

        def group(g, carry):
            run([g * GROUP + j for j in range(GROUP)], False)
            return carry

        lax.fori_loop(1, n_blocks // GROUP, group, 0)


def _attn_a_kernel(q16_ref, k16_ref, v16_ref, q4_ref, k4_ref, v4_ref, q1_ref, k1_ref, v1_ref,
                   bias_ref, o_ref, acc16_ref, m16_ref, den16_ref, acc4_ref, m4_ref, den4_ref, *, seq):
    lane_lo = lax.broadcasted_iota(jnp.int32, (BLK, LANES), 1) < HEAD_DIM
    state16 = (acc16_ref, m16_ref, den16_ref)
    state4 = (acc4_ref, m4_ref, den4_ref)

    def update(q, kk, vv, bias, rows, old, new):
        s = _pair_scores(q, kk, bias, lane_lo)
        part = s[:, :LANES] if s.shape[1] == LANES else jnp.maximum(s[:, :LANES], s[:, LANES:])
        if old is not None:
            acc_old, m_old, den_old = old
            part = jnp.maximum(part, jnp.concatenate([jnp.where(lane_lo, m_old, NEG),
                                                      jnp.where(lane_lo, NEG, m_old)], axis=0))
        m2 = jnp.max(part, axis=-1, keepdims=True)
        p = jnp.exp(s - m2)
        l2 = jnp.sum(p, axis=-1, keepdims=True)
        pv = _pair_select(jnp.dot(p.astype(BF16), vv, preferred_element_type=F32), lane_lo)
        m_new = _pair_select(m2, lane_lo)
        den = _pair_select(l2, lane_lo)
        if old is None:
            acc = pv
        else:
            a = jnp.exp(m_old - m_new)
            acc = a * acc_old + pv
            den = a * den_old + den
        if new is None:
            o_ref[rows, :] = (acc / den).astype(o_ref.dtype)
        else:
            new[0][rows, :] = acc
            new[1][rows, :] = m_new
            new[2][rows, :] = den

    def branch(q_ref, k_ref, v_ref, table, dil, r, old, new):
        def rows(jb):
            start = r + jb * (BLK * dil)
            return pl.ds(start, BLK, stride=dil) if dil > 1 else pl.ds(start, BLK)

        def prefetch(jb):
            return None if old is None else tuple(ref[rows(jb), :] for ref in old)

        def block(jb, q, kk, vv, bias, pre):
            update(q, kk, vv, bias, rows(jb), pre, new)

        def load_bias(first):
            return bias_ref[table, :, BLK:2 * BLK] if first else bias_ref[table]

        _band_sequence(q_ref, k_ref, v_ref, q_ref.shape[0] // BLK, load_bias, block, prefetch)

    def dilated(q_ref, k_ref, v_ref, table, old, new):
        dil = q_ref.shape[0]
        per_group = min(max(GROUP // (q_ref.shape[1] // BLK), 1), dil)
        assert dil % per_group == 0

        def group(g, carry):
            for rr in range(per_group):
                r = g * per_group + rr
                branch(q_ref.at[r], k_ref.at[r], v_ref.at[r], table, dil, r, old, new)
            return carry

        lax.fori_loop(0, dil // per_group, group, 0)

    dilated(q16_ref, k16_ref, v16_ref, 2, None, state16)
    dilated(q4_ref, k4_ref, v4_ref, 1, state16, state4)
    branch(q1_ref, k1_ref, v1_ref, 0, 1, 0, state4, None)


def _attn_a_call(p_slots, p4, p16, bias_a, batch, seq):
    n_slots = p_slots.shape[0]
    dils = [d for _, d in A_PATTERNS]
    assert dils == [1, p4.shape[2], p16.shape[2]] and seq % (dils[2] * BLK) == 0
    view1 = p_slots.reshape(n_slots, batch, seq, LANES)

    in_specs, args = [], []
    for arr in (p16, p4, view1):
        for slot0 in (0, N_PAIRS, 2 * N_PAIRS):
            in_specs.append(pl.BlockSpec((None, None) + arr.shape[2:],
                                         lambda b, hp, slot0=slot0, nd=arr.ndim: (slot0 + hp, b) + (0,) * (nd - 2)))
            args.append(arr)
    in_specs.append(pl.BlockSpec((len(A_PATTERNS), None, 2 * BLK, 2 * BLK), lambda b, hp: (0, hp, 0, 0)))
    args.append(bias_a)
    blocks = 10 * _nbytes((seq, LANES), BF16) + _nbytes((3, 2 * BLK, 2 * BLK), F32)
    scratch = 6 * _nbytes((seq, LANES), F32)
    return pl.pallas_call(
        functools.partial(_attn_a_kernel, seq=seq),
        grid=(batch, N_PAIRS),
        in_specs=in_specs,
        out_specs=pl.BlockSpec((None, seq, LANES), lambda b, hp: (hp, b, 0)),
        out_shape=jax.ShapeDtypeStruct((N_PAIRS, batch * seq, LANES), BF16),
        scratch_shapes=[pltpu.VMEM((seq, LANES), F32)] * 6,
        compiler_params=pltpu.CompilerParams(
            dimension_semantics=("arbitrary", "arbitrary"),
            vmem_limit_bytes=_vmem_limit(blocks, scratch, 4 * GROUP * _nbytes((2 * BLK, 2 * BLK), F32))),
        name="attn_dilated",
    )(*args)


def _attn_b_kernel(sink_ref, q_ref, k_ref, v_ref, bias_ref, o_ref, *, seq):
    hp = pl.program_id(1)
    lane_lo = lax.broadcasted_iota(jnp.int32, (BLK, LANES), 1) < HEAD_DIM
    row_lo = lax.broadcasted_iota(jnp.int32, (2 * BLK, 1), 0) < BLK
    sink2 = jnp.where(row_lo, sink_ref[2 * hp], sink_ref[2 * hp + 1])

    def block(jb, q, kk, vv, bias, pre):
        s = _pair_scores(q, kk, bias, lane_lo)
        m2 = jnp.maximum(jnp.max(s, axis=-1, keepdims=True), sink2)
        p = jnp.exp(s - m2)
        l2 = jnp.sum(p, axis=-1, keepdims=True) + jnp.exp(sink2 - m2)
        pv = _pair_select(jnp.dot(p.astype(BF16), vv, preferred_element_type=F32), lane_lo)
        r0 = jb * BLK if isinstance(jb, int) else pl.multiple_of(jb * BLK, BLK)
        o_ref[pl.ds(r0, BLK), :] = (pv / _pair_select(l2, lane_lo)).astype(o_ref.dtype)

    def load_bias(first):
        return bias_ref[:, BLK:2 * BLK] if first else bias_ref[...]

    _band_sequence(q_ref, k_ref, v_ref, seq // BLK, load_bias, block)


def _attn_b_call(p_slots, bias_b, sinks, batch, seq):
    n_slots = p_slots.shape[0]
    view = p_slots.reshape(n_slots, batch, seq, LANES)
    q0, k0, v0 = 3 * N_PAIRS, 4 * N_PAIRS, 4 * N_PAIRS + N_KV_B
    pairs_per_kv = N_PAIRS // N_KV_B
    blk = (None, None, seq, LANES)
    blocks = 4 * _nbytes((seq, LANES), BF16) + _nbytes((2 * BLK, 2 * BLK), F32)
    return pl.pallas_call(
        functools.partial(_attn_b_kernel, seq=seq),
        grid=(batch, N_PAIRS),
        in_specs=[pl.BlockSpec(memory_space=pltpu.SMEM),
                  pl.BlockSpec(blk, lambda b, hp: (q0 + hp, b, 0, 0)),
                  pl.BlockSpec(blk, lambda b, hp: (k0 + hp // pairs_per_kv, b, 0, 0)),
                  pl.BlockSpec(blk, lambda b, hp: (v0 + hp // pairs_per_kv, b, 0, 0)),
                  pl.BlockSpec((None, 2 * BLK, 2 * BLK), lambda b, hp: (hp, 0, 0))],
        out_specs=pl.BlockSpec((None, seq, LANES), lambda b, hp: (hp, b, 0)),
        out_shape=jax.ShapeDtypeStruct((N_PAIRS, batch * seq, LANES), BF16),
        compiler_params=pltpu.CompilerParams(
            dimension_semantics=("arbitrary", "arbitrary"),
            vmem_limit_bytes=_vmem_limit(blocks, temp_bytes=8 * _nbytes((2 * BLK, 2 * BLK), F32))),
        name="attn_window",
    )(sinks, view, view, view, bias_b)


def _cached_sample(i, ak_ref, av_ref, q_ref, qt_ref, knt_ref, vn_ref, bk_ref, bv_ref, bias_a_ref, bias_b_ref,
                   relt_ref, sink_ref, o_ref, sa_ref, sb_ref, wa_ref, wb_ref):
    n_branches = bias_a_ref.shape[0]
    heads_per_kv = N_HEADS_B // N_KV_B
    bias0 = relt_ref[:, 0:1]
    sink = sink_ref[...]
    q = q_ref[i] * SCALE
    vn = vn_ref[i]
    s0 = jnp.sum(qt_ref[i] * SCALE * knt_ref[i], axis=-1, keepdims=True) + bias0
    for h in range(N_HEADS_A):
        sa_ref[i, h:h + 1, :] = jnp.sum(ak_ref[h] * q[:, h:h + 1], axis=0, keepdims=True)
    for hb in range(N_HEADS_B):
        h = N_HEADS_A + hb
        sb_ref[i, hb:hb + 1, :] = jnp.sum(bk_ref[i, hb // heads_per_kv] * q[:, h:h + 1], axis=0, keepdims=True)

    s0a = s0[:N_HEADS_A]
    ts = [sa_ref[i] + bias_a_ref[t] for t in range(n_branches)]
    m = s0a
    for t_ in ts:
        m = jnp.maximum(m, jnp.max(t_, axis=-1, keepdims=True))
    w = jnp.exp(ts[0] - m)
    for t_ in ts[1:]:
        w = w + jnp.exp(t_ - m)
    p0a = n_branches * jnp.exp(s0a - m)
    inv = 1.0 / (jnp.sum(w, axis=-1, keepdims=True) + p0a)
    wa_ref[i] = w * inv
    p0a = p0a * inv

    s0b = s0[N_HEADS_A:]
    tb = sb_ref[i] + bias_b_ref[...]
    mb = jnp.maximum(jnp.maximum(s0b, sink), jnp.max(tb, axis=-1, keepdims=True))
    wb = jnp.exp(tb - mb)
    p0b = jnp.exp(s0b - mb)
    invb = 1.0 / (jnp.sum(wb, axis=-1, keepdims=True) + p0b + jnp.exp(sink - mb))
    wb_ref[i] = wb * invb
    p0b = p0b * invb

    cols = []
    for h in range(N_HEADS_A):
        acc = jnp.sum(av_ref[h] * wa_ref[i, h:h + 1, :], axis=-1, keepdims=True)
        cols.append(acc + p0a[h:h + 1, :] * vn[:, h:h + 1])
    for hb in range(N_HEADS_B):
        h = N_HEADS_A + hb
        acc = jnp.sum(bv_ref[i, hb // heads_per_kv] * wb_ref[i, hb:hb + 1, :], axis=-1, keepdims=True)
        cols.append(acc + p0b[hb:hb + 1, :] * vn[:, h:h + 1])
    o_ref[i] = jnp.concatenate(cols, axis=1)


def kernel(x_prompt, x_sample, cache_a_k, cache_a_v, cache_b_k, cache_b_v, c_prompt, c_sample, rel_bias, w_ada, b_ada, ffn1_wg, ffn1_wu, ffn1_wd, w_in, w_out, sinks, ffn2_wg, ffn2_wu, ffn2_wd, ln_g, ln_b):
    batch, seq, d = x_prompt.shape
    n_dec, dec_seq, _ = x_sample.shape
    depth = w_ada.shape[0]
    assert dec_seq == 1 and d == 2 * A_COLS and all(w // dl == BLK for w, dl in A_PATTERNS)
    alpha = (2 * depth) ** 0.25
    fc = 256
    tm = 512
    a_rows, b_rows = min(A_PATTERNS[-1][0], seq), min(B_WINDOW, seq)
    q_cols = 4 * A_COLS
    kvb = N_KV_B * HEAD_DIM
    heads_per_kv = N_HEADS_B // N_KV_B

    band_map, cached_a, cached_b = _bucket_maps(cache_a_k.shape[2], cache_b_k.shape[2])
    band_bias = _band_bias_call(rel_bias, band_map)
    bias_a, bias_b = band_bias[:len(A_PATTERNS)], band_bias[len(A_PATTERNS)]
    bias_sa, bias_sb = _cached_bias_call(rel_bias, cached_a, cached_b)

    pad = (-(n_dec + batch)) % 16
    c_all = jnp.concatenate([c_sample, c_prompt, jnp.zeros((pad, d), F32)], axis=0)
    xp = x_prompt.reshape(batch * seq, d)
    xs = x_sample.reshape(n_dec, d)
    outs = [[] for _ in range(8)]

    def head_rows(x, n_heads):
        return x.reshape(n_dec, n_heads, HEAD_DIM)

    for l in range(depth):
        mod2 = _ada_call(c_all, w_ada[l], b_ada[l])
        ffn1 = [w[l].astype(BF16) for w in (ffn1_wg, ffn1_wu, ffn1_wd)]
        ffn2 = [w[l].astype(BF16) for w in (ffn2_wg, ffn2_wu, ffn2_wd)]
        w_in_s = w_in[l].astype(BF16)
        wo = w_out[l].astype(BF16)

        s1 = _ffn_call(xs, mod2, (0, 1, 2), 0, 1, *ffn1, ln_g[l, 0], ln_b[l, 0], alpha, n_dec, fc)
        proj_s = _proj_sample_call(s1, mod2, w_in_s)
        c = A_COLS
        sak, sav = proj_s[:, c:2 * c], proj_s[:, 2 * c:3 * c]
        sbk, sbv = proj_s[:, q_cols:q_cols + kvb], proj_s[:, q_cols + kvb:q_cols + 2 * kvb]
        qt3 = jnp.concatenate([head_rows(proj_s[:, :c], N_HEADS_A),
                               head_rows(proj_s[:, 3 * c:q_cols], N_HEADS_B)], axis=1)
        knt3 = jnp.concatenate([head_rows(sak, N_HEADS_A),
                                jnp.repeat(head_rows(sbk, N_KV_B), heads_per_kv, axis=1)], axis=1)
        vnt3 = jnp.concatenate([head_rows(sav, N_HEADS_A),
                                jnp.repeat(head_rows(sbv, N_KV_B), heads_per_kv, axis=1)], axis=1)
        to_cols = (0, 2, 3, 1)
        decode_side = (qt3.transpose(0, 2, 1), qt3, knt3, vnt3.transpose(0, 2, 1),
                       cache_b_k[l].transpose(to_cols), cache_b_v[l].transpose(to_cols),
                       bias_sa, bias_sb, rel_bias.T, sinks[l].reshape(N_HEADS_B, 1),
                       cache_a_k[l].transpose(to_cols), cache_a_v[l].transpose(to_cols))

        n_first = n_dec // 2 if (n_dec // 2) % (2 * (batch * seq // tm)) == 0 else n_dec
        x1, mix3 = _ffn_call(xp, mod2, (0, 1, 2), n_dec, seq, *ffn1, ln_g[l, 0], ln_b[l, 0], alpha, tm, fc,
                             side=(decode_side, 0, n_first))
        p_slots, p4, p16, pak, pav, pbk, pbv = _proj_prompt_call(x1, mod2, n_dec, seq, w_in_s, a_rows, b_rows, tm)
        mix_a = _attn_a_call(p_slots, p4, p16, bias_a, batch, seq)
        mix_b = _attn_b_call(p_slots, bias_b, sinks[l], batch, seq)
        xp = _ffn_call(x1, mod2, (6, 7, 8), n_dec, seq, *ffn2, ln_g[l, 2], ln_b[l, 2], alpha, tm, fc,
                       mix=(mix_a, mix_b), w_out=wo, gate_k=5, ln1_g=ln_g[l, 1], ln1_b=ln_b[l, 1],
                       side=(decode_side, n_first, n_dec - n_first) if n_first < n_dec else None)
        if n_first < n_dec:
            xp, mix3_rest = xp
            mix3 = jnp.concatenate([mix3, mix3_rest], axis=0)

        mix_s = mix3.transpose(0, 2, 1).reshape(n_dec, d)
        xs = _ffn_call(s1, mod2, (6, 7, 8), 0, 1, *ffn2, ln_g[l, 2], ln_b[l, 2], alpha, n_dec, fc,
                       mix=mix_s, w_out=wo, gate_k=5, ln1_g=ln_g[l, 1], ln1_b=ln_b[l, 1])

        new = [pak.reshape(batch, a_rows, N_HEADS_A, HEAD_DIM), pav.reshape(batch, a_rows, N_HEADS_A, HEAD_DIM),
               pbk.reshape(batch, b_rows, N_KV_B, HEAD_DIM), pbv.reshape(batch, b_rows, N_KV_B, HEAD_DIM),
               sak.reshape(n_dec, 1, N_HEADS_A, HEAD_DIM), sav.reshape(n_dec, 1, N_HEADS_A, HEAD_DIM),
               sbk.reshape(n_dec, 1, N_KV_B, HEAD_DIM), sbv.reshape(n_dec, 1, N_KV_B, HEAD_DIM)]
        for acc, arr in zip(outs, new):
            acc.append(arr)

    return (xp.reshape(batch, seq, d), xs.reshape(n_dec, 1, d)) + tuple(jnp.stack(o) for o in outs)
```

```python
import functools
import math

import jax
import jax.numpy as jnp
import numpy as np
from jax import lax
from jax.experimental import pallas as pl
from jax.experimental.pallas import tpu as pltpu

HEAD_DIM = 64
N_HEADS_A = 8
N_HEADS_B = 8
N_KV_B = 2
A_PATTERNS = ((128, 1), (512, 4), (2048, 16))
B_WINDOW = 128
BLK = 128
NUM_BUCKETS = 32
MAX_DISTANCE = 2048
LN_EPS = 1e-5
NEG = -1e30
SCALE = HEAD_DIM ** -0.5

LANES = 128
SUBLANES = 8
V7X_VMEM_BYTES = 64 * 1024 * 1024
A_COLS = N_HEADS_A * HEAD_DIM
N_PAIRS = A_COLS // LANES

F32 = jnp.float32
BF16 = jnp.bfloat16


def _vmem_limit(pipelined_bytes, scratch_bytes=0, temp_bytes=0):
    want = 2 * pipelined_bytes + scratch_bytes + temp_bytes + (4 << 20)
    return int(min(want, V7X_VMEM_BYTES - (6 << 20)))


def _resident_spec(shape):
    return pl.BlockSpec(shape, lambda *_: (0,) * len(shape), pipeline_mode=pl.Buffered(1))


def _nbytes(shape, dtype):
    return math.prod(shape) * jnp.dtype(dtype).itemsize


def _layer_norm(x, g, b):
    mu = jnp.mean(x, axis=-1, keepdims=True)
    xc = x - mu
    var = jnp.mean(xc * xc, axis=-1, keepdims=True)
    return xc * lax.rsqrt(var + LN_EPS) * g + b


def _silu(x):
    return x * jax.nn.sigmoid(x)


def _seq_mod_spec(seq0, tiles_per_seq, k, d):
    return pl.BlockSpec((SUBLANES, d), lambda i: ((seq0 + i // tiles_per_seq) // SUBLANES, k))


def _seq_mod(ref, mod_row):
    if mod_row is None:
        return ref[...]
    seq0, tiles_per_seq = mod_row
    return ref[pl.ds((seq0 + pl.program_id(0) // tiles_per_seq) % SUBLANES, 1), :]


def _ada_kernel(c_ref, w_ref, b_ref, o_ref):
    a = _silu(c_ref[...]).astype(BF16)
    o_ref[...] = jnp.dot(a, w_ref[...].astype(BF16), preferred_element_type=F32) + b_ref[...]


def _ada_call(c_all, w_ada, b_ada):
    m, d = c_all.shape
    n = w_ada.shape[1]
    tn = n // 8
    blocks = _nbytes((m, d), F32) + _nbytes((d, tn), F32) + _nbytes((m, tn), F32)
    return pl.pallas_call(
        _ada_kernel,
        grid=(n // tn,),
        in_specs=[pl.BlockSpec((m, d), lambda j: (0, 0)),
                  pl.BlockSpec((d, tn), lambda j: (0, j)),
                  pl.BlockSpec((1, tn), lambda j: (0, j))],
        out_specs=pl.BlockSpec((m, tn), lambda j: (0, j)),
        out_shape=jax.ShapeDtypeStruct((m, n), F32),
        compiler_params=pltpu.CompilerParams(
            dimension_semantics=("arbitrary",),
            vmem_limit_bytes=_vmem_limit(blocks, temp_bytes=_nbytes((d, tn), BF16))),
        name="ada_mod",
    )(c_all, w_ada, b_ada.reshape(1, n))


def _t5_bucket(n):
    max_exact = NUM_BUCKETS // 2
    nf = np.maximum(n, 1).astype(np.float32)
    large = max_exact + (np.log(nf / max_exact) / math.log(MAX_DISTANCE / max_exact)
                         * (NUM_BUCKETS - max_exact)).astype(np.int32)
    return np.where(n < max_exact, n, np.minimum(large, NUM_BUCKETS - 1))


def _bucket_maps(a_rows, b_rows):
    steps = [d for _, d in A_PATTERNS] + [1]
    q = np.arange(BLK)[:, None]
    k = np.arange(2 * BLK)[None, :]
    dist = q + BLK - k
    valid = (dist >= 0) & (dist <= BLK)
    band = np.stack([np.where(valid, _t5_bucket(np.maximum(dist, 0) * s), -1) for s in steps])

    def cached(rows, step):
        back = rows - np.arange(rows)
        hit = (back % step == 0) & (back // step <= BLK)
        return np.where(hit, _t5_bucket(back), -1)[None, :].astype(np.int32)

    cached_a = np.stack([cached(a_rows, s) for s in steps[:-1]])
    return band.astype(np.int32), cached_a, cached(b_rows, 1)


def _band_bias_kernel(rel_ref, bmap_ref, o_ref):
    head0 = jnp.where(pl.program_id(0) == len(A_PATTERNS), N_HEADS_A, 0)
    rows = 2 * SUBLANES
    for r0 in range(0, BLK, rows):
        bm = bmap_ref[r0:r0 + rows, :]
        accs = [jnp.full(bm.shape, NEG, F32)] * (2 * N_PAIRS)
        for b in range(NUM_BUCKETS):
            hit = bm == b
            accs = [jnp.where(hit, rel_ref[b, head0 + h], acc) for h, acc in enumerate(accs)]
        for h, acc in enumerate(accs):
            o_ref[h // 2, (h % 2) * BLK + r0:(h % 2) * BLK + r0 + rows, :] = acc


def _band_bias_call(rel_bias, band_map):
    n_tab = band_map.shape[0]
    return pl.pallas_call(
        _band_bias_kernel,
        grid=(n_tab,),
        in_specs=[pl.BlockSpec(memory_space=pltpu.SMEM),
                  pl.BlockSpec((None, BLK, 2 * BLK), lambda t: (t, 0, 0))],
        out_specs=pl.BlockSpec((None, N_PAIRS, 2 * BLK, 2 * BLK), lambda t: (t, 0, 0, 0)),
        out_shape=jax.ShapeDtypeStruct((n_tab, N_PAIRS, 2 * BLK, 2 * BLK), F32),
        compiler_params=pltpu.CompilerParams(dimension_semantics=("arbitrary",)),
        name="band_bias",
    )(rel_bias, band_map)


def _cached_bias_kernel(relt_ref, amap_ref, bmap_ref, oa_ref, ob_ref):
    relt = relt_ref[...]

    def table(bm, heads):
        acc = jnp.full(bm.shape, NEG, F32)
        for b in range(NUM_BUCKETS):
            acc = jnp.where(bm == b, heads[:, b:b + 1], acc)
        return acc

    for t in range(amap_ref.shape[0]):
        oa_ref[t] = table(jnp.broadcast_to(amap_ref[t], oa_ref.shape[1:]), relt[:N_HEADS_A])
    ob_ref[...] = table(jnp.broadcast_to(bmap_ref[...], ob_ref.shape), relt[N_HEADS_A:])


def _cached_bias_call(rel_bias, cached_a, cached_b):
    return pl.pallas_call(
        _cached_bias_kernel,
        out_shape=(jax.ShapeDtypeStruct((cached_a.shape[0], N_HEADS_A, cached_a.shape[2]), F32),
                   jax.ShapeDtypeStruct((N_HEADS_B, cached_b.shape[1]), F32)),
        name="cached_bias",
    )(rel_bias.T, cached_a, cached_b)


def _ffn_kernel(*refs, alpha, pre_mix, mix_slots, n_chunks, fc, n_side, sample0, mod_row):
    refs = list(refs)
    x_ref = refs.pop(0)
    if pre_mix:
        mix_refs = [refs.pop(0) for _ in range(2 if mix_slots else 1)]
        wo_ref, gm_ref, ln1g_ref, ln1b_ref = (refs.pop(0) for _ in range(4))
    sh_ref, sc_ref, gt_ref, wg_ref, wu_ref, wd_ref, lng_ref, lnb_ref = (refs.pop(0) for _ in range(8))
    if n_side:
        side_in = [refs.pop(0) for _ in range(10)]
        ak_hbm, av_hbm, o_ref, oc_ref, h_ref, acc_ref, kbuf, vbuf, sem = (refs.pop(0) for _ in range(9))
        side_scratch = refs
        end = sample0 + pl.num_programs(0) * n_side
        first = sample0 + pl.program_id(0) * n_side

        def cache_copies(g, slot):
            return (pltpu.make_async_copy(ak_hbm.at[g], kbuf.at[slot], sem.at[0, slot]),
                    pltpu.make_async_copy(av_hbm.at[g], vbuf.at[slot], sem.at[1, slot]))

        @pl.when(pl.program_id(0) == 0)
        def _():
            for cp in cache_copies(sample0, 0):
                cp.start()

        def side(j):
            g, slot = first + j, j % 2
            for cp in cache_copies(g, slot):
                cp.wait()

            @pl.when(g + 1 < end)
            def _():
                for cp in cache_copies(g + 1, 1 - slot):
                    cp.start()

            _cached_sample(j, kbuf.at[slot], vbuf.at[slot], *side_in, oc_ref, *side_scratch)

        assert n_side % 2 == 0
        side_at = {(j * n_chunks) // n_side if j else -1: j for j in range(n_side)}
    else:
        o_ref, h_ref, acc_ref = refs
        side_at = {}

    if -1 in side_at:
        side(side_at[-1])
    x = x_ref[...]
    if pre_mix:
        if mix_slots:
            mixed = jnp.concatenate([r[j] for r in mix_refs for j in range(N_PAIRS)], axis=-1)
        else:
            mixed = mix_refs[0][...].astype(BF16)
        y = jnp.dot(mixed, wo_ref[...], preferred_element_type=F32)
        x = _layer_norm(alpha * x + _seq_mod(gm_ref, mod_row) * y, ln1g_ref[...], ln1b_ref[...])
    h_ref[...] = (x * (1.0 + _seq_mod(sc_ref, mod_row)) + _seq_mod(sh_ref, mod_row)).astype(BF16)
    for c in range(n_chunks):
        if c in side_at:
            side(side_at[c])
        cols = slice(c * fc, (c + 1) * fc)
        g = jnp.dot(h_ref[...], wg_ref[:, cols], preferred_element_type=F32)
        u = jnp.dot(h_ref[...], wu_ref[:, cols], preferred_element_type=F32)
        y = jnp.dot((_silu(g) * u).astype(BF16), wd_ref[cols, :], preferred_element_type=F32)
        if c == 0:
            acc_ref[...] = y
        else:
            acc_ref[...] += y
    o_ref[...] = _layer_norm(alpha * x + 0.5 * _seq_mod(gt_ref, mod_row) * acc_ref[...], lng_ref[...], lnb_ref[...])


def _ffn_call(x, mod2, mod_ks, seq0, rows_per_seq, wg, wu, wd, ln_g, ln_b, alpha, tm, fc,
              mix=None, w_out=None, gate_k=None, ln1_g=None, ln1_b=None, side=None):
    n, d = x.shape
    d_ff = wg.shape[1]
    n_chunks = d_ff // fc
    assert n_chunks * fc == d_ff
    per_row = rows_per_seq == 1
    tiles_per_seq = max(rows_per_seq // tm, 1)

    def mod_spec(k):
        if per_row:
            return pl.BlockSpec((tm, d), lambda i, k=k: (seq0 // tm + i, k))
        return _seq_mod_spec(seq0, tiles_per_seq, k, d)

    mod_arr = mod2
    row_spec = pl.BlockSpec((tm, d), lambda i: (i, 0))
    vec_spec = pl.BlockSpec((1, d), lambda i: (0, 0))
    args, specs = [x], [row_spec]
    pre_mix = mix is not None
    mix_slots = pre_mix and isinstance(mix, tuple)
    blocks = 2 * _nbytes((tm, d), F32) + (3 * _nbytes((tm, d), F32) if per_row else 0)
    resident = 3 * _nbytes(wg.shape, BF16)
    if pre_mix:
        if mix_slots:
            for mm in mix:
                args.append(mm)
                specs.append(pl.BlockSpec((N_PAIRS, tm, LANES), lambda i: (0, i, 0)))
        else:
            args.append(mix)
            specs.append(row_spec)
        args += [w_out, mod_arr, ln1_g.reshape(1, d), ln1_b.reshape(1, d)]
        specs += [_resident_spec((d, d)), mod_spec(gate_k), vec_spec, vec_spec]
        blocks += _nbytes((tm, d), F32)
        resident += _nbytes((d, d), BF16)
    args += [mod_arr, mod_arr, mod_arr, wg, wu, wd, ln_g.reshape(1, d), ln_b.reshape(1, d)]
    specs += [mod_spec(mod_ks[0]), mod_spec(mod_ks[1]), mod_spec(mod_ks[2]),
              _resident_spec(wg.shape), _resident_spec(wu.shape), _resident_spec(wd.shape), vec_spec, vec_spec]
    scratch = _nbytes((tm, d), BF16) + _nbytes((tm, d), F32) + resident
    temps = 6 * _nbytes((tm, fc), F32) + 3 * _nbytes((tm, d), F32)
    out_specs, out_shape = row_spec, jax.ShapeDtypeStruct((n, d), F32)
    scratch_shapes = [pltpu.VMEM((tm, d), BF16), pltpu.VMEM((tm, d), F32)]
    n_side = sample0 = 0
    if side is not None:
        (*small, ak, av), sample0, count = side
        n_steps = n // tm
        n_side = count // n_steps
        assert n_side * n_steps == count and sample0 % n_side == 0
        blk0 = sample0 // n_side
        for a in small[:6]:
            args.append(a)
            specs.append(pl.BlockSpec((n_side,) + a.shape[1:],
                                      lambda i, nd=a.ndim: (blk0 + i,) + (0,) * (nd - 1)))
            blocks += n_side * _nbytes(a.shape[1:-1] + (max(a.shape[-1], LANES),), F32)
        for a in small[6:]:
            args.append(a)
            specs.append(pl.BlockSpec(a.shape, lambda i, nd=a.ndim: (0,) * nd))
            blocks += _nbytes(a.shape, F32)
        args += [ak, av]
        specs += [pl.BlockSpec(memory_space=pl.ANY)] * 2
        q3 = small[0]
        out_specs = [row_spec, pl.BlockSpec((n_side,) + q3.shape[1:], lambda i: (i, 0, 0))]
        out_shape = [out_shape, jax.ShapeDtypeStruct((count,) + q3.shape[1:], F32)]
        sa_shape = (n_side, N_HEADS_A, ak.shape[3])
        sb_shape = (n_side, N_HEADS_B, small[4].shape[3])
        scratch_shapes += [pltpu.VMEM((2,) + ak.shape[1:], F32), pltpu.VMEM((2,) + av.shape[1:], F32),
                           pltpu.SemaphoreType.DMA((2, 2)),
                           pltpu.VMEM(sa_shape, F32), pltpu.VMEM(sb_shape, F32),
                           pltpu.VMEM(sa_shape, F32), pltpu.VMEM(sb_shape, F32)]
        scratch += 4 * _nbytes(ak.shape[1:], F32) + 2 * _nbytes(sa_shape, F32) + 2 * _nbytes(sb_shape, F32)
    return pl.pallas_call(
        functools.partial(_ffn_kernel, alpha=alpha, pre_mix=pre_mix, mix_slots=mix_slots,
                          n_chunks=n_chunks, fc=fc, n_side=n_side, sample0=sample0,
                          mod_row=None if per_row else (seq0, tiles_per_seq)),
        grid=(n // tm,),
        in_specs=specs,
        out_specs=out_specs,
        out_shape=out_shape,
        scratch_shapes=scratch_shapes,
        compiler_params=pltpu.CompilerParams(
            dimension_semantics=("arbitrary",),
            vmem_limit_bytes=_vmem_limit(blocks, scratch, temps)),
        name=("ffn_mix" if pre_mix else "ffn") + ("_cached" if n_side else ""),
    )(*args)


def _proj_prompt_kernel(x_ref, sh_ref, sc_ref, w_ref, p_ref, p4_ref, p16_ref, ak_ref, av_ref, bk_ref, bv_ref,
                        stage_ref, stage4_ref, *, tm, b_rows, mod_row):
    h = (x_ref[...] * (1.0 + _seq_mod(sc_ref, mod_row)) + _seq_mod(sh_ref, mod_row)).astype(BF16)
    lane_lo = lax.broadcasted_iota(jnp.int32, (tm, LANES), 1) < HEAD_DIM
    for g in range(4):
        res = jnp.dot(h, w_ref[:, g * A_COLS:(g + 1) * A_COLS], preferred_element_type=F32)
        scaled = res * SCALE if g in (0, 3) else res
        for s in range(N_PAIRS):
            slab = scaled[:, s * LANES:(s + 1) * LANES]
            p_ref[g * N_PAIRS + s] = slab.astype(BF16)
            if g < 3:
                slot = g * N_PAIRS + s
                d4, d16 = p4_ref.shape[1], p16_ref.shape[1]
                stage_ref[slot] = slab
                for r in range(d4):
                    rows = stage_ref[slot, pl.ds(r, tm // d4, stride=d4), :]
                    p4_ref[slot, r] = rows.astype(BF16)
                    stage4_ref[slot, r] = rows
                for r in range(d16):
                    rows = stage4_ref[slot, r % d4, pl.ds(r // d4, tm // d16, stride=d16 // d4), :]
                    p16_ref[slot, r] = rows.astype(BF16)
        if g in (1, 2):
            out_ref = ak_ref if g == 1 else av_ref
            out_ref[...] = res
    res = jnp.dot(h, w_ref[:, 4 * A_COLS:], preferred_element_type=F32)
    for j, out_ref in enumerate((bk_ref, bv_ref)):
        slab = res[:, j * LANES:(j + 1) * LANES]
        swapped = pltpu.roll(slab, HEAD_DIM, axis=1)
        p_ref[4 * N_PAIRS + N_KV_B * j] = jnp.where(lane_lo, slab, swapped).astype(BF16)
        p_ref[4 * N_PAIRS + N_KV_B * j + 1] = jnp.where(lane_lo, swapped, slab).astype(BF16)

        out_ref[...] = slab[tm - b_rows:, :]


def _proj_prompt_call(x, mod2, seq0, seq, w_ext, a_rows, b_rows, tm):
    n, d = x.shape
    batch = n // seq
    tiles_per_seq = seq // tm
    a_tiles = a_rows // tm
    n_slots = 4 * N_PAIRS + 2 * N_KV_B
    assert w_ext.shape[1] == 4 * A_COLS + 2 * LANES and N_KV_B * HEAD_DIM == LANES

    def mod_spec(k):
        return _seq_mod_spec(seq0, tiles_per_seq, k, d)

    def a_map(i):
        return (i // tiles_per_seq, jnp.maximum(i % tiles_per_seq - (tiles_per_seq - a_tiles), 0), 0)

    a_blk = (None, tm, A_COLS)
    b_blk = (None, b_rows, LANES)
    a_slots = 3 * N_PAIRS
    d4, d16 = A_PATTERNS[1][1], A_PATTERNS[2][1]
    assert tm % (16 * d16) == 0

    def dil_spec(dil):
        return pl.BlockSpec((a_slots, None, dil, tm // dil, LANES),
                            lambda i: (0, i // tiles_per_seq, 0, i % tiles_per_seq, 0))

    blocks = (_nbytes((tm, d), F32) + _nbytes(w_ext.shape, BF16) + _nbytes((n_slots + 2 * a_slots, tm, LANES), BF16)
              + 2 * _nbytes((tm, A_COLS), F32) + 2 * _nbytes((b_rows, LANES), F32))
    return pl.pallas_call(
        functools.partial(_proj_prompt_kernel, tm=tm, b_rows=b_rows, mod_row=(seq0, tiles_per_seq)),
        grid=(n // tm,),
        in_specs=[pl.BlockSpec((tm, d), lambda i: (i, 0)), mod_spec(3), mod_spec(4),
                  pl.BlockSpec(w_ext.shape, lambda i: (0, 0))],
        out_specs=[pl.BlockSpec((n_slots, tm, LANES), lambda i: (0, i, 0)),
                   dil_spec(d4), dil_spec(d16),
                   pl.BlockSpec(a_blk, a_map),
                   pl.BlockSpec(a_blk, a_map),
                   pl.BlockSpec(b_blk, lambda i: (i // tiles_per_seq, 0, 0)),
                   pl.BlockSpec(b_blk, lambda i: (i // tiles_per_seq, 0, 0))],
        out_shape=[jax.ShapeDtypeStruct((n_slots, n, LANES), BF16),
                   jax.ShapeDtypeStruct((a_slots, batch, d4, seq // d4, LANES), BF16),
                   jax.ShapeDtypeStruct((a_slots, batch, d16, seq // d16, LANES), BF16),
                   jax.ShapeDtypeStruct((batch, a_rows, A_COLS), F32),
                   jax.ShapeDtypeStruct((batch, a_rows, A_COLS), F32),
                   jax.ShapeDtypeStruct((batch, b_rows, LANES), F32),
                   jax.ShapeDtypeStruct((batch, b_rows, LANES), F32)],
        scratch_shapes=[pltpu.VMEM((a_slots, tm, LANES), F32), pltpu.VMEM((a_slots, d4, tm // d4, LANES), F32)],
        compiler_params=pltpu.CompilerParams(
            dimension_semantics=("arbitrary",),
            vmem_limit_bytes=_vmem_limit(blocks, 2 * _nbytes((a_slots, tm, LANES), F32),
                                         4 * _nbytes((tm, A_COLS), F32))),
        name="proj_prompt",
    )(x, mod2, mod2, w_ext)


def _proj_sample_kernel(x_ref, sh_ref, sc_ref, w_ref, o_ref):
    h = (x_ref[...] * (1.0 + sc_ref[...]) + sh_ref[...]).astype(BF16)
    o_ref[...] = jnp.dot(h, w_ref[...], preferred_element_type=F32)


def _proj_sample_call(x, mod2, w_ext):
    n, d = x.shape
    cols = w_ext.shape[1]
    blocks = 3 * _nbytes((n, d), F32) + _nbytes(w_ext.shape, BF16) + _nbytes((n, cols), F32)
    return pl.pallas_call(
        _proj_sample_kernel,
        grid=(1,),
        in_specs=[pl.BlockSpec((n, d), lambda i: (0, 0)),
                  pl.BlockSpec((n, d), lambda i: (0, 3)),
                  pl.BlockSpec((n, d), lambda i: (0, 4)),
                  pl.BlockSpec(w_ext.shape, lambda i: (0, 0))],
        out_specs=pl.BlockSpec((n, cols), lambda i: (0, 0)),
        out_shape=jax.ShapeDtypeStruct((n, cols), F32),
        compiler_params=pltpu.CompilerParams(
            dimension_semantics=("arbitrary",), vmem_limit_bytes=_vmem_limit(blocks)),
        name="proj_sample",
    )(x, mod2, mod2, w_ext)


def _pair_scores(q, kk, bias, lane_lo):
    zero = jnp.zeros_like(q)
    q2 = jnp.concatenate([jnp.where(lane_lo, q, zero), jnp.where(lane_lo, zero, q)], axis=0)
    s = lax.dot_general(q2, kk, (((1,), (1,)), ((), ())), preferred_element_type=F32)
    return s + bias


def _pair_select(a, lane_lo):
    return jnp.where(lane_lo, a[:BLK], a[BLK:])


GROUP = 16


def _band_sequence(q_ref, k_ref, v_ref, n_blocks, load_bias, block):
    def one(jb, first):
        if first:
            block(jb, q_ref[0:BLK, :], k_ref[0:BLK, :], v_ref[0:BLK, :], load_bias(True))
        else:
            r0 = jb * BLK
            if not isinstance(jb, int):
                r0 = pl.multiple_of(r0, BLK)
            block(jb, q_ref[pl.ds(r0, BLK), :], k_ref[pl.ds(r0 - BLK, 2 * BLK), :],
                  v_ref[pl.ds(r0 - BLK, 2 * BLK), :], load_bias(False))

    for jb in range(min(GROUP, n_blocks)):
        one(jb, jb == 0)
    if n_blocks > GROUP:
        assert n_blocks % GROUP == 0

        def group(g, carry):
            for j in range(GROUP):
                one(g * GROUP + j, False)
            return carry

        lax.fori_loop(1, n_blocks // GROUP, group, 0)


def _attn_a_kernel(q16_ref, k16_ref, v16_ref, q4_ref, k4_ref, v4_ref, q1_ref, k1_ref, v1_ref,
                   bias_ref, o_ref, acc16_ref, m16_ref, den16_ref, acc4_ref, m4_ref, den4_ref, *, seq):
    lane_lo = lax.broadcasted_iota(jnp.int32, (BLK, LANES), 1) < HEAD_DIM
    state16 = (acc16_ref, m16_ref, den16_ref)
    state4 = (acc4_ref, m4_ref, den4_ref)

    def update(q, kk, vv, bias, rows, old, new):
        s = _pair_scores(q, kk, bias, lane_lo)
        part = s[:, :LANES] if s.shape[1] == LANES else jnp.maximum(s[:, :LANES], s[:, LANES:])
        if old is not None:
            acc_old, m_old, den_old = (ref[rows, :] for ref in old)
            part = jnp.maximum(part, jnp.concatenate([jnp.where(lane_lo, m_old, NEG),
                                                      jnp.where(lane_lo, NEG, m_old)], axis=0))
        m2 = jnp.max(part, axis=-1, keepdims=True)
        p = jnp.exp(s - m2)
        l2 = jnp.sum(p, axis=-1, keepdims=True)
        pv = _pair_select(jnp.dot(p.astype(BF16), vv, preferred_element_type=F32), lane_lo)
        m_new = _pair_select(m2, lane_lo)
        den = _pair_select(l2, lane_lo)
        if old is None:
            acc = pv
        else:
            a = jnp.exp(m_old - m_new)
            acc = a * acc_old + pv
            den = a * den_old + den
        if new is None:
            o_ref[rows, :] = (acc / den).astype(o_ref.dtype)
        else:
            new[0][rows, :] = acc
            new[1][rows, :] = m_new
            new[2][rows, :] = den

    def branch(q_ref, k_ref, v_ref, table, dil, r, old, new):
        def block(jb, q, kk, vv, bias):
            start = r + jb * (BLK * dil)
            rows = pl.ds(start, BLK, stride=dil) if dil > 1 else pl.ds(start, BLK)
            update(q, kk, vv, bias, rows, old, new)

        def load_bias(first):
            return bias_ref[table, :, BLK:2 * BLK] if first else bias_ref[table]

        _band_sequence(q_ref, k_ref, v_ref, q_ref.shape[0] // BLK, load_bias, block)

    def dilated(q_ref, k_ref, v_ref, table, old, new):
        dil = q_ref.shape[0]
        per_group = min(max(GROUP // (q_ref.shape[1] // BLK), 1), dil)
        assert dil % per_group == 0

        def group(g, carry):
            for rr in range(per_group):
                r = g * per_group + rr
                branch(q_ref.at[r], k_ref.at[r], v_ref.at[r], table, dil, r, old, new)
            return carry

        lax.fori_loop(0, dil // per_group, group, 0)

    dilated(q16_ref, k16_ref, v16_ref, 2, None, state16)
    dilated(q4_ref, k4_ref, v4_ref, 1, state16, state4)
    branch(q1_ref, k1_ref, v1_ref, 0, 1, 0, state4, None)


def _attn_a_call(p_slots, p4, p16, bias_a, batch, seq):
    n_slots = p_slots.shape[0]
    dils = [d for _, d in A_PATTERNS]
    assert dils == [1, p4.shape[2], p16.shape[2]] and seq % (dils[2] * BLK) == 0
    view1 = p_slots.reshape(n_slots, batch, seq, LANES)

    in_specs, args = [], []
    for arr in (p16, p4, view1):
        for slot0 in (0, N_PAIRS, 2 * N_PAIRS):
            in_specs.append(pl.BlockSpec((None, None) + arr.shape[2:],
                                         lambda b, hp, slot0=slot0, nd=arr.ndim: (slot0 + hp, b) + (0,) * (nd - 2)))
            args.append(arr)
    in_specs.append(pl.BlockSpec((len(A_PATTERNS), None, 2 * BLK, 2 * BLK), lambda b, hp: (0, hp, 0, 0)))
    args.append(bias_a)
    blocks = 10 * _nbytes((seq, LANES), BF16) + _nbytes((3, 2 * BLK, 2 * BLK), F32)
    scratch = 6 * _nbytes((seq, LANES), F32)
    return pl.pallas_call(
        functools.partial(_attn_a_kernel, seq=seq),
        grid=(batch, N_PAIRS),
        in_specs=in_specs,
        out_specs=pl.BlockSpec((None, seq, LANES), lambda b, hp: (hp, b, 0)),
        out_shape=jax.ShapeDtypeStruct((N_PAIRS, batch * seq, LANES), BF16),
        scratch_shapes=[pltpu.VMEM((seq, LANES), F32)] * 6,
        compiler_params=pltpu.CompilerParams(
            dimension_semantics=("arbitrary", "arbitrary"),
            vmem_limit_bytes=_vmem_limit(blocks, scratch, 4 * GROUP * _nbytes((2 * BLK, 2 * BLK), F32))),
        name="attn_dilated",
    )(*args)


def _attn_b_kernel(sink_ref, q_ref, k_ref, v_ref, bias_ref, o_ref, *, seq):
    hp = pl.program_id(1)
    lane_lo = lax.broadcasted_iota(jnp.int32, (BLK, LANES), 1) < HEAD_DIM
    row_lo = lax.broadcasted_iota(jnp.int32, (2 * BLK, 1), 0) < BLK
    sink2 = jnp.where(row_lo, sink_ref[2 * hp], sink_ref[2 * hp + 1])

    def block(jb, q, kk, vv, bias):
        s = _pair_scores(q, kk, bias, lane_lo)
        m2 = jnp.maximum(jnp.max(s, axis=-1, keepdims=True), sink2)
        p = jnp.exp(s - m2)
        l2 = jnp.sum(p, axis=-1, keepdims=True) + jnp.exp(sink2 - m2)
        pv = _pair_select(jnp.dot(p.astype(BF16), vv, preferred_element_type=F32), lane_lo)
        r0 = jb * BLK if isinstance(jb, int) else pl.multiple_of(jb * BLK, BLK)
        o_ref[pl.ds(r0, BLK), :] = (pv / _pair_select(l2, lane_lo)).astype(o_ref.dtype)

    def load_bias(first):
        return bias_ref[:, BLK:2 * BLK] if first else bias_ref[...]

    _band_sequence(q_ref, k_ref, v_ref, seq // BLK, load_bias, block)


def _attn_b_call(p_slots, bias_b, sinks, batch, seq):
    n_slots = p_slots.shape[0]
    view = p_slots.reshape(n_slots, batch, seq, LANES)
    q0, k0, v0 = 3 * N_PAIRS, 4 * N_PAIRS, 4 * N_PAIRS + N_KV_B
    pairs_per_kv = N_PAIRS // N_KV_B
    blk = (None, None, seq, LANES)
    blocks = 4 * _nbytes((seq, LANES), BF16) + _nbytes((2 * BLK, 2 * BLK), F32)
    return pl.pallas_call(
        functools.partial(_attn_b_kernel, seq=seq),
        grid=(batch, N_PAIRS),
        in_specs=[pl.BlockSpec(memory_space=pltpu.SMEM),
                  pl.BlockSpec(blk, lambda b, hp: (q0 + hp, b, 0, 0)),
                  pl.BlockSpec(blk, lambda b, hp: (k0 + hp // pairs_per_kv, b, 0, 0)),
                  pl.BlockSpec(blk, lambda b, hp: (v0 + hp // pairs_per_kv, b, 0, 0)),
                  pl.BlockSpec((None, 2 * BLK, 2 * BLK), lambda b, hp: (hp, 0, 0))],
        out_specs=pl.BlockSpec((None, seq, LANES), lambda b, hp: (hp, b, 0)),
        out_shape=jax.ShapeDtypeStruct((N_PAIRS, batch * seq, LANES), BF16),
        compiler_params=pltpu.CompilerParams(
            dimension_semantics=("arbitrary", "arbitrary"),
            vmem_limit_bytes=_vmem_limit(blocks, temp_bytes=8 * _nbytes((2 * BLK, 2 * BLK), F32))),
        name="attn_window",
    )(sinks, view, view, view, bias_b)


def _cached_sample(i, ak_ref, av_ref, q_ref, qt_ref, knt_ref, vn_ref, bk_ref, bv_ref, bias_a_ref, bias_b_ref,
                   relt_ref, sink_ref, o_ref, sa_ref, sb_ref, wa_ref, wb_ref):
    n_branches = bias_a_ref.shape[0]
    heads_per_kv = N_HEADS_B // N_KV_B
    bias0 = relt_ref[:, 0:1]
    sink = sink_ref[...]
    q = q_ref[i] * SCALE
    vn = vn_ref[i]
    s0 = jnp.sum(qt_ref[i] * SCALE * knt_ref[i], axis=-1, keepdims=True) + bias0
    for h in range(N_HEADS_A):
        sa_ref[i, h:h + 1, :] = jnp.sum(ak_ref[h] * q[:, h:h + 1], axis=0, keepdims=True)
    for hb in range(N_HEADS_B):
        h = N_HEADS_A + hb
        sb_ref[i, hb:hb + 1, :] = jnp.sum(bk_ref[i, hb // heads_per_kv] * q[:, h:h + 1], axis=0, keepdims=True)

    s0a = s0[:N_HEADS_A]
    ts = [sa_ref[i] + bias_a_ref[t] for t in range(n_branches)]
    m = s0a
    for t_ in ts:
        m = jnp.maximum(m, jnp.max(t_, axis=-1, keepdims=True))
    w = jnp.exp(ts[0] - m)
    for t_ in ts[1:]:
        w = w + jnp.exp(t_ - m)
    p0a = n_branches * jnp.exp(s0a - m)
    inv = 1.0 / (jnp.sum(w, axis=-1, keepdims=True) + p0a)
    wa_ref[i] = w * inv
    p0a = p0a * inv

    s0b = s0[N_HEADS_A:]
    tb = sb_ref[i] + bias_b_ref[...]
    mb = jnp.maximum(jnp.maximum(s0b, sink), jnp.max(tb, axis=-1, keepdims=True))
    wb = jnp.exp(tb - mb)
    p0b = jnp.exp(s0b - mb)
    invb = 1.0 / (jnp.sum(wb, axis=-1, keepdims=True) + p0b + jnp.exp(sink - mb))
    wb_ref[i] = wb * invb
    p0b = p0b * invb

    cols = []
    for h in range(N_HEADS_A):
        acc = jnp.sum(av_ref[h] * wa_ref[i, h:h + 1, :], axis=-1, keepdims=True)
        cols.append(acc + p0a[h:h + 1, :] * vn[:, h:h + 1])
    for hb in range(N_HEADS_B):
        h = N_HEADS_A + hb
        acc = jnp.sum(bv_ref[i, hb // heads_per_kv] * wb_ref[i, hb:hb + 1, :], axis=-1, keepdims=True)
        cols.append(acc + p0b[hb:hb + 1, :] * vn[:, h:h + 1])
    o_ref[i] = jnp.concatenate(cols, axis=1)


def kernel(x_prompt, x_sample, cache_a_k, cache_a_v, cache_b_k, cache_b_v, c_prompt, c_sample, rel_bias, w_ada, b_ada, ffn1_wg, ffn1_wu, ffn1_wd, w_in, w_out, sinks, ffn2_wg, ffn2_wu, ffn2_wd, ln_g, ln_b):
    batch, seq, d = x_prompt.shape
    n_dec, dec_seq, _ = x_sample.shape
    depth = w_ada.shape[0]
    assert dec_seq == 1 and d == 2 * A_COLS and all(w // dl == BLK for w, dl in A_PATTERNS)
    alpha = (2 * depth) ** 0.25
    fc = 256
    tm = 512
    tm_proj = 1024 if seq % 1024 == 0 else tm
    a_rows, b_rows = min(A_PATTERNS[-1][0], seq), min(B_WINDOW, seq)
    q_cols = 4 * A_COLS
    kvb = N_KV_B * HEAD_DIM
    heads_per_kv = N_HEADS_B // N_KV_B

    band_map, cached_a, cached_b = _bucket_maps(cache_a_k.shape[2], cache_b_k.shape[2])
    band_bias = _band_bias_call(rel_bias, band_map)
    bias_a, bias_b = band_bias[:len(A_PATTERNS)], band_bias[len(A_PATTERNS)]
    bias_sa, bias_sb = _cached_bias_call(rel_bias, cached_a, cached_b)

    pad = (-(n_dec + batch)) % 16
    c_all = jnp.concatenate([c_sample, c_prompt, jnp.zeros((pad, d), F32)], axis=0)
    xp = x_prompt.reshape(batch * seq, d)
    xs = x_sample.reshape(n_dec, d)
    outs = [[] for _ in range(8)]

    def head_rows(x, n_heads):
        return x.reshape(n_dec, n_heads, HEAD_DIM)

    for l in range(depth):
        mod2 = _ada_call(c_all, w_ada[l], b_ada[l])
        ffn1 = [w[l].astype(BF16) for w in (ffn1_wg, ffn1_wu, ffn1_wd)]
        ffn2 = [w[l].astype(BF16) for w in (ffn2_wg, ffn2_wu, ffn2_wd)]
        w_in_s = w_in[l].astype(BF16)
        wo = w_out[l].astype(BF16)

        s1 = _ffn_call(xs, mod2, (0, 1, 2), 0, 1, *ffn1, ln_g[l, 0], ln_b[l, 0], alpha, n_dec, fc)
        proj_s = _proj_sample_call(s1, mod2, w_in_s)
        c = A_COLS
        sak, sav = proj_s[:, c:2 * c], proj_s[:, 2 * c:3 * c]
        sbk, sbv = proj_s[:, q_cols:q_cols + kvb], proj_s[:, q_cols + kvb:q_cols + 2 * kvb]
        qt3 = jnp.concatenate([head_rows(proj_s[:, :c], N_HEADS_A),
                               head_rows(proj_s[:, 3 * c:q_cols], N_HEADS_B)], axis=1)
        knt3 = jnp.concatenate([head_rows(sak, N_HEADS_A),
                                jnp.repeat(head_rows(sbk, N_KV_B), heads_per_kv, axis=1)], axis=1)
        vnt3 = jnp.concatenate([head_rows(sav, N_HEADS_A),
                                jnp.repeat(head_rows(sbv, N_KV_B), heads_per_kv, axis=1)], axis=1)
        to_cols = (0, 2, 3, 1)
        decode_side = (qt3.transpose(0, 2, 1), qt3, knt3, vnt3.transpose(0, 2, 1),
                       cache_b_k[l].transpose(to_cols), cache_b_v[l].transpose(to_cols),
                       bias_sa, bias_sb, rel_bias.T, sinks[l].reshape(N_HEADS_B, 1),
                       cache_a_k[l].transpose(to_cols), cache_a_v[l].transpose(to_cols))

        n_first = n_dec // 2 if (n_dec // 2) % (2 * (batch * seq // tm)) == 0 else n_dec
        x1, mix3 = _ffn_call(xp, mod2, (0, 1, 2), n_dec, seq, *ffn1, ln_g[l, 0], ln_b[l, 0], alpha, tm, fc,
                             side=(decode_side, 0, n_first))
        p_slots, p4, p16, pak, pav, pbk, pbv = _proj_prompt_call(x1, mod2, n_dec, seq, w_in_s, a_rows, b_rows, tm_proj)
        mix_a = _attn_a_call(p_slots, p4, p16, bias_a, batch, seq)
        mix_b = _attn_b_call(p_slots, bias_b, sinks[l], batch, seq)
        xp = _ffn_call(x1, mod2, (6, 7, 8), n_dec, seq, *ffn2, ln_g[l, 2], ln_b[l, 2], alpha, tm, fc,
                       mix=(mix_a, mix_b), w_out=wo, gate_k=5, ln1_g=ln_g[l, 1], ln1_b=ln_b[l, 1],
                       side=(decode_side, n_first, n_dec - n_first) if n_first < n_dec else None)
        if n_first < n_dec:
            xp, mix3_rest = xp
            mix3 = jnp.concatenate([mix3, mix3_rest], axis=0)

        mix_s = mix3.transpose(0, 2, 1).reshape(n_dec, d)
        xs = _ffn_call(s1, mod2, (6, 7, 8), 0, 1, *ffn2, ln_g[l, 2], ln_b[l, 2], alpha, n_dec, fc,
                       mix=mix_s, w_out=wo, gate_k=5, ln1_g=ln_g[l, 1], ln1_b=ln_b[l, 1])

        new = [pak.reshape(batch, a_rows, N_HEADS_A, HEAD_DIM), pav.reshape(batch, a_rows, N_HEADS_A, HEAD_DIM),
               pbk.reshape(batch, b_rows, N_KV_B, HEAD_DIM), pbv.reshape(batch, b_rows, N_KV_B, HEAD_DIM),
               sak.reshape(n_dec, 1, N_HEADS_A, HEAD_DIM), sav.reshape(n_dec, 1, N_HEADS_A, HEAD_DIM),
               sbk.reshape(n_dec, 1, N_KV_B, HEAD_DIM), sbv.reshape(n_dec, 1, N_KV_B, HEAD_DIM)]
        for acc, arr in zip(outs, new):
            acc.append(arr)

    return (xp.reshape(batch, seq, d), xs.reshape(n_dec, 1, d)) + tuple(jnp.stack(o) for o in outs)
```

```python
import functools
import math

import jax
import jax.numpy as jnp
import numpy as np
from jax import lax
from jax.experimental import pallas as pl
from jax.experimental.pallas import tpu as pltpu

HEAD_DIM = 64
N_HEADS_A = 8
N_HEADS_B = 8
N_KV_B = 2
A_PATTERNS = ((128, 1), (512, 4), (2048, 16))
B_WINDOW = 128
BLK = 128
NUM_BUCKETS = 32
MAX_DISTANCE = 2048
LN_EPS = 1e-5
NEG = -1e30
SCALE = HEAD_DIM ** -0.5

LANES = 128
SUBLANES = 8
V7X_VMEM_BYTES = 64 * 1024 * 1024
A_COLS = N_HEADS_A * HEAD_DIM
N_PAIRS = A_COLS // LANES

F32 = jnp.float32
BF16 = jnp.bfloat16


SPILL_ALLOWANCE_BYTES = 4 << 20
VMEM_RESERVED_BYTES = 6 << 20


def _vmem_limit(pipelined_bytes, scratch_bytes=0, temp_bytes=0):
    want = 2 * pipelined_bytes + scratch_bytes + temp_bytes + SPILL_ALLOWANCE_BYTES
    return int(min(want, V7X_VMEM_BYTES - VMEM_RESERVED_BYTES))


def _resident_spec(shape):
    return pl.BlockSpec(shape, lambda *_: (0,) * len(shape), pipeline_mode=pl.Buffered(1))


def _nbytes(shape, dtype):
    return math.prod(shape) * jnp.dtype(dtype).itemsize


def _layer_norm(x, g, b):
    mu = jnp.mean(x, axis=-1, keepdims=True)
    xc = x - mu
    var = jnp.mean(xc * xc, axis=-1, keepdims=True)
    return xc * lax.rsqrt(var + LN_EPS) * g + b


def _silu(x):
    return x * jax.nn.sigmoid(x)


def _seq_mod_spec(seq0, tiles_per_seq, k, d):
    return pl.BlockSpec((SUBLANES, d), lambda i: ((seq0 + i // tiles_per_seq) // SUBLANES, k))


def _seq_mod(ref, mod_row):
    if mod_row is None:
        return ref[...]
    seq0, tiles_per_seq = mod_row
    return ref[pl.ds((seq0 + pl.program_id(0) // tiles_per_seq) % SUBLANES, 1), :]


def _ada_kernel(c_ref, w_ref, b_ref, o_ref):
    a = _silu(c_ref[...]).astype(BF16)
    o_ref[...] = jnp.dot(a, w_ref[...].astype(BF16), preferred_element_type=F32) + b_ref[...]


def _ada_call(c_all, w_ada, b_ada):
    m, d = c_all.shape
    n = w_ada.shape[1]
    tn = n // 8
    assert tn % LANES == 0
    blocks = _nbytes((m, d), F32) + _nbytes((d, tn), F32) + _nbytes((m, tn), F32)
    return pl.pallas_call(
        _ada_kernel,
        grid=(n // tn,),
        in_specs=[pl.BlockSpec((m, d), lambda j: (0, 0)),
                  pl.BlockSpec((d, tn), lambda j: (0, j)),
                  pl.BlockSpec((1, tn), lambda j: (0, j))],
        out_specs=pl.BlockSpec((m, tn), lambda j: (0, j)),
        out_shape=jax.ShapeDtypeStruct((m, n), F32),
        compiler_params=pltpu.CompilerParams(
            dimension_semantics=("arbitrary",),
            vmem_limit_bytes=_vmem_limit(blocks, temp_bytes=_nbytes((d, tn), BF16))),
        name="ada_mod",
    )(c_all, w_ada, b_ada.reshape(1, n))


def _t5_bucket(n):
    max_exact = NUM_BUCKETS // 2
    nf = np.maximum(n, 1).astype(np.float32)
    large = max_exact + (np.log(nf / max_exact) / math.log(MAX_DISTANCE / max_exact)
                         * (NUM_BUCKETS - max_exact)).astype(np.int32)
    return np.where(n < max_exact, n, np.minimum(large, NUM_BUCKETS - 1))


def _bucket_maps(a_rows, b_rows):
    steps = [d for _, d in A_PATTERNS] + [1]
    q = np.arange(BLK)[:, None]
    k = np.arange(2 * BLK)[None, :]
    dist = q + BLK - k
    valid = (dist >= 0) & (dist <= BLK)
    band = np.stack([np.where(valid, _t5_bucket(np.maximum(dist, 0) * s), -1) for s in steps])

    def cached(rows, step):
        back = rows - np.arange(rows)
        hit = (back % step == 0) & (back // step <= BLK)
        return np.where(hit, _t5_bucket(back), -1)[None, :].astype(np.int32)

    cached_a = np.stack([cached(a_rows, s) for s in steps[:-1]])
    return band.astype(np.int32), cached_a, cached(b_rows, 1)


def _band_bias_kernel(rel_ref, bmap_ref, o_ref):
    head0 = jnp.where(pl.program_id(0) == len(A_PATTERNS), N_HEADS_A, 0)
    rows = 2 * SUBLANES
    for r0 in range(0, BLK, rows):
        bm = bmap_ref[r0:r0 + rows, :]
        accs = [jnp.full(bm.shape, NEG, F32)] * (2 * N_PAIRS)
        for b in range(NUM_BUCKETS):
            hit = bm == b
            accs = [jnp.where(hit, rel_ref[b, head0 + h], acc) for h, acc in enumerate(accs)]
        for h, acc in enumerate(accs):
            o_ref[h // 2, (h % 2) * BLK + r0:(h % 2) * BLK + r0 + rows, :] = acc


def _band_bias_call(rel_bias, band_map):
    n_tab = band_map.shape[0]
    return pl.pallas_call(
        _band_bias_kernel,
        grid=(n_tab,),
        in_specs=[pl.BlockSpec(memory_space=pltpu.SMEM),
                  pl.BlockSpec((None, BLK, 2 * BLK), lambda t: (t, 0, 0))],
        out_specs=pl.BlockSpec((None, N_PAIRS, 2 * BLK, 2 * BLK), lambda t: (t, 0, 0, 0)),
        out_shape=jax.ShapeDtypeStruct((n_tab, N_PAIRS, 2 * BLK, 2 * BLK), F32),
        compiler_params=pltpu.CompilerParams(dimension_semantics=("arbitrary",)),
        name="band_bias",
    )(rel_bias, band_map)


def _cached_bias_kernel(relt_ref, amap_ref, bmap_ref, oa_ref, ob_ref):
    relt = relt_ref[...]

    def table(bm, heads):
        acc = jnp.full(bm.shape, NEG, F32)
        for b in range(NUM_BUCKETS):
            acc = jnp.where(bm == b, heads[:, b:b + 1], acc)
        return acc

    for t in range(amap_ref.shape[0]):
        oa_ref[t] = table(jnp.broadcast_to(amap_ref[t], oa_ref.shape[1:]), relt[:N_HEADS_A])
    ob_ref[...] = table(jnp.broadcast_to(bmap_ref[...], ob_ref.shape), relt[N_HEADS_A:])


def _cached_bias_call(rel_bias, cached_a, cached_b):
    return pl.pallas_call(
        _cached_bias_kernel,
        out_shape=(jax.ShapeDtypeStruct((cached_a.shape[0], N_HEADS_A, cached_a.shape[2]), F32),
                   jax.ShapeDtypeStruct((N_HEADS_B, cached_b.shape[1]), F32)),
        name="cached_bias",
    )(rel_bias.T, cached_a, cached_b)


def _ffn_kernel(*refs, alpha, pre_mix, mix_slots, n_chunks, fc, n_side, sample0, mod_row):
    refs = list(refs)
    x_ref = refs.pop(0)
    if pre_mix:
        mix_refs = [refs.pop(0) for _ in range(2 if mix_slots else 1)]
        wo_ref, gm_ref, ln1g_ref, ln1b_ref = (refs.pop(0) for _ in range(4))
    sh_ref, sc_ref, gt_ref, wg_ref, wu_ref, wd_ref, lng_ref, lnb_ref = (refs.pop(0) for _ in range(8))
    if n_side:
        side_in = [refs.pop(0) for _ in range(10)]
        ak_hbm, av_hbm, o_ref, oc_ref, h_ref, acc_ref, kbuf, vbuf, sem = (refs.pop(0) for _ in range(9))
        side_scratch = refs
        end = sample0 + pl.num_programs(0) * n_side
        first = sample0 + pl.program_id(0) * n_side

        def cache_copies(g, slot):
            return (pltpu.make_async_copy(ak_hbm.at[g], kbuf.at[slot], sem.at[0, slot]),
                    pltpu.make_async_copy(av_hbm.at[g], vbuf.at[slot], sem.at[1, slot]))

        @pl.when(pl.program_id(0) == 0)
        def _():
            for cp in cache_copies(sample0, 0):
                cp.start()

        def side(j):
            g, slot = first + j, j % 2
            for cp in cache_copies(g, slot):
                cp.wait()

            @pl.when(g + 1 < end)
            def _():
                for cp in cache_copies(g + 1, 1 - slot):
                    cp.start()

            _cached_sample(j, kbuf.at[slot], vbuf.at[slot], *side_in, oc_ref, *side_scratch)

        assert n_side % 2 == 0
        side_at = {(j * n_chunks) // n_side if j else -1: j for j in range(n_side)}
    else:
        o_ref, h_ref, acc_ref = refs
        side_at = {}

    if -1 in side_at:
        side(side_at[-1])
    x = x_ref[...]
    if pre_mix:
        if mix_slots:
            mixed = jnp.concatenate([r[j] for r in mix_refs for j in range(N_PAIRS)], axis=-1)
        else:
            mixed = mix_refs[0][...].astype(BF16)
        y = jnp.dot(mixed, wo_ref[...], preferred_element_type=F32)
        x = _layer_norm(alpha * x + _seq_mod(gm_ref, mod_row) * y, ln1g_ref[...], ln1b_ref[...])
    h_ref[...] = (x * (1.0 + _seq_mod(sc_ref, mod_row)) + _seq_mod(sh_ref, mod_row)).astype(BF16)
    for c in range(n_chunks):
        if c in side_at:
            side(side_at[c])
        cols = slice(c * fc, (c + 1) * fc)
        g = jnp.dot(h_ref[...], wg_ref[:, cols], preferred_element_type=F32)
        u = jnp.dot(h_ref[...], wu_ref[:, cols], preferred_element_type=F32)
        y = jnp.dot((_silu(g) * u).astype(BF16), wd_ref[cols, :], preferred_element_type=F32)
        if c == 0:
            acc_ref[...] = y
        else:
            acc_ref[...] += y
    o_ref[...] = _layer_norm(alpha * x + 0.5 * _seq_mod(gt_ref, mod_row) * acc_ref[...], lng_ref[...], lnb_ref[...])


def _ffn_call(x, mod2, mod_ks, seq0, rows_per_seq, wg, wu, wd, ln_g, ln_b, alpha, tm, fc,
              mix=None, w_out=None, gate_k=None, ln1_g=None, ln1_b=None, side=None):
    n, d = x.shape
    d_ff = wg.shape[1]
    n_chunks = d_ff // fc
    assert n_chunks * fc == d_ff
    per_row = rows_per_seq == 1
    tiles_per_seq = max(rows_per_seq // tm, 1)

    def mod_spec(k):
        if per_row:
            return pl.BlockSpec((tm, d), lambda i, k=k: (seq0 // tm + i, k))
        return _seq_mod_spec(seq0, tiles_per_seq, k, d)

    mod_arr = mod2
    row_spec = pl.BlockSpec((tm, d), lambda i: (i, 0))
    vec_spec = pl.BlockSpec((1, d), lambda i: (0, 0))
    args, specs = [x], [row_spec]
    pre_mix = mix is not None
    mix_slots = pre_mix and isinstance(mix, tuple)
    blocks = 2 * _nbytes((tm, d), F32) + (3 * _nbytes((tm, d), F32) if per_row else 0)
    resident = 3 * _nbytes(wg.shape, BF16)
    if pre_mix:
        if mix_slots:
            for mm in mix:
                args.append(mm)
                specs.append(pl.BlockSpec((N_PAIRS, tm, LANES), lambda i: (0, i, 0)))
        else:
            args.append(mix)
            specs.append(row_spec)
        args += [w_out, mod_arr, ln1_g.reshape(1, d), ln1_b.reshape(1, d)]
        specs += [_resident_spec((d, d)), mod_spec(gate_k), vec_spec, vec_spec]
        blocks += _nbytes((tm, d), F32)
        resident += _nbytes((d, d), BF16)
    args += [mod_arr, mod_arr, mod_arr, wg, wu, wd, ln_g.reshape(1, d), ln_b.reshape(1, d)]
    specs += [mod_spec(mod_ks[0]), mod_spec(mod_ks[1]), mod_spec(mod_ks[2]),
              _resident_spec(wg.shape), _resident_spec(wu.shape), _resident_spec(wd.shape), vec_spec, vec_spec]
    scratch = _nbytes((tm, d), BF16) + _nbytes((tm, d), F32) + resident
    temps = 6 * _nbytes((tm, fc), F32) + 3 * _nbytes((tm, d), F32)
    out_specs, out_shape = row_spec, jax.ShapeDtypeStruct((n, d), F32)
    scratch_shapes = [pltpu.VMEM((tm, d), BF16), pltpu.VMEM((tm, d), F32)]
    n_side = sample0 = 0
    if side is not None:
        (*small, ak, av), sample0, count = side
        n_steps = n // tm
        n_side = count // n_steps
        assert n_side * n_steps == count and sample0 % n_side == 0
        blk0 = sample0 // n_side
        for a in small[:6]:
            args.append(a)
            specs.append(pl.BlockSpec((n_side,) + a.shape[1:],
                                      lambda i, nd=a.ndim: (blk0 + i,) + (0,) * (nd - 1)))
            blocks += n_side * _nbytes(a.shape[1:-1] + (max(a.shape[-1], LANES),), F32)
        for a in small[6:]:
            args.append(a)
            specs.append(pl.BlockSpec(a.shape, lambda i, nd=a.ndim: (0,) * nd))
            blocks += _nbytes(a.shape, F32)
        args += [ak, av]
        specs += [pl.BlockSpec(memory_space=pl.ANY)] * 2
        q3 = small[0]
        out_specs = [row_spec, pl.BlockSpec((n_side,) + q3.shape[1:], lambda i: (i, 0, 0))]
        out_shape = [out_shape, jax.ShapeDtypeStruct((count,) + q3.shape[1:], F32)]
        sa_shape = (n_side, N_HEADS_A, ak.shape[3])
        sb_shape = (n_side, N_HEADS_B, small[4].shape[3])
        scratch_shapes += [pltpu.VMEM((2,) + ak.shape[1:], F32), pltpu.VMEM((2,) + av.shape[1:], F32),
                           pltpu.SemaphoreType.DMA((2, 2)),
                           pltpu.VMEM(sa_shape, F32), pltpu.VMEM(sb_shape, F32),
                           pltpu.VMEM(sa_shape, F32), pltpu.VMEM(sb_shape, F32)]
        scratch += 4 * _nbytes(ak.shape[1:], F32) + 2 * _nbytes(sa_shape, F32) + 2 * _nbytes(sb_shape, F32)
    return pl.pallas_call(
        functools.partial(_ffn_kernel, alpha=alpha, pre_mix=pre_mix, mix_slots=mix_slots,
                          n_chunks=n_chunks, fc=fc, n_side=n_side, sample0=sample0,
                          mod_row=None if per_row else (seq0, tiles_per_seq)),
        grid=(n // tm,),
        in_specs=specs,
        out_specs=out_specs,
        out_shape=out_shape,
        scratch_shapes=scratch_shapes,
        compiler_params=pltpu.CompilerParams(
            dimension_semantics=("arbitrary",),
            vmem_limit_bytes=_vmem_limit(blocks, scratch, temps)),
        name=("ffn_mix" if pre_mix else "ffn") + ("_cached" if n_side else ""),
    )(*args)


def _proj_prompt_kernel(x_ref, sh_ref, sc_ref, w_ref, p_ref, p4_ref, p16_ref, ak_ref, av_ref, bk_ref, bv_ref,
                        stage_ref, stage4_ref, *, tm, b_rows, mod_row):
    h = (x_ref[...] * (1.0 + _seq_mod(sc_ref, mod_row)) + _seq_mod(sh_ref, mod_row)).astype(BF16)
    lane_lo = lax.broadcasted_iota(jnp.int32, (tm, LANES), 1) < HEAD_DIM
    for g in range(4):
        res = jnp.dot(h, w_ref[:, g * A_COLS:(g + 1) * A_COLS], preferred_element_type=F32)
        scaled = res * SCALE if g in (0, 3) else res
        for s in range(N_PAIRS):
            slab = scaled[:, s * LANES:(s + 1) * LANES]
            p_ref[g * N_PAIRS + s] = slab.astype(BF16)
            if g < 3:
                slot = g * N_PAIRS + s
                d4, d16 = p4_ref.shape[1], p16_ref.shape[1]
                stage_ref[slot] = slab
                for r in range(d4):
                    rows = stage_ref[slot, pl.ds(r, tm // d4, stride=d4), :]
                    p4_ref[slot, r] = rows.astype(BF16)
                    stage4_ref[slot, r] = rows
                for r in range(d16):
                    rows = stage4_ref[slot, r % d4, pl.ds(r // d4, tm // d16, stride=d16 // d4), :]
                    p16_ref[slot, r] = rows.astype(BF16)
        if g in (1, 2):
            out_ref = ak_ref if g == 1 else av_ref
            out_ref[...] = res
    res = jnp.dot(h, w_ref[:, 4 * A_COLS:], preferred_element_type=F32)
    for j, out_ref in enumerate((bk_ref, bv_ref)):
        slab = res[:, j * LANES:(j + 1) * LANES]
        swapped = pltpu.roll(slab, HEAD_DIM, axis=1)
        p_ref[4 * N_PAIRS + N_KV_B * j] = jnp.where(lane_lo, slab, swapped).astype(BF16)
        p_ref[4 * N_PAIRS + N_KV_B * j + 1] = jnp.where(lane_lo, swapped, slab).astype(BF16)

        out_ref[...] = slab[tm - b_rows:, :]


def _proj_prompt_call(x, mod2, seq0, seq, w_ext, a_rows, b_rows, tm):
    n, d = x.shape
    batch = n // seq
    tiles_per_seq = seq // tm
    a_tiles = a_rows // tm
    n_slots = 4 * N_PAIRS + 2 * N_KV_B
    assert w_ext.shape[1] == 4 * A_COLS + 2 * LANES and N_KV_B * HEAD_DIM == LANES

    def mod_spec(k):
        return _seq_mod_spec(seq0, tiles_per_seq, k, d)

    def a_map(i):
        return (i // tiles_per_seq, jnp.maximum(i % tiles_per_seq - (tiles_per_seq - a_tiles), 0), 0)

    a_blk = (None, tm, A_COLS)
    b_blk = (None, b_rows, LANES)
    a_slots = 3 * N_PAIRS
    d4, d16 = A_PATTERNS[1][1], A_PATTERNS[2][1]
    assert tm % (16 * d16) == 0

    def dil_spec(dil):
        return pl.BlockSpec((a_slots, None, dil, tm // dil, LANES),
                            lambda i: (0, i // tiles_per_seq, 0, i % tiles_per_seq, 0))

    blocks = (_nbytes((tm, d), F32) + _nbytes(w_ext.shape, BF16) + _nbytes((n_slots + 2 * a_slots, tm, LANES), BF16)
              + 2 * _nbytes((tm, A_COLS), F32) + 2 * _nbytes((b_rows, LANES), F32))
    return pl.pallas_call(
        functools.partial(_proj_prompt_kernel, tm=tm, b_rows=b_rows, mod_row=(seq0, tiles_per_seq)),
        grid=(n // tm,),
        in_specs=[pl.BlockSpec((tm, d), lambda i: (i, 0)), mod_spec(3), mod_spec(4),
                  pl.BlockSpec(w_ext.shape, lambda i: (0, 0))],
        out_specs=[pl.BlockSpec((n_slots, tm, LANES), lambda i: (0, i, 0)),
                   dil_spec(d4), dil_spec(d16),
                   pl.BlockSpec(a_blk, a_map),
                   pl.BlockSpec(a_blk, a_map),
                   pl.BlockSpec(b_blk, lambda i: (i // tiles_per_seq, 0, 0)),
                   pl.BlockSpec(b_blk, lambda i: (i // tiles_per_seq, 0, 0))],
        out_shape=[jax.ShapeDtypeStruct((n_slots, n, LANES), BF16),
                   jax.ShapeDtypeStruct((a_slots, batch, d4, seq // d4, LANES), BF16),
                   jax.ShapeDtypeStruct((a_slots, batch, d16, seq // d16, LANES), BF16),
                   jax.ShapeDtypeStruct((batch, a_rows, A_COLS), F32),
                   jax.ShapeDtypeStruct((batch, a_rows, A_COLS), F32),
                   jax.ShapeDtypeStruct((batch, b_rows, LANES), F32),
                   jax.ShapeDtypeStruct((batch, b_rows, LANES), F32)],
        scratch_shapes=[pltpu.VMEM((a_slots, tm, LANES), F32), pltpu.VMEM((a_slots, d4, tm // d4, LANES), F32)],
        compiler_params=pltpu.CompilerParams(
            dimension_semantics=("arbitrary",),
            vmem_limit_bytes=_vmem_limit(blocks, 2 * _nbytes((a_slots, tm, LANES), F32),
                                         4 * _nbytes((tm, A_COLS), F32))),
        name="proj_prompt",
    )(x, mod2, mod2, w_ext)


def _proj_sample_kernel(x_ref, sh_ref, sc_ref, w_ref, o_ref):
    h = (x_ref[...] * (1.0 + sc_ref[...]) + sh_ref[...]).astype(BF16)
    o_ref[...] = jnp.dot(h, w_ref[...], preferred_element_type=F32)


def _proj_sample_call(x, mod2, w_ext):
    n, d = x.shape
    cols = w_ext.shape[1]
    blocks = 3 * _nbytes((n, d), F32) + _nbytes(w_ext.shape, BF16) + _nbytes((n, cols), F32)
    return pl.pallas_call(
        _proj_sample_kernel,
        grid=(1,),
        in_specs=[pl.BlockSpec((n, d), lambda i: (0, 0)),
                  pl.BlockSpec((n, d), lambda i: (0, 3)),
                  pl.BlockSpec((n, d), lambda i: (0, 4)),
                  pl.BlockSpec(w_ext.shape, lambda i: (0, 0))],
        out_specs=pl.BlockSpec((n, cols), lambda i: (0, 0)),
        out_shape=jax.ShapeDtypeStruct((n, cols), F32),
        compiler_params=pltpu.CompilerParams(
            dimension_semantics=("arbitrary",), vmem_limit_bytes=_vmem_limit(blocks)),
        name="proj_sample",
    )(x, mod2, mod2, w_ext)


def _pair_scores(q, kk, bias, lane_lo):
    zero = jnp.zeros_like(q)
    q2 = jnp.concatenate([jnp.where(lane_lo, q, zero), jnp.where(lane_lo, zero, q)], axis=0)
    s = lax.dot_general(q2, kk, (((1,), (1,)), ((), ())), preferred_element_type=F32)
    return s + bias


def _pair_select(a, lane_lo):
    return jnp.where(lane_lo, a[:BLK], a[BLK:])


GROUP = 16


def _band_sequence(q_ref, k_ref, v_ref, n_blocks, load_bias, block):
    def one(jb, first):
        if first:
            block(jb, q_ref[0:BLK, :], k_ref[0:BLK, :], v_ref[0:BLK, :], load_bias(True))
        else:
            r0 = jb * BLK
            if not isinstance(jb, int):
                r0 = pl.multiple_of(r0, BLK)
            block(jb, q_ref[pl.ds(r0, BLK), :], k_ref[pl.ds(r0 - BLK, 2 * BLK), :],
                  v_ref[pl.ds(r0 - BLK, 2 * BLK), :], load_bias(False))

    for jb in range(min(GROUP, n_blocks)):
        one(jb, jb == 0)
    if n_blocks > GROUP:
        assert n_blocks % GROUP == 0

        def group(g, carry):
            for j in range(GROUP):
                one(g * GROUP + j, False)
            return carry

        lax.fori_loop(1, n_blocks // GROUP, group, 0)


def _attn_a_kernel(q16_ref, k16_ref, v16_ref, q4_ref, k4_ref, v4_ref, q1_ref, k1_ref, v1_ref,
                   bias_ref, o_ref, acc16_ref, m16_ref, den16_ref, acc4_ref, m4_ref, den4_ref, *, seq):
    lane_lo = lax.broadcasted_iota(jnp.int32, (BLK, LANES), 1) < HEAD_DIM
    state16 = (acc16_ref, m16_ref, den16_ref)
    state4 = (acc4_ref, m4_ref, den4_ref)

    def update(q, kk, vv, bias, rows, old, new):
        s = _pair_scores(q, kk, bias, lane_lo)
        part = s[:, :LANES] if s.shape[1] == LANES else jnp.maximum(s[:, :LANES], s[:, LANES:])
        if old is not None:
            acc_old, m_old, den_old = (ref[rows, :] for ref in old)
            part = jnp.maximum(part, jnp.concatenate([jnp.where(lane_lo, m_old, NEG),
                                                      jnp.where(lane_lo, NEG, m_old)], axis=0))
        m2 = jnp.max(part, axis=-1, keepdims=True)
        p = jnp.exp(s - m2)
        l2 = jnp.sum(p, axis=-1, keepdims=True)
        pv = _pair_select(jnp.dot(p.astype(BF16), vv, preferred_element_type=F32), lane_lo)
        m_new = _pair_select(m2, lane_lo)
        den = _pair_select(l2, lane_lo)
        if old is None:
            acc = pv
        else:
            a = jnp.exp(m_old - m_new)
            acc = a * acc_old + pv
            den = a * den_old + den
        if new is None:
            o_ref[rows, :] = (acc / den).astype(o_ref.dtype)
        else:
            new[0][rows, :] = acc
            new[1][rows, :] = m_new
            new[2][rows, :] = den

    def branch(q_ref, k_ref, v_ref, table, dil, r, old, new):
        def block(jb, q, kk, vv, bias):
            start = r + jb * (BLK * dil)
            rows = pl.ds(start, BLK, stride=dil) if dil > 1 else pl.ds(start, BLK)
            update(q, kk, vv, bias, rows, old, new)

        def load_bias(first):
            return bias_ref[table, :, BLK:2 * BLK] if first else bias_ref[table]

        _band_sequence(q_ref, k_ref, v_ref, q_ref.shape[0] // BLK, load_bias, block)

    def dilated(q_ref, k_ref, v_ref, table, old, new):
        dil = q_ref.shape[0]
        per_group = min(max(GROUP // (q_ref.shape[1] // BLK), 1), dil)
        assert dil % per_group == 0

        def group(g, carry):
            for rr in range(per_group):
                r = g * per_group + rr
                branch(q_ref.at[r], k_ref.at[r], v_ref.at[r], table, dil, r, old, new)
            return carry

        lax.fori_loop(0, dil // per_group, group, 0)

    dilated(q16_ref, k16_ref, v16_ref, 2, None, state16)
    dilated(q4_ref, k4_ref, v4_ref, 1, state16, state4)
    branch(q1_ref, k1_ref, v1_ref, 0, 1, 0, state4, None)


def _attn_a_call(p_slots, p4, p16, bias_a, batch, seq):
    n_slots = p_slots.shape[0]
    dils = [d for _, d in A_PATTERNS]
    assert dils == [1, p4.shape[2], p16.shape[2]] and seq % (dils[2] * BLK) == 0
    view1 = p_slots.reshape(n_slots, batch, seq, LANES)

    in_specs, args = [], []
    for arr in (p16, p4, view1):
        for slot0 in (0, N_PAIRS, 2 * N_PAIRS):
            in_specs.append(pl.BlockSpec((None, None) + arr.shape[2:],
                                         lambda b, hp, slot0=slot0, nd=arr.ndim: (slot0 + hp, b) + (0,) * (nd - 2)))
            args.append(arr)
    in_specs.append(pl.BlockSpec((len(A_PATTERNS), None, 2 * BLK, 2 * BLK), lambda b, hp: (0, hp, 0, 0)))
    args.append(bias_a)
    blocks = 10 * _nbytes((seq, LANES), BF16) + _nbytes((3, 2 * BLK, 2 * BLK), F32)
    scratch = 6 * _nbytes((seq, LANES), F32)
    return pl.pallas_call(
        functools.partial(_attn_a_kernel, seq=seq),
        grid=(batch, N_PAIRS),
        in_specs=in_specs,
        out_specs=pl.BlockSpec((None, seq, LANES), lambda b, hp: (hp, b, 0)),
        out_shape=jax.ShapeDtypeStruct((N_PAIRS, batch * seq, LANES), BF16),
        scratch_shapes=[pltpu.VMEM((seq, LANES), F32)] * 6,
        compiler_params=pltpu.CompilerParams(
            dimension_semantics=("arbitrary", "arbitrary"),
            vmem_limit_bytes=_vmem_limit(blocks, scratch, 4 * GROUP * _nbytes((2 * BLK, 2 * BLK), F32))),
        name="attn_dilated",
    )(*args)


def _attn_b_kernel(sink_ref, q_ref, k_ref, v_ref, bias_ref, o_ref, *, seq):
    hp = pl.program_id(1)
    lane_lo = lax.broadcasted_iota(jnp.int32, (BLK, LANES), 1) < HEAD_DIM
    row_lo = lax.broadcasted_iota(jnp.int32, (2 * BLK, 1), 0) < BLK
    sink2 = jnp.where(row_lo, sink_ref[2 * hp], sink_ref[2 * hp + 1])

    def block(jb, q, kk, vv, bias):
        s = _pair_scores(q, kk, bias, lane_lo)
        m2 = jnp.maximum(jnp.max(s, axis=-1, keepdims=True), sink2)
        p = jnp.exp(s - m2)
        l2 = jnp.sum(p, axis=-1, keepdims=True) + jnp.exp(sink2 - m2)
        pv = _pair_select(jnp.dot(p.astype(BF16), vv, preferred_element_type=F32), lane_lo)
        r0 = jb * BLK if isinstance(jb, int) else pl.multiple_of(jb * BLK, BLK)
        o_ref[pl.ds(r0, BLK), :] = (pv / _pair_select(l2, lane_lo)).astype(o_ref.dtype)

    def load_bias(first):
        return bias_ref[:, BLK:2 * BLK] if first else bias_ref[...]

    _band_sequence(q_ref, k_ref, v_ref, seq // BLK, load_bias, block)


def _attn_b_call(p_slots, bias_b, sinks, batch, seq):
    n_slots = p_slots.shape[0]
    view = p_slots.reshape(n_slots, batch, seq, LANES)
    q0, k0, v0 = 3 * N_PAIRS, 4 * N_PAIRS, 4 * N_PAIRS + N_KV_B
    pairs_per_kv = N_PAIRS // N_KV_B
    blk = (None, None, seq, LANES)
    blocks = 4 * _nbytes((seq, LANES), BF16) + _nbytes((2 * BLK, 2 * BLK), F32)
    return pl.pallas_call(
        functools.partial(_attn_b_kernel, seq=seq),
        grid=(batch, N_PAIRS),
        in_specs=[pl.BlockSpec(memory_space=pltpu.SMEM),
                  pl.BlockSpec(blk, lambda b, hp: (q0 + hp, b, 0, 0)),
                  pl.BlockSpec(blk, lambda b, hp: (k0 + hp // pairs_per_kv, b, 0, 0)),
                  pl.BlockSpec(blk, lambda b, hp: (v0 + hp // pairs_per_kv, b, 0, 0)),
                  pl.BlockSpec((None, 2 * BLK, 2 * BLK), lambda b, hp: (hp, 0, 0))],
        out_specs=pl.BlockSpec((None, seq, LANES), lambda b, hp: (hp, b, 0)),
        out_shape=jax.ShapeDtypeStruct((N_PAIRS, batch * seq, LANES), BF16),
        compiler_params=pltpu.CompilerParams(
            dimension_semantics=("arbitrary", "arbitrary"),
            vmem_limit_bytes=_vmem_limit(blocks, temp_bytes=8 * _nbytes((2 * BLK, 2 * BLK), F32))),
        name="attn_window",
    )(sinks, view, view, view, bias_b)


def _cached_sample(i, ak_ref, av_ref, q_ref, qt_ref, knt_ref, vn_ref, bk_ref, bv_ref, bias_a_ref, bias_b_ref,
                   relt_ref, sink_ref, o_ref, sa_ref, sb_ref, wa_ref, wb_ref):
    n_branches = bias_a_ref.shape[0]
    heads_per_kv = N_HEADS_B // N_KV_B
    bias0 = relt_ref[:, 0:1]
    sink = sink_ref[...]
    q = q_ref[i] * SCALE
    vn = vn_ref[i]
    s0 = jnp.sum(qt_ref[i] * SCALE * knt_ref[i], axis=-1, keepdims=True) + bias0
    for h in range(N_HEADS_A):
        sa_ref[i, h:h + 1, :] = jnp.sum(ak_ref[h] * q[:, h:h + 1], axis=0, keepdims=True)
    for hb in range(N_HEADS_B):
        h = N_HEADS_A + hb
        sb_ref[i, hb:hb + 1, :] = jnp.sum(bk_ref[i, hb // heads_per_kv] * q[:, h:h + 1], axis=0, keepdims=True)

    s0a = s0[:N_HEADS_A]
    ts = [sa_ref[i] + bias_a_ref[t] for t in range(n_branches)]
    m = s0a
    for t_ in ts:
        m = jnp.maximum(m, jnp.max(t_, axis=-1, keepdims=True))
    w = jnp.exp(ts[0] - m)
    for t_ in ts[1:]:
        w = w + jnp.exp(t_ - m)
    p0a = n_branches * jnp.exp(s0a - m)
    inv = 1.0 / (jnp.sum(w, axis=-1, keepdims=True) + p0a)
    wa_ref[i] = w * inv
    p0a = p0a * inv

    s0b = s0[N_HEADS_A:]
    tb = sb_ref[i] + bias_b_ref[...]
    mb = jnp.maximum(jnp.maximum(s0b, sink), jnp.max(tb, axis=-1, keepdims=True))
    wb = jnp.exp(tb - mb)
    p0b = jnp.exp(s0b - mb)
    invb = 1.0 / (jnp.sum(wb, axis=-1, keepdims=True) + p0b + jnp.exp(sink - mb))
    wb_ref[i] = wb * invb
    p0b = p0b * invb

    cols = []
    for h in range(N_HEADS_A):
        acc = jnp.sum(av_ref[h] * wa_ref[i, h:h + 1, :], axis=-1, keepdims=True)
        cols.append(acc + p0a[h:h + 1, :] * vn[:, h:h + 1])
    for hb in range(N_HEADS_B):
        h = N_HEADS_A + hb
        acc = jnp.sum(bv_ref[i, hb // heads_per_kv] * wb_ref[i, hb:hb + 1, :], axis=-1, keepdims=True)
        cols.append(acc + p0b[hb:hb + 1, :] * vn[:, h:h + 1])
    o_ref[i] = jnp.concatenate(cols, axis=1)


def kernel(x_prompt, x_sample, cache_a_k, cache_a_v, cache_b_k, cache_b_v, c_prompt, c_sample, rel_bias, w_ada, b_ada, ffn1_wg, ffn1_wu, ffn1_wd, w_in, w_out, sinks, ffn2_wg, ffn2_wu, ffn2_wd, ln_g, ln_b):
    batch, seq, d = x_prompt.shape
    n_dec, dec_seq, _ = x_sample.shape
    depth = w_ada.shape[0]
    assert dec_seq == 1 and d == 2 * A_COLS and all(w // dl == BLK for w, dl in A_PATTERNS)
    alpha = (2 * depth) ** 0.25
    fc = 256
    tm = 512
    tm_proj = 1024 if seq % 1024 == 0 else tm
    a_rows, b_rows = min(A_PATTERNS[-1][0], seq), min(B_WINDOW, seq)
    q_cols = 4 * A_COLS
    kvb = N_KV_B * HEAD_DIM
    heads_per_kv = N_HEADS_B // N_KV_B

    band_map, cached_a, cached_b = _bucket_maps(cache_a_k.shape[2], cache_b_k.shape[2])
    band_bias = _band_bias_call(rel_bias, band_map)
    bias_a, bias_b = band_bias[:len(A_PATTERNS)], band_bias[len(A_PATTERNS)]
    bias_sa, bias_sb = _cached_bias_call(rel_bias, cached_a, cached_b)

    pad = (-(n_dec + batch)) % 16
    c_all = jnp.concatenate([c_sample, c_prompt, jnp.zeros((pad, d), F32)], axis=0)
    xp = x_prompt.reshape(batch * seq, d)
    xs = x_sample.reshape(n_dec, d)
    outs = [[] for _ in range(8)]

    def head_rows(x, n_heads):
        return x.reshape(n_dec, n_heads, HEAD_DIM)

    for l in range(depth):
        mod2 = _ada_call(c_all, w_ada[l], b_ada[l])
        ffn1 = [w[l].astype(BF16) for w in (ffn1_wg, ffn1_wu, ffn1_wd)]
        ffn2 = [w[l].astype(BF16) for w in (ffn2_wg, ffn2_wu, ffn2_wd)]
        w_in_s = w_in[l].astype(BF16)
        wo = w_out[l].astype(BF16)

        s1 = _ffn_call(xs, mod2, (0, 1, 2), 0, 1, *ffn1, ln_g[l, 0], ln_b[l, 0], alpha, n_dec, fc)
        proj_s = _proj_sample_call(s1, mod2, w_in_s)
        c = A_COLS
        sak, sav = proj_s[:, c:2 * c], proj_s[:, 2 * c:3 * c]
        sbk, sbv = proj_s[:, q_cols:q_cols + kvb], proj_s[:, q_cols + kvb:q_cols + 2 * kvb]
        qt3 = jnp.concatenate([head_rows(proj_s[:, :c], N_HEADS_A),
                               head_rows(proj_s[:, 3 * c:q_cols], N_HEADS_B)], axis=1)
        knt3 = jnp.concatenate([head_rows(sak, N_HEADS_A),
                                jnp.repeat(head_rows(sbk, N_KV_B), heads_per_kv, axis=1)], axis=1)
        vnt3 = jnp.concatenate([head_rows(sav, N_HEADS_A),
                                jnp.repeat(head_rows(sbv, N_KV_B), heads_per_kv, axis=1)], axis=1)
        to_cols = (0, 2, 3, 1)
        decode_side = (qt3.transpose(0, 2, 1), qt3, knt3, vnt3.transpose(0, 2, 1),
                       cache_b_k[l].transpose(to_cols), cache_b_v[l].transpose(to_cols),
                       bias_sa, bias_sb, rel_bias.T, sinks[l].reshape(N_HEADS_B, 1),
                       cache_a_k[l].transpose(to_cols), cache_a_v[l].transpose(to_cols))

        n_first = n_dec // 2 if (n_dec // 2) % (2 * (batch * seq // tm)) == 0 else n_dec
        x1, mix3 = _ffn_call(xp, mod2, (0, 1, 2), n_dec, seq, *ffn1, ln_g[l, 0], ln_b[l, 0], alpha, tm, fc,
                             side=(decode_side, 0, n_first))
        p_slots, p4, p16, pak, pav, pbk, pbv = _proj_prompt_call(x1, mod2, n_dec, seq, w_in_s, a_rows, b_rows, tm_proj)
        mix_a = _attn_a_call(p_slots, p4, p16, bias_a, batch, seq)
        mix_b = _attn_b_call(p_slots, bias_b, sinks[l], batch, seq)
        xp = _ffn_call(x1, mod2, (6, 7, 8), n_dec, seq, *ffn2, ln_g[l, 2], ln_b[l, 2], alpha, tm, fc,
                       mix=(mix_a, mix_b), w_out=wo, gate_k=5, ln1_g=ln_g[l, 1], ln1_b=ln_b[l, 1],
                       side=(decode_side, n_first, n_dec - n_first) if n_first < n_dec else None)
        if n_first < n_dec:
            xp, mix3_rest = xp
            mix3 = jnp.concatenate([mix3, mix3_rest], axis=0)

        mix_s = mix3.transpose(0, 2, 1).reshape(n_dec, d)
        xs = _ffn_call(s1, mod2, (6, 7, 8), 0, 1, *ffn2, ln_g[l, 2], ln_b[l, 2], alpha, n_dec, fc,
                       mix=mix_s, w_out=wo, gate_k=5, ln1_g=ln_g[l, 1], ln1_b=ln_b[l, 1])

        new = [pak.reshape(batch, a_rows, N_HEADS_A, HEAD_DIM), pav.reshape(batch, a_rows, N_HEADS_A, HEAD_DIM),
               pbk.reshape(batch, b_rows, N_KV_B, HEAD_DIM), pbv.reshape(batch, b_rows, N_KV_B, HEAD_DIM),
               sak.reshape(n_dec, 1, N_HEADS_A, HEAD_DIM), sav.reshape(n_dec, 1, N_HEADS_A, HEAD_DIM),
               sbk.reshape(n_dec, 1, N_KV_B, HEAD_DIM), sbv.reshape(n_dec, 1, N_KV_B, HEAD_DIM)]
        for acc, arr in zip(outs, new):
            acc.append(arr)

    return (xp.reshape(batch, seq, d), xs.reshape(n_dec, 1, d)) + tuple(jnp.stack(o) for o in outs)
```

```python
import functools
import math

import jax
import jax.numpy as jnp
import numpy as np
from jax import lax
from jax.experimental import pallas as pl
from jax.experimental.pallas import tpu as pltpu

HEAD_DIM = 64
N_HEADS_A = 8
N_HEADS_B = 8
N_KV_B = 2
A_PATTERNS = ((128, 1), (512, 4), (2048, 16))
B_WINDOW = 128
BLK = 128
NUM_BUCKETS = 32
MAX_DISTANCE = 2048
LN_EPS = 1e-5
NEG = -1e30
SCALE = HEAD_DIM ** -0.5

LANES = 128
SUBLANES = 8
V7X_VMEM_BYTES = 64 * 1024 * 1024
A_COLS = N_HEADS_A * HEAD_DIM
N_PAIRS = A_COLS // LANES

F32 = jnp.float32
BF16 = jnp.bfloat16


SPILL_ALLOWANCE_BYTES = 4 << 20
VMEM_RESERVED_BYTES = 6 << 20


def _vmem_limit(pipelined_bytes, scratch_bytes=0, temp_bytes=0):
    want = 2 * pipelined_bytes + scratch_bytes + temp_bytes + SPILL_ALLOWANCE_BYTES
    return int(min(want, V7X_VMEM_BYTES - VMEM_RESERVED_BYTES))


def _resident_spec(shape):
    return pl.BlockSpec(shape, lambda *_: (0,) * len(shape), pipeline_mode=pl.Buffered(1))


def _nbytes(shape, dtype):
    return math.prod(shape) * jnp.dtype(dtype).itemsize


def _layer_norm(x, g, b):
    mu = jnp.mean(x, axis=-1, keepdims=True)
    xc = x - mu
    var = jnp.mean(xc * xc, axis=-1, keepdims=True)
    return xc * lax.rsqrt(var + LN_EPS) * g + b


def _silu(x):
    return x * jax.nn.sigmoid(x)


def _seq_mod_spec(seq0, tiles_per_seq, k, d):
    return pl.BlockSpec((SUBLANES, d), lambda i: ((seq0 + i // tiles_per_seq) // SUBLANES, k))


def _seq_mod(ref, mod_row):
    if mod_row is None:
        return ref[...]
    seq0, tiles_per_seq = mod_row
    return ref[pl.ds((seq0 + pl.program_id(0) // tiles_per_seq) % SUBLANES, 1), :]


def _ada_kernel(c_ref, w_ref, b_ref, o_ref):
    a = _silu(c_ref[...]).astype(BF16)
    o_ref[...] = jnp.dot(a, w_ref[...].astype(BF16), preferred_element_type=F32) + b_ref[...]


def _ada_call(c_all, w_ada, b_ada):
    m, d = c_all.shape
    n = w_ada.shape[1]
    tn = n // 8
    assert tn % LANES == 0
    blocks = _nbytes((m, d), F32) + _nbytes((d, tn), F32) + _nbytes((m, tn), F32)
    return pl.pallas_call(
        _ada_kernel,
        grid=(n // tn,),
        in_specs=[pl.BlockSpec((m, d), lambda j: (0, 0)),
                  pl.BlockSpec((d, tn), lambda j: (0, j)),
                  pl.BlockSpec((1, tn), lambda j: (0, j))],
        out_specs=pl.BlockSpec((m, tn), lambda j: (0, j)),
        out_shape=jax.ShapeDtypeStruct((m, n), F32),
        compiler_params=pltpu.CompilerParams(
            dimension_semantics=("arbitrary",),
            vmem_limit_bytes=_vmem_limit(blocks, temp_bytes=_nbytes((d, tn), BF16))),
        name="ada_mod",
    )(c_all, w_ada, b_ada.reshape(1, n))


def _t5_bucket(n):
    max_exact = NUM_BUCKETS // 2
    nf = np.maximum(n, 1).astype(np.float32)
    large = max_exact + (np.log(nf / max_exact) / math.log(MAX_DISTANCE / max_exact)
                         * (NUM_BUCKETS - max_exact)).astype(np.int32)
    return np.where(n < max_exact, n, np.minimum(large, NUM_BUCKETS - 1))


def _bucket_maps(a_rows, b_rows):
    steps = [d for _, d in A_PATTERNS] + [1]
    q = np.arange(BLK)[:, None]
    k = np.arange(2 * BLK)[None, :]
    dist = q + BLK - k
    valid = (dist >= 0) & (dist <= BLK)
    band = np.stack([np.where(valid, _t5_bucket(np.maximum(dist, 0) * s), -1) for s in steps])

    def cached(rows, step):
        back = rows - np.arange(rows)
        hit = (back % step == 0) & (back // step <= BLK)
        return np.where(hit, _t5_bucket(back), -1)[None, :].astype(np.int32)

    cached_a = np.stack([cached(a_rows, s) for s in steps[:-1]])
    return band.astype(np.int32), cached_a, cached(b_rows, 1)


def _band_bias_kernel(rel_ref, bmap_ref, o_ref):
    head0 = jnp.where(pl.program_id(0) == len(A_PATTERNS), N_HEADS_A, 0)
    rows = 2 * SUBLANES
    for r0 in range(0, BLK, rows):
        bm = bmap_ref[r0:r0 + rows, :]
        accs = [jnp.full(bm.shape, NEG, F32)] * (2 * N_PAIRS)
        for b in range(NUM_BUCKETS):
            hit = bm == b
            accs = [jnp.where(hit, rel_ref[b, head0 + h], acc) for h, acc in enumerate(accs)]
        for h, acc in enumerate(accs):
            o_ref[h // 2, (h % 2) * BLK + r0:(h % 2) * BLK + r0 + rows, :] = acc


def _band_bias_call(rel_bias, band_map):
    n_tab = band_map.shape[0]
    return pl.pallas_call(
        _band_bias_kernel,
        grid=(n_tab,),
        in_specs=[pl.BlockSpec(memory_space=pltpu.SMEM),
                  pl.BlockSpec((None, BLK, 2 * BLK), lambda t: (t, 0, 0))],
        out_specs=pl.BlockSpec((None, N_PAIRS, 2 * BLK, 2 * BLK), lambda t: (t, 0, 0, 0)),
        out_shape=jax.ShapeDtypeStruct((n_tab, N_PAIRS, 2 * BLK, 2 * BLK), F32),
        compiler_params=pltpu.CompilerParams(dimension_semantics=("arbitrary",)),
        name="band_bias",
    )(rel_bias, band_map)


def _cached_bias_kernel(relt_ref, amap_ref, bmap_ref, oa_ref, ob_ref):
    relt = relt_ref[...]

    def table(bm, heads):
        acc = jnp.full(bm.shape, NEG, F32)
        for b in range(NUM_BUCKETS):
            acc = jnp.where(bm == b, heads[:, b:b + 1], acc)
        return acc

    for t in range(amap_ref.shape[0]):
        oa_ref[t] = table(jnp.broadcast_to(amap_ref[t], oa_ref.shape[1:]), relt[:N_HEADS_A])
    ob_ref[...] = table(jnp.broadcast_to(bmap_ref[...], ob_ref.shape), relt[N_HEADS_A:])


def _cached_bias_call(rel_bias, cached_a, cached_b):
    return pl.pallas_call(
        _cached_bias_kernel,
        out_shape=(jax.ShapeDtypeStruct((cached_a.shape[0], N_HEADS_A, cached_a.shape[2]), F32),
                   jax.ShapeDtypeStruct((N_HEADS_B, cached_b.shape[1]), F32)),
        name="cached_bias",
    )(rel_bias.T, cached_a, cached_b)


def _ffn_kernel(*refs, alpha, pre_mix, mix_slots, n_chunks, fc, n_side, sample0, mod_row):
    refs = list(refs)
    x_ref = refs.pop(0)
    if pre_mix:
        mix_refs = [refs.pop(0) for _ in range(2 if mix_slots else 1)]
        wo_ref, gm_ref, ln1g_ref, ln1b_ref = (refs.pop(0) for _ in range(4))
    sh_ref, sc_ref, gt_ref, wg_ref, wu_ref, wd_ref, lng_ref, lnb_ref = (refs.pop(0) for _ in range(8))
    if n_side:
        side_in = [refs.pop(0) for _ in range(10)]
        ak_hbm, av_hbm, o_ref, oc_ref, h_ref, acc_ref, kbuf, vbuf, sem = (refs.pop(0) for _ in range(9))
        side_scratch = refs
        end = sample0 + pl.num_programs(0) * n_side
        first = sample0 + pl.program_id(0) * n_side

        n_slots = kbuf.shape[0]

        def cache_copies(g):
            slot = lax.rem(g - sample0, n_slots)
            return (pltpu.make_async_copy(ak_hbm.at[g], kbuf.at[slot], sem.at[0, slot]),
                    pltpu.make_async_copy(av_hbm.at[g], vbuf.at[slot], sem.at[1, slot]))

        @pl.when(pl.program_id(0) == 0)
        def _():
            for ahead in range(n_slots - 1):
                for cp in cache_copies(sample0 + ahead):
                    cp.start()

        def side(j):
            g = first + j
            for cp in cache_copies(g):
                cp.wait()

            @pl.when(g + n_slots - 1 < end)
            def _():
                for cp in cache_copies(g + n_slots - 1):
                    cp.start()

            slot = lax.rem(g - sample0, n_slots)
            _cached_sample(j, kbuf.at[slot], vbuf.at[slot], *side_in, oc_ref, *side_scratch)

        side_at = {(j * n_chunks) // n_side if j else -1: j for j in range(n_side)}
    else:
        o_ref, h_ref, acc_ref = refs
        side_at = {}

    if -1 in side_at:
        side(side_at[-1])
    x = x_ref[...]
    if pre_mix:
        if mix_slots:
            mixed = jnp.concatenate([r[j] for r in mix_refs for j in range(N_PAIRS)], axis=-1)
        else:
            mixed = mix_refs[0][...].astype(BF16)
        y = jnp.dot(mixed, wo_ref[...], preferred_element_type=F32)
        x = _layer_norm(alpha * x + _seq_mod(gm_ref, mod_row) * y, ln1g_ref[...], ln1b_ref[...])
    h_ref[...] = (x * (1.0 + _seq_mod(sc_ref, mod_row)) + _seq_mod(sh_ref, mod_row)).astype(BF16)
    for c in range(n_chunks):
        if c in side_at:
            side(side_at[c])
        cols = slice(c * fc, (c + 1) * fc)
        g = jnp.dot(h_ref[...], wg_ref[:, cols], preferred_element_type=F32)
        u = jnp.dot(h_ref[...], wu_ref[:, cols], preferred_element_type=F32)
        y = jnp.dot((_silu(g) * u).astype(BF16), wd_ref[cols, :], preferred_element_type=F32)
        if c == 0:
            acc_ref[...] = y
        else:
            acc_ref[...] += y
    o_ref[...] = _layer_norm(alpha * x + 0.5 * _seq_mod(gt_ref, mod_row) * acc_ref[...], lng_ref[...], lnb_ref[...])


def _ffn_call(x, mod2, mod_ks, seq0, rows_per_seq, wg, wu, wd, ln_g, ln_b, alpha, tm, fc,
              mix=None, w_out=None, gate_k=None, ln1_g=None, ln1_b=None, side=None):
    n, d = x.shape
    d_ff = wg.shape[1]
    n_chunks = d_ff // fc
    assert n_chunks * fc == d_ff
    per_row = rows_per_seq == 1
    tiles_per_seq = max(rows_per_seq // tm, 1)

    def mod_spec(k):
        if per_row:
            return pl.BlockSpec((tm, d), lambda i, k=k: (seq0 // tm + i, k))
        return _seq_mod_spec(seq0, tiles_per_seq, k, d)

    mod_arr = mod2
    row_spec = pl.BlockSpec((tm, d), lambda i: (i, 0))
    vec_spec = pl.BlockSpec((1, d), lambda i: (0, 0))
    args, specs = [x], [row_spec]
    pre_mix = mix is not None
    mix_slots = pre_mix and isinstance(mix, tuple)
    blocks = 2 * _nbytes((tm, d), F32) + (3 * _nbytes((tm, d), F32) if per_row else 0)
    resident = 3 * _nbytes(wg.shape, BF16)
    if pre_mix:
        if mix_slots:
            for mm in mix:
                args.append(mm)
                specs.append(pl.BlockSpec((N_PAIRS, tm, LANES), lambda i: (0, i, 0)))
        else:
            args.append(mix)
            specs.append(row_spec)
        args += [w_out, mod_arr, ln1_g.reshape(1, d), ln1_b.reshape(1, d)]
        specs += [_resident_spec((d, d)), mod_spec(gate_k), vec_spec, vec_spec]
        blocks += _nbytes((tm, d), F32)
        resident += _nbytes((d, d), BF16)
    args += [mod_arr, mod_arr, mod_arr, wg, wu, wd, ln_g.reshape(1, d), ln_b.reshape(1, d)]
    specs += [mod_spec(mod_ks[0]), mod_spec(mod_ks[1]), mod_spec(mod_ks[2]),
              _resident_spec(wg.shape), _resident_spec(wu.shape), _resident_spec(wd.shape), vec_spec, vec_spec]
    scratch = _nbytes((tm, d), BF16) + _nbytes((tm, d), F32) + resident
    temps = 6 * _nbytes((tm, fc), F32) + 3 * _nbytes((tm, d), F32)
    out_specs, out_shape = row_spec, jax.ShapeDtypeStruct((n, d), F32)
    scratch_shapes = [pltpu.VMEM((tm, d), BF16), pltpu.VMEM((tm, d), F32)]
    n_side = sample0 = 0
    if side is not None:
        (*small, ak, av), sample0, count = side
        n_steps = n // tm
        n_side = count // n_steps
        assert n_side * n_steps == count and sample0 % n_side == 0
        blk0 = sample0 // n_side
        for a in small[:6]:
            args.append(a)
            specs.append(pl.BlockSpec((n_side,) + a.shape[1:],
                                      lambda i, nd=a.ndim: (blk0 + i,) + (0,) * (nd - 1)))
            blocks += n_side * _nbytes(a.shape[1:-1] + (max(a.shape[-1], LANES),), F32)
        for a in small[6:]:
            args.append(a)
            specs.append(pl.BlockSpec(a.shape, lambda i, nd=a.ndim: (0,) * nd))
            blocks += _nbytes(a.shape, F32)
        args += [ak, av]
        specs += [pl.BlockSpec(memory_space=pl.ANY)] * 2
        q3 = small[0]
        out_specs = [row_spec, pl.BlockSpec((n_side,) + q3.shape[1:], lambda i: (i, 0, 0))]
        out_shape = [out_shape, jax.ShapeDtypeStruct((count,) + q3.shape[1:], F32)]
        ring = 2 if pre_mix else 3
        assert count >= ring - 1
        sa_shape = (n_side, N_HEADS_A, ak.shape[3])
        sb_shape = (n_side, N_HEADS_B, small[4].shape[3])
        scratch_shapes += [pltpu.VMEM((ring,) + ak.shape[1:], F32), pltpu.VMEM((ring,) + av.shape[1:], F32),
                           pltpu.SemaphoreType.DMA((2, ring)),
                           pltpu.VMEM(sa_shape, F32), pltpu.VMEM(sb_shape, F32),
                           pltpu.VMEM(sa_shape, F32), pltpu.VMEM(sb_shape, F32)]
        scratch += 2 * ring * _nbytes(ak.shape[1:], F32) + 2 * _nbytes(sa_shape, F32) + 2 * _nbytes(sb_shape, F32)
    return pl.pallas_call(
        functools.partial(_ffn_kernel, alpha=alpha, pre_mix=pre_mix, mix_slots=mix_slots,
                          n_chunks=n_chunks, fc=fc, n_side=n_side, sample0=sample0,
                          mod_row=None if per_row else (seq0, tiles_per_seq)),
        grid=(n // tm,),
        in_specs=specs,
        out_specs=out_specs,
        out_shape=out_shape,
        scratch_shapes=scratch_shapes,
        compiler_params=pltpu.CompilerParams(
            dimension_semantics=("arbitrary",),
            vmem_limit_bytes=_vmem_limit(blocks, scratch, temps)),
        name=("ffn_mix" if pre_mix else "ffn") + ("_cached" if n_side else ""),
    )(*args)


def _proj_prompt_kernel(x_ref, sh_ref, sc_ref, w_ref, p_ref, p4_ref, p16_ref, ak_ref, av_ref, bk_ref, bv_ref,
                        stage_ref, stage4_ref, *, tm, b_rows, mod_row):
    h = (x_ref[...] * (1.0 + _seq_mod(sc_ref, mod_row)) + _seq_mod(sh_ref, mod_row)).astype(BF16)
    lane_lo = lax.broadcasted_iota(jnp.int32, (tm, LANES), 1) < HEAD_DIM
    for g in range(4):
        res = jnp.dot(h, w_ref[:, g * A_COLS:(g + 1) * A_COLS], preferred_element_type=F32)
        scaled = res * SCALE if g in (0, 3) else res
        for s in range(N_PAIRS):
            slab = scaled[:, s * LANES:(s + 1) * LANES]
            p_ref[g * N_PAIRS + s] = slab.astype(BF16)
            if g < 3:
                slot = g * N_PAIRS + s
                d4, d16 = p4_ref.shape[1], p16_ref.shape[1]
                stage_ref[slot] = slab
                for r in range(d4):
                    rows = stage_ref[slot, pl.ds(r, tm // d4, stride=d4), :]
                    p4_ref[slot, r] = rows.astype(BF16)
                    stage4_ref[slot, r] = rows
                for r in range(d16):
                    rows = stage4_ref[slot, r % d4, pl.ds(r // d4, tm // d16, stride=d16 // d4), :]
                    p16_ref[slot, r] = rows.astype(BF16)
        if g in (1, 2):
            out_ref = ak_ref if g == 1 else av_ref
            out_ref[...] = res
    res = jnp.dot(h, w_ref[:, 4 * A_COLS:], preferred_element_type=F32)
    for j, out_ref in enumerate((bk_ref, bv_ref)):
        slab = res[:, j * LANES:(j + 1) * LANES]
        swapped = pltpu.roll(slab, HEAD_DIM, axis=1)
        p_ref[4 * N_PAIRS + N_KV_B * j] = jnp.where(lane_lo, slab, swapped).astype(BF16)
        p_ref[4 * N_PAIRS + N_KV_B * j + 1] = jnp.where(lane_lo, swapped, slab).astype(BF16)

        out_ref[...] = slab[tm - b_rows:, :]


def _proj_prompt_call(x, mod2, seq0, seq, w_ext, a_rows, b_rows, tm):
    n, d = x.shape
    batch = n // seq
    tiles_per_seq = seq // tm
    a_tiles = a_rows // tm
    n_slots = 4 * N_PAIRS + 2 * N_KV_B
    assert w_ext.shape[1] == 4 * A_COLS + 2 * LANES and N_KV_B * HEAD_DIM == LANES

    def mod_spec(k):
        return _seq_mod_spec(seq0, tiles_per_seq, k, d)

    def a_map(i):
        return (i // tiles_per_seq, jnp.maximum(i % tiles_per_seq - (tiles_per_seq - a_tiles), 0), 0)

    a_blk = (None, tm, A_COLS)
    b_blk = (None, b_rows, LANES)
    a_slots = 3 * N_PAIRS
    d4, d16 = A_PATTERNS[1][1], A_PATTERNS[2][1]
    assert tm % (16 * d16) == 0

    def dil_spec(dil):
        return pl.BlockSpec((a_slots, None, dil, tm // dil, LANES),
                            lambda i: (0, i // tiles_per_seq, 0, i % tiles_per_seq, 0))

    blocks = (_nbytes((tm, d), F32) + _nbytes(w_ext.shape, BF16) + _nbytes((n_slots + 2 * a_slots, tm, LANES), BF16)
              + 2 * _nbytes((tm, A_COLS), F32) + 2 * _nbytes((b_rows, LANES), F32))
    return pl.pallas_call(
        functools.partial(_proj_prompt_kernel, tm=tm, b_rows=b_rows, mod_row=(seq0, tiles_per_seq)),
        grid=(n // tm,),
        in_specs=[pl.BlockSpec((tm, d), lambda i: (i, 0)), mod_spec(3), mod_spec(4),
                  pl.BlockSpec(w_ext.shape, lambda i: (0, 0))],
        out_specs=[pl.BlockSpec((n_slots, tm, LANES), lambda i: (0, i, 0)),
                   dil_spec(d4), dil_spec(d16),
                   pl.BlockSpec(a_blk, a_map),
                   pl.BlockSpec(a_blk, a_map),
                   pl.BlockSpec(b_blk, lambda i: (i // tiles_per_seq, 0, 0)),
                   pl.BlockSpec(b_blk, lambda i: (i // tiles_per_seq, 0, 0))],
        out_shape=[jax.ShapeDtypeStruct((n_slots, n, LANES), BF16),
                   jax.ShapeDtypeStruct((a_slots, batch, d4, seq // d4, LANES), BF16),
                   jax.ShapeDtypeStruct((a_slots, batch, d16, seq // d16, LANES), BF16),
                   jax.ShapeDtypeStruct((batch, a_rows, A_COLS), F32),
                   jax.ShapeDtypeStruct((batch, a_rows, A_COLS), F32),
                   jax.ShapeDtypeStruct((batch, b_rows, LANES), F32),
                   jax.ShapeDtypeStruct((batch, b_rows, LANES), F32)],
        scratch_shapes=[pltpu.VMEM((a_slots, tm, LANES), F32), pltpu.VMEM((a_slots, d4, tm // d4, LANES), F32)],
        compiler_params=pltpu.CompilerParams(
            dimension_semantics=("arbitrary",),
            vmem_limit_bytes=_vmem_limit(blocks, 2 * _nbytes((a_slots, tm, LANES), F32),
                                         4 * _nbytes((tm, A_COLS), F32))),
        name="proj_prompt",
    )(x, mod2, mod2, w_ext)


def _proj_sample_kernel(x_ref, sh_ref, sc_ref, w_ref, o_ref):
    h = (x_ref[...] * (1.0 + sc_ref[...]) + sh_ref[...]).astype(BF16)
    o_ref[...] = jnp.dot(h, w_ref[...], preferred_element_type=F32)


def _proj_sample_call(x, mod2, w_ext):
    n, d = x.shape
    cols = w_ext.shape[1]
    blocks = 3 * _nbytes((n, d), F32) + _nbytes(w_ext.shape, BF16) + _nbytes((n, cols), F32)
    return pl.pallas_call(
        _proj_sample_kernel,
        grid=(1,),
        in_specs=[pl.BlockSpec((n, d), lambda i: (0, 0)),
                  pl.BlockSpec((n, d), lambda i: (0, 3)),
                  pl.BlockSpec((n, d), lambda i: (0, 4)),
                  pl.BlockSpec(w_ext.shape, lambda i: (0, 0))],
        out_specs=pl.BlockSpec((n, cols), lambda i: (0, 0)),
        out_shape=jax.ShapeDtypeStruct((n, cols), F32),
        compiler_params=pltpu.CompilerParams(
            dimension_semantics=("arbitrary",), vmem_limit_bytes=_vmem_limit(blocks)),
        name="proj_sample",
    )(x, mod2, mod2, w_ext)


def _pair_scores(q, kk, bias, lane_lo):
    zero = jnp.zeros_like(q)
    q2 = jnp.concatenate([jnp.where(lane_lo, q, zero), jnp.where(lane_lo, zero, q)], axis=0)
    s = lax.dot_general(q2, kk, (((1,), (1,)), ((), ())), preferred_element_type=F32)
    return s + bias


def _pair_select(a, lane_lo):
    return jnp.where(lane_lo, a[:BLK], a[BLK:])


GROUP = 16


def _band_sequence(q_ref, k_ref, v_ref, n_blocks, load_bias, block):
    def one(jb, first):
        if first:
            block(jb, q_ref[0:BLK, :], k_ref[0:BLK, :], v_ref[0:BLK, :], load_bias(True))
        else:
            r0 = jb * BLK
            if not isinstance(jb, int):
                r0 = pl.multiple_of(r0, BLK)
            block(jb, q_ref[pl.ds(r0, BLK), :], k_ref[pl.ds(r0 - BLK, 2 * BLK), :],
                  v_ref[pl.ds(r0 - BLK, 2 * BLK), :], load_bias(False))

    for jb in range(min(GROUP, n_blocks)):
        one(jb, jb == 0)
    if n_blocks > GROUP:
        assert n_blocks % GROUP == 0

        def group(g, carry):
            for j in range(GROUP):
                one(g * GROUP + j, False)
            return carry

        lax.fori_loop(1, n_blocks // GROUP, group, 0)


def _attn_a_kernel(q16_ref, k16_ref, v16_ref, q4_ref, k4_ref, v4_ref, q1_ref, k1_ref, v1_ref,
                   bias_ref, o_ref, acc16_ref, m16_ref, den16_ref, acc4_ref, m4_ref, den4_ref, *, seq):
    lane_lo = lax.broadcasted_iota(jnp.int32, (BLK, LANES), 1) < HEAD_DIM
    state16 = (acc16_ref, m16_ref, den16_ref)
    state4 = (acc4_ref, m4_ref, den4_ref)

    def update(q, kk, vv, bias, rows, old, new):
        s = _pair_scores(q, kk, bias, lane_lo)
        part = s[:, :LANES] if s.shape[1] == LANES else jnp.maximum(s[:, :LANES], s[:, LANES:])
        if old is not None:
            acc_old, m_old, den_old = (ref[rows, :] for ref in old)
            part = jnp.maximum(part, jnp.concatenate([jnp.where(lane_lo, m_old, NEG),
                                                      jnp.where(lane_lo, NEG, m_old)], axis=0))
        m2 = jnp.max(part, axis=-1, keepdims=True)
        p = jnp.exp(s - m2)
        l2 = jnp.sum(p, axis=-1, keepdims=True)
        pv = _pair_select(jnp.dot(p.astype(BF16), vv, preferred_element_type=F32), lane_lo)
        m_new = _pair_select(m2, lane_lo)
        den = _pair_select(l2, lane_lo)
        if old is None:
            acc = pv
        else:
            a = jnp.exp(m_old - m_new)
            acc = a * acc_old + pv
            den = a * den_old + den
        if new is None:
            o_ref[rows, :] = (acc / den).astype(o_ref.dtype)
        else:
            new[0][rows, :] = acc
            new[1][rows, :] = m_new
            new[2][rows, :] = den

    def branch(q_ref, k_ref, v_ref, table, dil, r, old, new):
        def block(jb, q, kk, vv, bias):
            start = r + jb * (BLK * dil)
            rows = pl.ds(start, BLK, stride=dil) if dil > 1 else pl.ds(start, BLK)
            update(q, kk, vv, bias, rows, old, new)

        def load_bias(first):
            return bias_ref[table, :, BLK:2 * BLK] if first else bias_ref[table]

        _band_sequence(q_ref, k_ref, v_ref, q_ref.shape[0] // BLK, load_bias, block)

    def dilated(q_ref, k_ref, v_ref, table, old, new):
        dil = q_ref.shape[0]
        per_group = min(max(GROUP // (q_ref.shape[1] // BLK), 1), dil)
        assert dil % per_group == 0

        def group(g, carry):
            for rr in range(per_group):
                r = g * per_group + rr
                branch(q_ref.at[r], k_ref.at[r], v_ref.at[r], table, dil, r, old, new)
            return carry

        lax.fori_loop(0, dil // per_group, group, 0)

    dilated(q16_ref, k16_ref, v16_ref, 2, None, state16)
    dilated(q4_ref, k4_ref, v4_ref, 1, state16, state4)
    branch(q1_ref, k1_ref, v1_ref, 0, 1, 0, state4, None)


def _attn_a_call(p_slots, p4, p16, bias_a, batch, seq):
    n_slots = p_slots.shape[0]
    dils = [d for _, d in A_PATTERNS]
    assert dils == [1, p4.shape[2], p16.shape[2]] and seq % (dils[2] * BLK) == 0
    view1 = p_slots.reshape(n_slots, batch, seq, LANES)

    in_specs, args = [], []
    for arr in (p16, p4, view1):
        for slot0 in (0, N_PAIRS, 2 * N_PAIRS):
            in_specs.append(pl.BlockSpec((None, None) + arr.shape[2:],
                                         lambda b, hp, slot0=slot0, nd=arr.ndim: (slot0 + hp, b) + (0,) * (nd - 2)))
            args.append(arr)
    in_specs.append(pl.BlockSpec((len(A_PATTERNS), None, 2 * BLK, 2 * BLK), lambda b, hp: (0, hp, 0, 0)))
    args.append(bias_a)
    blocks = 10 * _nbytes((seq, LANES), BF16) + _nbytes((3, 2 * BLK, 2 * BLK), F32)
    scratch = 6 * _nbytes((seq, LANES), F32)
    return pl.pallas_call(
        functools.partial(_attn_a_kernel, seq=seq),
        grid=(batch, N_PAIRS),
        in_specs=in_specs,
        out_specs=pl.BlockSpec((None, seq, LANES), lambda b, hp: (hp, b, 0)),
        out_shape=jax.ShapeDtypeStruct((N_PAIRS, batch * seq, LANES), BF16),
        scratch_shapes=[pltpu.VMEM((seq, LANES), F32)] * 6,
        compiler_params=pltpu.CompilerParams(
            dimension_semantics=("arbitrary", "arbitrary"),
            vmem_limit_bytes=_vmem_limit(blocks, scratch, 4 * GROUP * _nbytes((2 * BLK, 2 * BLK), F32))),
        name="attn_dilated",
    )(*args)


def _attn_b_kernel(sink_ref, q_ref, k_ref, v_ref, bias_ref, o_ref, *, seq):
    hp = pl.program_id(1)
    lane_lo = lax.broadcasted_iota(jnp.int32, (BLK, LANES), 1) < HEAD_DIM
    row_lo = lax.broadcasted_iota(jnp.int32, (2 * BLK, 1), 0) < BLK
    sink2 = jnp.where(row_lo, sink_ref[2 * hp], sink_ref[2 * hp + 1])

    def block(jb, q, kk, vv, bias):
        s = _pair_scores(q, kk, bias, lane_lo)
        m2 = jnp.maximum(jnp.max(s, axis=-1, keepdims=True), sink2)
        p = jnp.exp(s - m2)
        l2 = jnp.sum(p, axis=-1, keepdims=True) + jnp.exp(sink2 - m2)
        pv = _pair_select(jnp.dot(p.astype(BF16), vv, preferred_element_type=F32), lane_lo)
        r0 = jb * BLK if isinstance(jb, int) else pl.multiple_of(jb * BLK, BLK)
        o_ref[pl.ds(r0, BLK), :] = (pv / _pair_select(l2, lane_lo)).astype(o_ref.dtype)

    def load_bias(first):
        return bias_ref[:, BLK:2 * BLK] if first else bias_ref[...]

    _band_sequence(q_ref, k_ref, v_ref, seq // BLK, load_bias, block)


def _attn_b_call(p_slots, bias_b, sinks, batch, seq):
    n_slots = p_slots.shape[0]
    view = p_slots.reshape(n_slots, batch, seq, LANES)
    q0, k0, v0 = 3 * N_PAIRS, 4 * N_PAIRS, 4 * N_PAIRS + N_KV_B
    pairs_per_kv = N_PAIRS // N_KV_B
    blk = (None, None, seq, LANES)
    blocks = 4 * _nbytes((seq, LANES), BF16) + _nbytes((2 * BLK, 2 * BLK), F32)
    return pl.pallas_call(
        functools.partial(_attn_b_kernel, seq=seq),
        grid=(batch, N_PAIRS),
        in_specs=[pl.BlockSpec(memory_space=pltpu.SMEM),
                  pl.BlockSpec(blk, lambda b, hp: (q0 + hp, b, 0, 0)),
                  pl.BlockSpec(blk, lambda b, hp: (k0 + hp // pairs_per_kv, b, 0, 0)),
                  pl.BlockSpec(blk, lambda b, hp: (v0 + hp // pairs_per_kv, b, 0, 0)),
                  pl.BlockSpec((None, 2 * BLK, 2 * BLK), lambda b, hp: (hp, 0, 0))],
        out_specs=pl.BlockSpec((None, seq, LANES), lambda b, hp: (hp, b, 0)),
        out_shape=jax.ShapeDtypeStruct((N_PAIRS, batch * seq, LANES), BF16),
        compiler_params=pltpu.CompilerParams(
            dimension_semantics=("arbitrary", "arbitrary"),
            vmem_limit_bytes=_vmem_limit(blocks, temp_bytes=8 * _nbytes((2 * BLK, 2 * BLK), F32))),
        name="attn_window",
    )(sinks, view, view, view, bias_b)


def _cached_sample(i, ak_ref, av_ref, q_ref, qt_ref, knt_ref, vn_ref, bk_ref, bv_ref, bias_a_ref, bias_b_ref,
                   relt_ref, sink_ref, o_ref, sa_ref, sb_ref, wa_ref, wb_ref):
    n_branches = bias_a_ref.shape[0]
    heads_per_kv = N_HEADS_B // N_KV_B
    bias0 = relt_ref[:, 0:1]
    sink = sink_ref[...]
    q = q_ref[i] * SCALE
    vn = vn_ref[i]
    s0 = jnp.sum(qt_ref[i] * SCALE * knt_ref[i], axis=-1, keepdims=True) + bias0
    for h in range(N_HEADS_A):
        sa_ref[i, h:h + 1, :] = jnp.sum(ak_ref[h] * q[:, h:h + 1], axis=0, keepdims=True)
    for hb in range(N_HEADS_B):
        h = N_HEADS_A + hb
        sb_ref[i, hb:hb + 1, :] = jnp.sum(bk_ref[i, hb // heads_per_kv] * q[:, h:h + 1], axis=0, keepdims=True)

    s0a = s0[:N_HEADS_A]
    ts = [sa_ref[i] + bias_a_ref[t] for t in range(n_branches)]
    m = s0a
    for t_ in ts:
        m = jnp.maximum(m, jnp.max(t_, axis=-1, keepdims=True))
    w = jnp.exp(ts[0] - m)
    for t_ in ts[1:]:
        w = w + jnp.exp(t_ - m)
    p0a = n_branches * jnp.exp(s0a - m)
    inv = 1.0 / (jnp.sum(w, axis=-1, keepdims=True) + p0a)
    wa_ref[i] = w * inv
    p0a = p0a * inv

    s0b = s0[N_HEADS_A:]
    tb = sb_ref[i] + bias_b_ref[...]
    mb = jnp.maximum(jnp.maximum(s0b, sink), jnp.max(tb, axis=-1, keepdims=True))
    wb = jnp.exp(tb - mb)
    p0b = jnp.exp(s0b - mb)
    invb = 1.0 / (jnp.sum(wb, axis=-1, keepdims=True) + p0b + jnp.exp(sink - mb))
    wb_ref[i] = wb * invb
    p0b = p0b * invb

    cols = []
    for h in range(N_HEADS_A):
        acc = jnp.sum(av_ref[h] * wa_ref[i, h:h + 1, :], axis=-1, keepdims=True)
        cols.append(acc + p0a[h:h + 1, :] * vn[:, h:h + 1])
    for hb in range(N_HEADS_B):
        h = N_HEADS_A + hb
        acc = jnp.sum(bv_ref[i, hb // heads_per_kv] * wb_ref[i, hb:hb + 1, :], axis=-1, keepdims=True)
        cols.append(acc + p0b[hb:hb + 1, :] * vn[:, h:h + 1])
    o_ref[i] = jnp.concatenate(cols, axis=1)


def kernel(x_prompt, x_sample, cache_a_k, cache_a_v, cache_b_k, cache_b_v, c_prompt, c_sample, rel_bias, w_ada, b_ada, ffn1_wg, ffn1_wu, ffn1_wd, w_in, w_out, sinks, ffn2_wg, ffn2_wu, ffn2_wd, ln_g, ln_b):
    batch, seq, d = x_prompt.shape
    n_dec, dec_seq, _ = x_sample.shape
    depth = w_ada.shape[0]
    assert dec_seq == 1 and d == 2 * A_COLS and all(w // dl == BLK for w, dl in A_PATTERNS)
    alpha = (2 * depth) ** 0.25
    fc = 256
    tm = 512
    tm_proj = 1024 if seq % 1024 == 0 else tm
    a_rows, b_rows = min(A_PATTERNS[-1][0], seq), min(B_WINDOW, seq)
    q_cols = 4 * A_COLS
    kvb = N_KV_B * HEAD_DIM
    heads_per_kv = N_HEADS_B // N_KV_B

    band_map, cached_a, cached_b = _bucket_maps(cache_a_k.shape[2], cache_b_k.shape[2])
    band_bias = _band_bias_call(rel_bias, band_map)
    bias_a, bias_b = band_bias[:len(A_PATTERNS)], band_bias[len(A_PATTERNS)]
    bias_sa, bias_sb = _cached_bias_call(rel_bias, cached_a, cached_b)

    pad = (-(n_dec + batch)) % 16
    c_all = jnp.concatenate([c_sample, c_prompt, jnp.zeros((pad, d), F32)], axis=0)
    xp = x_prompt.reshape(batch * seq, d)
    xs = x_sample.reshape(n_dec, d)
    outs = [[] for _ in range(8)]

    def head_rows(x, n_heads):
        return x.reshape(n_dec, n_heads, HEAD_DIM)

    for l in range(depth):
        mod2 = _ada_call(c_all, w_ada[l], b_ada[l])
        ffn1 = [w[l].astype(BF16) for w in (ffn1_wg, ffn1_wu, ffn1_wd)]
        ffn2 = [w[l].astype(BF16) for w in (ffn2_wg, ffn2_wu, ffn2_wd)]
        w_in_s = w_in[l].astype(BF16)
        wo = w_out[l].astype(BF16)

        s1 = _ffn_call(xs, mod2, (0, 1, 2), 0, 1, *ffn1, ln_g[l, 0], ln_b[l, 0], alpha, n_dec, fc)
        proj_s = _proj_sample_call(s1, mod2, w_in_s)
        c = A_COLS
        sak, sav = proj_s[:, c:2 * c], proj_s[:, 2 * c:3 * c]
        sbk, sbv = proj_s[:, q_cols:q_cols + kvb], proj_s[:, q_cols + kvb:q_cols + 2 * kvb]
        qt3 = jnp.concatenate([head_rows(proj_s[:, :c], N_HEADS_A),
                               head_rows(proj_s[:, 3 * c:q_cols], N_HEADS_B)], axis=1)
        knt3 = jnp.concatenate([head_rows(sak, N_HEADS_A),
                                jnp.repeat(head_rows(sbk, N_KV_B), heads_per_kv, axis=1)], axis=1)
        vnt3 = jnp.concatenate([head_rows(sav, N_HEADS_A),
                                jnp.repeat(head_rows(sbv, N_KV_B), heads_per_kv, axis=1)], axis=1)
        to_cols = (0, 2, 3, 1)
        decode_side = (qt3.transpose(0, 2, 1), qt3, knt3, vnt3.transpose(0, 2, 1),
                       cache_b_k[l].transpose(to_cols), cache_b_v[l].transpose(to_cols),
                       bias_sa, bias_sb, rel_bias.T, sinks[l].reshape(N_HEADS_B, 1),
                       cache_a_k[l].transpose(to_cols), cache_a_v[l].transpose(to_cols))

        n_first = n_dec // 2 if (n_dec // 2) % (2 * (batch * seq // tm)) == 0 else n_dec
        x1, mix3 = _ffn_call(xp, mod2, (0, 1, 2), n_dec, seq, *ffn1, ln_g[l, 0], ln_b[l, 0], alpha, tm, fc,
                             side=(decode_side, 0, n_first))
        p_slots, p4, p16, pak, pav, pbk, pbv = _proj_prompt_call(x1, mod2, n_dec, seq, w_in_s, a_rows, b_rows, tm_proj)
        mix_a = _attn_a_call(p_slots, p4, p16, bias_a, batch, seq)
        mix_b = _attn_b_call(p_slots, bias_b, sinks[l], batch, seq)
        xp = _ffn_call(x1, mod2, (6, 7, 8), n_dec, seq, *ffn2, ln_g[l, 2], ln_b[l, 2], alpha, tm, fc,
                       mix=(mix_a, mix_b), w_out=wo, gate_k=5, ln1_g=ln_g[l, 1], ln1_b=ln_b[l, 1],
                       side=(decode_side, n_first, n_dec - n_first) if n_first < n_dec else None)
        if n_first < n_dec:
            xp, mix3_rest = xp
            mix3 = jnp.concatenate([mix3, mix3_rest], axis=0)

        mix_s = mix3.transpose(0, 2, 1).reshape(n_dec, d)
        xs = _ffn_call(s1, mod2, (6, 7, 8), 0, 1, *ffn2, ln_g[l, 2], ln_b[l, 2], alpha, n_dec, fc,
                       mix=mix_s, w_out=wo, gate_k=5, ln1_g=ln_g[l, 1], ln1_b=ln_b[l, 1])

        new = [pak.reshape(batch, a_rows, N_HEADS_A, HEAD_DIM), pav.reshape(batch, a_rows, N_HEADS_A, HEAD_DIM),
               pbk.reshape(batch, b_rows, N_KV_B, HEAD_DIM), pbv.reshape(batch, b_rows, N_KV_B, HEAD_DIM),
               sak.reshape(n_dec, 1, N_HEADS_A, HEAD_DIM), sav.reshape(n_dec, 1, N_HEADS_A, HEAD_DIM),
               sbk.reshape(n_dec, 1, N_KV_B, HEAD_DIM), sbv.reshape(n_dec, 1, N_KV_B, HEAD_DIM)]
        for acc, arr in zip(outs, new):
            acc.append(arr)

    return (xp.reshape(batch, seq, d), xs.reshape(n_dec, 1, d)) + tuple(jnp.stack(o) for o in outs)
```

```python
import functools
import math

import jax
import jax.numpy as jnp
import numpy as np
from jax import lax
from jax.experimental import pallas as pl
from jax.experimental.pallas import tpu as pltpu

HEAD_DIM = 64
N_HEADS_A = 8
N_HEADS_B = 8
N_KV_B = 2
A_PATTERNS = ((128, 1), (512, 4), (2048, 16))
B_WINDOW = 128
BLK = 128
NUM_BUCKETS = 32
MAX_DISTANCE = 2048
LN_EPS = 1e-5
NEG = -1e30
SCALE = HEAD_DIM ** -0.5

LANES = 128
SUBLANES = 8
V7X_VMEM_BYTES = 64 * 1024 * 1024
A_COLS = N_HEADS_A * HEAD_DIM
N_PAIRS = A_COLS // LANES

F32 = jnp.float32
BF16 = jnp.bfloat16


SPILL_ALLOWANCE_BYTES = 4 << 20
VMEM_RESERVED_BYTES = 6 << 20


def _vmem_limit(pipelined_bytes, scratch_bytes=0, temp_bytes=0):
    want = 2 * pipelined_bytes + scratch_bytes + temp_bytes + SPILL_ALLOWANCE_BYTES
    return int(min(want, V7X_VMEM_BYTES - VMEM_RESERVED_BYTES))


def _resident_spec(shape):
    return pl.BlockSpec(shape, lambda *_: (0,) * len(shape), pipeline_mode=pl.Buffered(1))


def _nbytes(shape, dtype):
    return math.prod(shape) * jnp.dtype(dtype).itemsize


def _layer_norm(x, g, b):
    mu = jnp.mean(x, axis=-1, keepdims=True)
    xc = x - mu
    var = jnp.mean(xc * xc, axis=-1, keepdims=True)
    return xc * lax.rsqrt(var + LN_EPS) * g + b


def _silu(x):
    return x * jax.nn.sigmoid(x)


def _seq_mod_spec(seq0, tiles_per_seq, k, d):
    return pl.BlockSpec((SUBLANES, d), lambda i: ((seq0 + i // tiles_per_seq) // SUBLANES, k))


def _seq_mod(ref, mod_row):
    if mod_row is None:
        return ref[...]
    seq0, tiles_per_seq = mod_row
    return ref[pl.ds((seq0 + pl.program_id(0) // tiles_per_seq) % SUBLANES, 1), :]


def _ada_kernel(c_ref, w_ref, b_ref, o_ref):
    a = _silu(c_ref[...]).astype(BF16)
    o_ref[...] = jnp.dot(a, w_ref[...].astype(BF16), preferred_element_type=F32) + b_ref[...]


def _ada_call(c_all, w_ada, b_ada):
    m, d = c_all.shape
    n = w_ada.shape[1]
    tn = n // 8
    assert tn % LANES == 0
    blocks = _nbytes((m, d), F32) + _nbytes((d, tn), F32) + _nbytes((m, tn), F32)
    return pl.pallas_call(
        _ada_kernel,
        grid=(n // tn,),
        in_specs=[pl.BlockSpec((m, d), lambda j: (0, 0)),
                  pl.BlockSpec((d, tn), lambda j: (0, j)),
                  pl.BlockSpec((1, tn), lambda j: (0, j))],
        out_specs=pl.BlockSpec((m, tn), lambda j: (0, j)),
        out_shape=jax.ShapeDtypeStruct((m, n), F32),
        compiler_params=pltpu.CompilerParams(
            dimension_semantics=("arbitrary",),
            vmem_limit_bytes=_vmem_limit(blocks, temp_bytes=_nbytes((d, tn), BF16))),
        name="ada_mod",
    )(c_all, w_ada, b_ada.reshape(1, n))


def _t5_bucket(n):
    max_exact = NUM_BUCKETS // 2
    nf = np.maximum(n, 1).astype(np.float32)
    large = max_exact + (np.log(nf / max_exact) / math.log(MAX_DISTANCE / max_exact)
                         * (NUM_BUCKETS - max_exact)).astype(np.int32)
    return np.where(n < max_exact, n, np.minimum(large, NUM_BUCKETS - 1))


def _bucket_maps(a_rows, b_rows):
    steps = [d for _, d in A_PATTERNS] + [1]
    q = np.arange(BLK)[:, None]
    k = np.arange(2 * BLK)[None, :]
    dist = q + BLK - k
    valid = (dist >= 0) & (dist <= BLK)
    band = np.stack([np.where(valid, _t5_bucket(np.maximum(dist, 0) * s), -1) for s in steps])

    def cached(rows, step):
        back = rows - np.arange(rows)
        hit = (back % step == 0) & (back // step <= BLK)
        return np.where(hit, _t5_bucket(back), -1)[None, :].astype(np.int32)

    cached_a = np.stack([cached(a_rows, s) for s in steps[:-1]])
    return band.astype(np.int32), cached_a, cached(b_rows, 1)


def _band_bias_kernel(rel_ref, bmap_ref, o_ref):
    head0 = jnp.where(pl.program_id(0) == len(A_PATTERNS), N_HEADS_A, 0)
    rows = 2 * SUBLANES
    for r0 in range(0, BLK, rows):
        bm = bmap_ref[r0:r0 + rows, :]
        accs = [jnp.full(bm.shape, NEG, F32)] * (2 * N_PAIRS)
        for b in range(NUM_BUCKETS):
            hit = bm == b
            accs = [jnp.where(hit, rel_ref[b, head0 + h], acc) for h, acc in enumerate(accs)]
        for h, acc in enumerate(accs):
            o_ref[h // 2, (h % 2) * BLK + r0:(h % 2) * BLK + r0 + rows, :] = acc


def _band_bias_call(rel_bias, band_map):
    n_tab = band_map.shape[0]
    return pl.pallas_call(
        _band_bias_kernel,
        grid=(n_tab,),
        in_specs=[pl.BlockSpec(memory_space=pltpu.SMEM),
                  pl.BlockSpec((None, BLK, 2 * BLK), lambda t: (t, 0, 0))],
        out_specs=pl.BlockSpec((None, N_PAIRS, 2 * BLK, 2 * BLK), lambda t: (t, 0, 0, 0)),
        out_shape=jax.ShapeDtypeStruct((n_tab, N_PAIRS, 2 * BLK, 2 * BLK), F32),
        compiler_params=pltpu.CompilerParams(dimension_semantics=("arbitrary",)),
        name="band_bias",
    )(rel_bias, band_map)


def _cached_bias_kernel(relt_ref, amap_ref, bmap_ref, oa_ref, ob_ref):
    relt = relt_ref[...]

    def table(bm, heads):
        acc = jnp.full(bm.shape, NEG, F32)
        for b in range(NUM_BUCKETS):
            acc = jnp.where(bm == b, heads[:, b:b + 1], acc)
        return acc

    for t in range(amap_ref.shape[0]):
        oa_ref[t] = table(jnp.broadcast_to(amap_ref[t], oa_ref.shape[1:]), relt[:N_HEADS_A])
    ob_ref[...] = table(jnp.broadcast_to(bmap_ref[...], ob_ref.shape), relt[N_HEADS_A:])


def _cached_bias_call(rel_bias, cached_a, cached_b):
    return pl.pallas_call(
        _cached_bias_kernel,
        out_shape=(jax.ShapeDtypeStruct((cached_a.shape[0], N_HEADS_A, cached_a.shape[2]), F32),
                   jax.ShapeDtypeStruct((N_HEADS_B, cached_b.shape[1]), F32)),
        name="cached_bias",
    )(rel_bias.T, cached_a, cached_b)


def _ffn_kernel(*refs, alpha, pre_mix, mix_slots, n_chunks, fc, n_side, sample0, mod_row):
    refs = list(refs)
    x_ref = refs.pop(0)
    if pre_mix:
        mix_refs = [refs.pop(0) for _ in range(2 if mix_slots else 1)]
        wo_ref, gm_ref, ln1g_ref, ln1b_ref = (refs.pop(0) for _ in range(4))
    sh_ref, sc_ref, gt_ref, wg_ref, wu_ref, wd_ref, lng_ref, lnb_ref = (refs.pop(0) for _ in range(8))
    if n_side:
        side_in = [refs.pop(0) for _ in range(10)]
        ak_hbm, av_hbm, o_ref, oc_ref, h_ref, acc_ref, kbuf, vbuf, sem = (refs.pop(0) for _ in range(9))
        side_scratch = refs
        end = sample0 + pl.num_programs(0) * n_side
        first = sample0 + pl.program_id(0) * n_side

        def cache_copies(g, slot):
            return (pltpu.make_async_copy(ak_hbm.at[g], kbuf.at[slot], sem.at[0, slot]),
                    pltpu.make_async_copy(av_hbm.at[g], vbuf.at[slot], sem.at[1, slot]))

        @pl.when(pl.program_id(0) == 0)
        def _():
            for cp in cache_copies(sample0, 0):
                cp.start()

        def side(j):
            g, slot = first + j, j % 2
            for cp in cache_copies(g, slot):
                cp.wait()

            @pl.when(g + 1 < end)
            def _():
                for cp in cache_copies(g + 1, 1 - slot):
                    cp.start()

            _cached_sample(j, kbuf.at[slot], vbuf.at[slot], *side_in, oc_ref, *side_scratch)

        assert n_side % 2 == 0
        side_at = {(j * n_chunks) // n_side if j else -1: j for j in range(n_side)}
    else:
        o_ref, h_ref, acc_ref = refs
        side_at = {}

    if -1 in side_at:
        side(side_at[-1])
    x = x_ref[...]
    if pre_mix:
        if mix_slots:
            mixed = jnp.concatenate([r[j] for r in mix_refs for j in range(N_PAIRS)], axis=-1)
        else:
            mixed = mix_refs[0][...].astype(BF16)
        y = jnp.dot(mixed, wo_ref[...], preferred_element_type=F32)
        x = _layer_norm(alpha * x + _seq_mod(gm_ref, mod_row) * y, ln1g_ref[...], ln1b_ref[...])
    h_ref[...] = (x * (1.0 + _seq_mod(sc_ref, mod_row)) + _seq_mod(sh_ref, mod_row)).astype(BF16)
    for c in range(n_chunks):
        if c in side_at:
            side(side_at[c])
        cols = slice(c * fc, (c + 1) * fc)
        g = jnp.dot(h_ref[...], wg_ref[:, cols], preferred_element_type=F32)
        u = jnp.dot(h_ref[...], wu_ref[:, cols], preferred_element_type=F32)
        y = jnp.dot((_silu(g) * u).astype(BF16), wd_ref[cols, :], preferred_element_type=F32)
        if c == 0:
            acc_ref[...] = y
        else:
            acc_ref[...] += y
    o_ref[...] = _layer_norm(alpha * x + 0.5 * _seq_mod(gt_ref, mod_row) * acc_ref[...], lng_ref[...], lnb_ref[...])


def _ffn_call(x, mod2, mod_ks, seq0, rows_per_seq, wg, wu, wd, ln_g, ln_b, alpha, tm, fc,
              mix=None, w_out=None, gate_k=None, ln1_g=None, ln1_b=None, side=None):
    n, d = x.shape
    d_ff = wg.shape[1]
    n_chunks = d_ff // fc
    assert n_chunks * fc == d_ff
    per_row = rows_per_seq == 1
    tiles_per_seq = max(rows_per_seq // tm, 1)

    def mod_spec(k):
        if per_row:
            return pl.BlockSpec((tm, d), lambda i, k=k: (seq0 // tm + i, k))
        return _seq_mod_spec(seq0, tiles_per_seq, k, d)

    mod_arr = mod2
    row_spec = pl.BlockSpec((tm, d), lambda i: (i, 0))
    vec_spec = pl.BlockSpec((1, d), lambda i: (0, 0))
    args, specs = [x], [row_spec]
    pre_mix = mix is not None
    mix_slots = pre_mix and isinstance(mix, tuple)
    blocks = 2 * _nbytes((tm, d), F32) + (3 * _nbytes((tm, d), F32) if per_row else 0)
    resident = 3 * _nbytes(wg.shape, BF16)
    if pre_mix:
        if mix_slots:
            for mm in mix:
                args.append(mm)
                specs.append(pl.BlockSpec((N_PAIRS, tm, LANES), lambda i: (0, i, 0)))
        else:
            args.append(mix)
            specs.append(row_spec)
        args += [w_out, mod_arr, ln1_g.reshape(1, d), ln1_b.reshape(1, d)]
        specs += [_resident_spec((d, d)), mod_spec(gate_k), vec_spec, vec_spec]
        blocks += _nbytes((tm, d), F32)
        resident += _nbytes((d, d), BF16)
    args += [mod_arr, mod_arr, mod_arr, wg, wu, wd, ln_g.reshape(1, d), ln_b.reshape(1, d)]
    specs += [mod_spec(mod_ks[0]), mod_spec(mod_ks[1]), mod_spec(mod_ks[2]),
              _resident_spec(wg.shape), _resident_spec(wu.shape), _resident_spec(wd.shape), vec_spec, vec_spec]
    scratch = _nbytes((tm, d), BF16) + _nbytes((tm, d), F32) + resident
    temps = 6 * _nbytes((tm, fc), F32) + 3 * _nbytes((tm, d), F32)
    out_specs, out_shape = row_spec, jax.ShapeDtypeStruct((n, d), F32)
    scratch_shapes = [pltpu.VMEM((tm, d), BF16), pltpu.VMEM((tm, d), F32)]
    n_side = sample0 = 0
    if side is not None:
        (*small, ak, av), sample0, count = side
        n_steps = n // tm
        n_side = count // n_steps
        assert n_side * n_steps == count and sample0 % n_side == 0
        blk0 = sample0 // n_side
        for a in small[:6]:
            args.append(a)
            specs.append(pl.BlockSpec((n_side,) + a.shape[1:],
                                      lambda i, nd=a.ndim: (blk0 + i,) + (0,) * (nd - 1)))
            blocks += n_side * _nbytes(a.shape[1:-1] + (max(a.shape[-1], LANES),), F32)
        for a in small[6:]:
            args.append(a)
            specs.append(pl.BlockSpec(a.shape, lambda i, nd=a.ndim: (0,) * nd))
            blocks += _nbytes(a.shape, F32)
        args += [ak, av]
        specs += [pl.BlockSpec(memory_space=pl.ANY)] * 2
        q3 = small[0]
        out_specs = [row_spec, pl.BlockSpec((n_side,) + q3.shape[1:], lambda i: (i, 0, 0))]
        out_shape = [out_shape, jax.ShapeDtypeStruct((count,) + q3.shape[1:], F32)]
        sa_shape = (n_side, N_HEADS_A, ak.shape[3])
        sb_shape = (n_side, N_HEADS_B, small[4].shape[3])
        scratch_shapes += [pltpu.VMEM((2,) + ak.shape[1:], F32), pltpu.VMEM((2,) + av.shape[1:], F32),
                           pltpu.SemaphoreType.DMA((2, 2)),
                           pltpu.VMEM(sa_shape, F32), pltpu.VMEM(sb_shape, F32),
                           pltpu.VMEM(sa_shape, F32), pltpu.VMEM(sb_shape, F32)]
        scratch += 4 * _nbytes(ak.shape[1:], F32) + 2 * _nbytes(sa_shape, F32) + 2 * _nbytes(sb_shape, F32)
    return pl.pallas_call(
        functools.partial(_ffn_kernel, alpha=alpha, pre_mix=pre_mix, mix_slots=mix_slots,
                          n_chunks=n_chunks, fc=fc, n_side=n_side, sample0=sample0,
                          mod_row=None if per_row else (seq0, tiles_per_seq)),
        grid=(n // tm,),
        in_specs=specs,
        out_specs=out_specs,
        out_shape=out_shape,
        scratch_shapes=scratch_shapes,
        compiler_params=pltpu.CompilerParams(
            dimension_semantics=("arbitrary",),
            vmem_limit_bytes=_vmem_limit(blocks, scratch, temps)),
        name=("ffn_mix" if pre_mix else "ffn") + ("_cached" if n_side else ""),
    )(*args)


def _proj_prompt_kernel(x_ref, sh_ref, sc_ref, w_ref, p_ref, p4_ref, p16_ref, ak_ref, av_ref, bk_ref, bv_ref,
                        stage_ref, stage4_ref, *, tm, b_rows, mod_row):
    h = (x_ref[...] * (1.0 + _seq_mod(sc_ref, mod_row)) + _seq_mod(sh_ref, mod_row)).astype(BF16)
    lane_lo = lax.broadcasted_iota(jnp.int32, (tm, LANES), 1) < HEAD_DIM
    for g in range(4):
        res = jnp.dot(h, w_ref[:, g * A_COLS:(g + 1) * A_COLS], preferred_element_type=F32)
        scaled = res * SCALE if g in (0, 3) else res
        for s in range(N_PAIRS):
            slab = scaled[:, s * LANES:(s + 1) * LANES]
            p_ref[g * N_PAIRS + s] = slab.astype(BF16)
            if g < 3:
                slot = g * N_PAIRS + s
                d4, d16 = p4_ref.shape[1], p16_ref.shape[1]
                stage_ref[slot] = slab
                for r in range(d4):
                    rows = stage_ref[slot, pl.ds(r, tm // d4, stride=d4), :]
                    p4_ref[slot, r] = rows.astype(BF16)
                    stage4_ref[slot, r] = rows
                for r in range(d16):
                    rows = stage4_ref[slot, r % d4, pl.ds(r // d4, tm // d16, stride=d16 // d4), :]
                    p16_ref[slot, r] = rows.astype(BF16)
        if g in (1, 2):
            out_ref = ak_ref if g == 1 else av_ref
            out_ref[...] = res
    res = jnp.dot(h, w_ref[:, 4 * A_COLS:], preferred_element_type=F32)
    for j, out_ref in enumerate((bk_ref, bv_ref)):
        slab = res[:, j * LANES:(j + 1) * LANES]
        swapped = pltpu.roll(slab, HEAD_DIM, axis=1)
        p_ref[4 * N_PAIRS + N_KV_B * j] = jnp.where(lane_lo, slab, swapped).astype(BF16)
        p_ref[4 * N_PAIRS + N_KV_B * j + 1] = jnp.where(lane_lo, swapped, slab).astype(BF16)

        out_ref[...] = slab[tm - b_rows:, :]


def _proj_prompt_call(x, mod2, seq0, seq, w_ext, a_rows, b_rows, tm):
    n, d = x.shape
    batch = n // seq
    tiles_per_seq = seq // tm
    a_tiles = a_rows // tm
    n_slots = 4 * N_PAIRS + 2 * N_KV_B
    assert w_ext.shape[1] == 4 * A_COLS + 2 * LANES and N_KV_B * HEAD_DIM == LANES

    def mod_spec(k):
        return _seq_mod_spec(seq0, tiles_per_seq, k, d)

    def a_map(i):
        return (i // tiles_per_seq, jnp.maximum(i % tiles_per_seq - (tiles_per_seq - a_tiles), 0), 0)

    a_blk = (None, tm, A_COLS)
    b_blk = (None, b_rows, LANES)
    a_slots = 3 * N_PAIRS
    d4, d16 = A_PATTERNS[1][1], A_PATTERNS[2][1]
    assert tm % (16 * d16) == 0

    def dil_spec(dil):
        return pl.BlockSpec((a_slots, None, dil, tm // dil, LANES),
                            lambda i: (0, i // tiles_per_seq, 0, i % tiles_per_seq, 0))

    blocks = (_nbytes((tm, d), F32) + _nbytes(w_ext.shape, BF16) + _nbytes((n_slots + 2 * a_slots, tm, LANES), BF16)
              + 2 * _nbytes((tm, A_COLS), F32) + 2 * _nbytes((b_rows, LANES), F32))
    return pl.pallas_call(
        functools.partial(_proj_prompt_kernel, tm=tm, b_rows=b_rows, mod_row=(seq0, tiles_per_seq)),
        grid=(n // tm,),
        in_specs=[pl.BlockSpec((tm, d), lambda i: (i, 0)), mod_spec(3), mod_spec(4),
                  pl.BlockSpec(w_ext.shape, lambda i: (0, 0))],
        out_specs=[pl.BlockSpec((n_slots, tm, LANES), lambda i: (0, i, 0)),
                   dil_spec(d4), dil_spec(d16),
                   pl.BlockSpec(a_blk, a_map),
                   pl.BlockSpec(a_blk, a_map),
                   pl.BlockSpec(b_blk, lambda i: (i // tiles_per_seq, 0, 0)),
                   pl.BlockSpec(b_blk, lambda i: (i // tiles_per_seq, 0, 0))],
        out_shape=[jax.ShapeDtypeStruct((n_slots, n, LANES), BF16),
                   jax.ShapeDtypeStruct((a_slots, batch, d4, seq // d4, LANES), BF16),
                   jax.ShapeDtypeStruct((a_slots, batch, d16, seq // d16, LANES), BF16),
                   jax.ShapeDtypeStruct((batch, a_rows, A_COLS), F32),
                   jax.ShapeDtypeStruct((batch, a_rows, A_COLS), F32),
                   jax.ShapeDtypeStruct((batch, b_rows, LANES), F32),
                   jax.ShapeDtypeStruct((batch, b_rows, LANES), F32)],
        scratch_shapes=[pltpu.VMEM((a_slots, tm, LANES), F32), pltpu.VMEM((a_slots, d4, tm // d4, LANES), F32)],
        compiler_params=pltpu.CompilerParams(
            dimension_semantics=("arbitrary",),
            vmem_limit_bytes=_vmem_limit(blocks, 2 * _nbytes((a_slots, tm, LANES), F32),
                                         4 * _nbytes((tm, A_COLS), F32))),
        name="proj_prompt",
    )(x, mod2, mod2, w_ext)


def _proj_sample_kernel(x_ref, sh_ref, sc_ref, w_ref, o_ref):
    h = (x_ref[...] * (1.0 + sc_ref[...]) + sh_ref[...]).astype(BF16)
    o_ref[...] = jnp.dot(h, w_ref[...], preferred_element_type=F32)


def _proj_sample_call(x, mod2, w_ext):
    n, d = x.shape
    cols = w_ext.shape[1]
    blocks = 3 * _nbytes((n, d), F32) + _nbytes(w_ext.shape, BF16) + _nbytes((n, cols), F32)
    return pl.pallas_call(
        _proj_sample_kernel,
        grid=(1,),
        in_specs=[pl.BlockSpec((n, d), lambda i: (0, 0)),
                  pl.BlockSpec((n, d), lambda i: (0, 3)),
                  pl.BlockSpec((n, d), lambda i: (0, 4)),
                  pl.BlockSpec(w_ext.shape, lambda i: (0, 0))],
        out_specs=pl.BlockSpec((n, cols), lambda i: (0, 0)),
        out_shape=jax.ShapeDtypeStruct((n, cols), F32),
        compiler_params=pltpu.CompilerParams(
            dimension_semantics=("arbitrary",), vmem_limit_bytes=_vmem_limit(blocks)),
        name="proj_sample",
    )(x, mod2, mod2, w_ext)


def _pair_scores(q, kk, bias, lane_lo):
    zero = jnp.zeros_like(q)
    q2 = jnp.concatenate([jnp.where(lane_lo, q, zero), jnp.where(lane_lo, zero, q)], axis=0)
    s = lax.dot_general(q2, kk, (((1,), (1,)), ((), ())), preferred_element_type=F32)
    return s + bias


def _pair_select(a, lane_lo):
    return jnp.where(lane_lo, a[:BLK], a[BLK:])


GROUP = 16


def _band_sequence(q_ref, k_ref, v_ref, n_blocks, load_bias, block):
    def one(jb, first):
        if first:
            block(jb, q_ref[0:BLK, :], k_ref[0:BLK, :], v_ref[0:BLK, :], load_bias(True))
        else:
            r0 = jb * BLK
            if not isinstance(jb, int):
                r0 = pl.multiple_of(r0, BLK)
            block(jb, q_ref[pl.ds(r0, BLK), :], k_ref[pl.ds(r0 - BLK, 2 * BLK), :],
                  v_ref[pl.ds(r0 - BLK, 2 * BLK), :], load_bias(False))

    for jb in range(min(GROUP, n_blocks)):
        one(jb, jb == 0)
    if n_blocks > GROUP:
        assert n_blocks % GROUP == 0

        def group(g, carry):
            for j in range(GROUP):
                one(g * GROUP + j, False)
            return carry

        lax.fori_loop(1, n_blocks // GROUP, group, 0)


def _attn_kernel(sink_ref, q16_ref, k16_ref, v16_ref, q4_ref, k4_ref, v4_ref, q1_ref, k1_ref, v1_ref, bias_ref,
                 qw_ref, kw_ref, vw_ref, bias_w_ref, o_ref, ow_ref,
                 acc16_ref, m16_ref, den16_ref, acc4_ref, m4_ref, den4_ref, *, seq):
    _window_attention(sink_ref, qw_ref, kw_ref, vw_ref, bias_w_ref, ow_ref, seq)
    lane_lo = lax.broadcasted_iota(jnp.int32, (BLK, LANES), 1) < HEAD_DIM
    state16 = (acc16_ref, m16_ref, den16_ref)
    state4 = (acc4_ref, m4_ref, den4_ref)

    def update(q, kk, vv, bias, rows, old, new):
        s = _pair_scores(q, kk, bias, lane_lo)
        part = s[:, :LANES] if s.shape[1] == LANES else jnp.maximum(s[:, :LANES], s[:, LANES:])
        if old is not None:
            acc_old, m_old, den_old = (ref[rows, :] for ref in old)
            part = jnp.maximum(part, jnp.concatenate([jnp.where(lane_lo, m_old, NEG),
                                                      jnp.where(lane_lo, NEG, m_old)], axis=0))
        m2 = jnp.max(part, axis=-1, keepdims=True)
        p = jnp.exp(s - m2)
        l2 = jnp.sum(p, axis=-1, keepdims=True)
        pv = _pair_select(jnp.dot(p.astype(BF16), vv, preferred_element_type=F32), lane_lo)
        m_new = _pair_select(m2, lane_lo)
        den = _pair_select(l2, lane_lo)
        if old is None:
            acc = pv
        else:
            a = jnp.exp(m_old - m_new)
            acc = a * acc_old + pv
            den = a * den_old + den
        if new is None:
            o_ref[rows, :] = (acc / den).astype(o_ref.dtype)
        else:
            new[0][rows, :] = acc
            new[1][rows, :] = m_new
            new[2][rows, :] = den

    def branch(q_ref, k_ref, v_ref, table, dil, r, old, new):
        def block(jb, q, kk, vv, bias):
            start = r + jb * (BLK * dil)
            rows = pl.ds(start, BLK, stride=dil) if dil > 1 else pl.ds(start, BLK)
            update(q, kk, vv, bias, rows, old, new)

        def load_bias(first):
            return bias_ref[table, :, BLK:2 * BLK] if first else bias_ref[table]

        _band_sequence(q_ref, k_ref, v_ref, q_ref.shape[0] // BLK, load_bias, block)

    def dilated(q_ref, k_ref, v_ref, table, old, new):
        dil = q_ref.shape[0]
        per_group = min(max(GROUP // (q_ref.shape[1] // BLK), 1), dil)
        assert dil % per_group == 0

        def group(g, carry):
            for rr in range(per_group):
                r = g * per_group + rr
                branch(q_ref.at[r], k_ref.at[r], v_ref.at[r], table, dil, r, old, new)
            return carry

        lax.fori_loop(0, dil // per_group, group, 0)

    dilated(q16_ref, k16_ref, v16_ref, 2, None, state16)
    dilated(q4_ref, k4_ref, v4_ref, 1, state16, state4)
    branch(q1_ref, k1_ref, v1_ref, 0, 1, 0, state4, None)


def _attn_call(p_slots, p4, p16, bias_a, bias_b, sinks, batch, seq):
    n_slots = p_slots.shape[0]
    dils = [d for _, d in A_PATTERNS]
    assert dils == [1, p4.shape[2], p16.shape[2]] and seq % (dils[2] * BLK) == 0
    view1 = p_slots.reshape(n_slots, batch, seq, LANES)
    qw0, kw0, vw0 = 3 * N_PAIRS, 4 * N_PAIRS, 4 * N_PAIRS + N_KV_B
    pairs_per_kv = N_PAIRS // N_KV_B
    wblk = (None, None, seq, LANES)

    in_specs, args = [pl.BlockSpec(memory_space=pltpu.SMEM)], [sinks]
    for arr in (p16, p4, view1):
        for slot0 in (0, N_PAIRS, 2 * N_PAIRS):
            in_specs.append(pl.BlockSpec((None, None) + arr.shape[2:],
                                         lambda b, hp, slot0=slot0, nd=arr.ndim: (slot0 + hp, b) + (0,) * (nd - 2)))
            args.append(arr)
    in_specs.append(pl.BlockSpec((len(A_PATTERNS), None, 2 * BLK, 2 * BLK), lambda b, hp: (0, hp, 0, 0)))
    args.append(bias_a)
    in_specs += [pl.BlockSpec(wblk, lambda b, hp: (qw0 + hp, b, 0, 0)),
                 pl.BlockSpec(wblk, lambda b, hp: (kw0 + hp // pairs_per_kv, b, 0, 0)),
                 pl.BlockSpec(wblk, lambda b, hp: (vw0 + hp // pairs_per_kv, b, 0, 0)),
                 pl.BlockSpec((None, 2 * BLK, 2 * BLK), lambda b, hp: (hp, 0, 0))]
    args += [view1, view1, view1, bias_b]
    out_spec = pl.BlockSpec((None, seq, LANES), lambda b, hp: (hp, b, 0))
    out_shape = jax.ShapeDtypeStruct((N_PAIRS, batch * seq, LANES), BF16)
    blocks = 14 * _nbytes((seq, LANES), BF16) + _nbytes((4, 2 * BLK, 2 * BLK), F32)
    scratch = 6 * _nbytes((seq, LANES), F32)
    return pl.pallas_call(
        functools.partial(_attn_kernel, seq=seq),
        grid=(batch, N_PAIRS),
        in_specs=in_specs,
        out_specs=[out_spec, out_spec],
        out_shape=[out_shape, out_shape],
        scratch_shapes=[pltpu.VMEM((seq, LANES), F32)] * 6,
        compiler_params=pltpu.CompilerParams(
            dimension_semantics=("arbitrary", "arbitrary"),
            vmem_limit_bytes=_vmem_limit(blocks, scratch, 4 * GROUP * _nbytes((2 * BLK, 2 * BLK), F32))),
        name="attn_prompt",
    )(*args)


def _window_attention(sink_ref, q_ref, k_ref, v_ref, bias_ref, o_ref, seq):
    hp = pl.program_id(1)
    lane_lo = lax.broadcasted_iota(jnp.int32, (BLK, LANES), 1) < HEAD_DIM
    row_lo = lax.broadcasted_iota(jnp.int32, (2 * BLK, 1), 0) < BLK
    sink2 = jnp.where(row_lo, sink_ref[2 * hp], sink_ref[2 * hp + 1])

    def block(jb, q, kk, vv, bias):
        s = _pair_scores(q, kk, bias, lane_lo)
        m2 = jnp.maximum(jnp.max(s, axis=-1, keepdims=True), sink2)
        p = jnp.exp(s - m2)
        l2 = jnp.sum(p, axis=-1, keepdims=True) + jnp.exp(sink2 - m2)
        pv = _pair_select(jnp.dot(p.astype(BF16), vv, preferred_element_type=F32), lane_lo)
        r0 = jb * BLK if isinstance(jb, int) else pl.multiple_of(jb * BLK, BLK)
        o_ref[pl.ds(r0, BLK), :] = (pv / _pair_select(l2, lane_lo)).astype(o_ref.dtype)

    def load_bias(first):
        return bias_ref[:, BLK:2 * BLK] if first else bias_ref[...]

    _band_sequence(q_ref, k_ref, v_ref, seq // BLK, load_bias, block)


def _cached_sample(i, ak_ref, av_ref, q_ref, qt_ref, knt_ref, vn_ref, bk_ref, bv_ref, bias_a_ref, bias_b_ref,
                   relt_ref, sink_ref, o_ref, sa_ref, sb_ref, wa_ref, wb_ref):
    n_branches = bias_a_ref.shape[0]
    heads_per_kv = N_HEADS_B // N_KV_B
    bias0 = relt_ref[:, 0:1]
    sink = sink_ref[...]
    q = q_ref[i] * SCALE
    vn = vn_ref[i]
    s0 = jnp.sum(qt_ref[i] * SCALE * knt_ref[i], axis=-1, keepdims=True) + bias0
    for h in range(N_HEADS_A):
        sa_ref[i, h:h + 1, :] = jnp.sum(ak_ref[h] * q[:, h:h + 1], axis=0, keepdims=True)
    for hb in range(N_HEADS_B):
        h = N_HEADS_A + hb
        sb_ref[i, hb:hb + 1, :] = jnp.sum(bk_ref[i, hb // heads_per_kv] * q[:, h:h + 1], axis=0, keepdims=True)

    s0a = s0[:N_HEADS_A]
    ts = [sa_ref[i] + bias_a_ref[t] for t in range(n_branches)]
    m = s0a
    for t_ in ts:
        m = jnp.maximum(m, jnp.max(t_, axis=-1, keepdims=True))
    w = jnp.exp(ts[0] - m)
    for t_ in ts[1:]:
        w = w + jnp.exp(t_ - m)
    p0a = n_branches * jnp.exp(s0a - m)
    inv = 1.0 / (jnp.sum(w, axis=-1, keepdims=True) + p0a)
    wa_ref[i] = w * inv
    p0a = p0a * inv

    s0b = s0[N_HEADS_A:]
    tb = sb_ref[i] + bias_b_ref[...]
    mb = jnp.maximum(jnp.maximum(s0b, sink), jnp.max(tb, axis=-1, keepdims=True))
    wb = jnp.exp(tb - mb)
    p0b = jnp.exp(s0b - mb)
    invb = 1.0 / (jnp.sum(wb, axis=-1, keepdims=True) + p0b + jnp.exp(sink - mb))
    wb_ref[i] = wb * invb
    p0b = p0b * invb

    cols = []
    for h in range(N_HEADS_A):
        acc = jnp.sum(av_ref[h] * wa_ref[i, h:h + 1, :], axis=-1, keepdims=True)
        cols.append(acc + p0a[h:h + 1, :] * vn[:, h:h + 1])
    for hb in range(N_HEADS_B):
        h = N_HEADS_A + hb
        acc = jnp.sum(bv_ref[i, hb // heads_per_kv] * wb_ref[i, hb:hb + 1, :], axis=-1, keepdims=True)
        cols.append(acc + p0b[hb:hb + 1, :] * vn[:, h:h + 1])
    o_ref[i] = jnp.concatenate(cols, axis=1)


def kernel(x_prompt, x_sample, cache_a_k, cache_a_v, cache_b_k, cache_b_v, c_prompt, c_sample, rel_bias, w_ada, b_ada, ffn1_wg, ffn1_wu, ffn1_wd, w_in, w_out, sinks, ffn2_wg, ffn2_wu, ffn2_wd, ln_g, ln_b):
    batch, seq, d = x_prompt.shape
    n_dec, dec_seq, _ = x_sample.shape
    depth = w_ada.shape[0]
    assert dec_seq == 1 and d == 2 * A_COLS and all(w // dl == BLK for w, dl in A_PATTERNS)
    alpha = (2 * depth) ** 0.25
    fc = 256
    tm = 512
    tm_proj = 1024 if seq % 1024 == 0 else tm
    a_rows, b_rows = min(A_PATTERNS[-1][0], seq), min(B_WINDOW, seq)
    q_cols = 4 * A_COLS
    kvb = N_KV_B * HEAD_DIM
    heads_per_kv = N_HEADS_B // N_KV_B

    band_map, cached_a, cached_b = _bucket_maps(cache_a_k.shape[2], cache_b_k.shape[2])
    band_bias = _band_bias_call(rel_bias, band_map)
    bias_a, bias_b = band_bias[:len(A_PATTERNS)], band_bias[len(A_PATTERNS)]
    bias_sa, bias_sb = _cached_bias_call(rel_bias, cached_a, cached_b)

    pad = (-(n_dec + batch)) % 16
    c_all = jnp.concatenate([c_sample, c_prompt, jnp.zeros((pad, d), F32)], axis=0)
    xp = x_prompt.reshape(batch * seq, d)
    xs = x_sample.reshape(n_dec, d)
    outs = [[] for _ in range(8)]

    def head_rows(x, n_heads):
        return x.reshape(n_dec, n_heads, HEAD_DIM)

    for l in range(depth):
        mod2 = _ada_call(c_all, w_ada[l], b_ada[l])
        ffn1 = [w[l].astype(BF16) for w in (ffn1_wg, ffn1_wu, ffn1_wd)]
        ffn2 = [w[l].astype(BF16) for w in (ffn2_wg, ffn2_wu, ffn2_wd)]
        w_in_s = w_in[l].astype(BF16)
        wo = w_out[l].astype(BF16)

        s1 = _ffn_call(xs, mod2, (0, 1, 2), 0, 1, *ffn1, ln_g[l, 0], ln_b[l, 0], alpha, n_dec, fc)
        proj_s = _proj_sample_call(s1, mod2, w_in_s)
        c = A_COLS
        sak, sav = proj_s[:, c:2 * c], proj_s[:, 2 * c:3 * c]
        sbk, sbv = proj_s[:, q_cols:q_cols + kvb], proj_s[:, q_cols + kvb:q_cols + 2 * kvb]
        qt3 = jnp.concatenate([head_rows(proj_s[:, :c], N_HEADS_A),
                               head_rows(proj_s[:, 3 * c:q_cols], N_HEADS_B)], axis=1)
        knt3 = jnp.concatenate([head_rows(sak, N_HEADS_A),
                                jnp.repeat(head_rows(sbk, N_KV_B), heads_per_kv, axis=1)], axis=1)
        vnt3 = jnp.concatenate([head_rows(sav, N_HEADS_A),
                                jnp.repeat(head_rows(sbv, N_KV_B), heads_per_kv, axis=1)], axis=1)
        to_cols = (0, 2, 3, 1)
        decode_side = (qt3.transpose(0, 2, 1), qt3, knt3, vnt3.transpose(0, 2, 1),
                       cache_b_k[l].transpose(to_cols), cache_b_v[l].transpose(to_cols),
                       bias_sa, bias_sb, rel_bias.T, sinks[l].reshape(N_HEADS_B, 1),
                       cache_a_k[l].transpose(to_cols), cache_a_v[l].transpose(to_cols))

        n_first = n_dec // 2 if (n_dec // 2) % (2 * (batch * seq // tm)) == 0 else n_dec
        x1, mix3 = _ffn_call(xp, mod2, (0, 1, 2), n_dec, seq, *ffn1, ln_g[l, 0], ln_b[l, 0], alpha, tm, fc,
                             side=(decode_side, 0, n_first))
        p_slots, p4, p16, pak, pav, pbk, pbv = _proj_prompt_call(x1, mod2, n_dec, seq, w_in_s, a_rows, b_rows, tm_proj)
        mix_a, mix_b = _attn_call(p_slots, p4, p16, bias_a, bias_b, sinks[l], batch, seq)
        xp = _ffn_call(x1, mod2, (6, 7, 8), n_dec, seq, *ffn2, ln_g[l, 2], ln_b[l, 2], alpha, tm, fc,
                       mix=(mix_a, mix_b), w_out=wo, gate_k=5, ln1_g=ln_g[l, 1], ln1_b=ln_b[l, 1],
                       side=(decode_side, n_first, n_dec - n_first) if n_first < n_dec else None)
        if n_first < n_dec:
            xp, mix3_rest = xp
            mix3 = jnp.concatenate([mix3, mix3_rest], axis=0)

        mix_s = mix3.transpose(0, 2, 1).reshape(n_dec, d)
        xs = _ffn_call(s1, mod2, (6, 7, 8), 0, 1, *ffn2, ln_g[l, 2], ln_b[l, 2], alpha, n_dec, fc,
                       mix=mix_s, w_out=wo, gate_k=5, ln1_g=ln_g[l, 1], ln1_b=ln_b[l, 1])

        new = [pak.reshape(batch, a_rows, N_HEADS_A, HEAD_DIM), pav.reshape(batch, a_rows, N_HEADS_A, HEAD_DIM),
               pbk.reshape(batch, b_rows, N_KV_B, HEAD_DIM), pbv.reshape(batch, b_rows, N_KV_B, HEAD_DIM),
               sak.reshape(n_dec, 1, N_HEADS_A, HEAD_DIM), sav.reshape(n_dec, 1, N_HEADS_A, HEAD_DIM),
               sbk.reshape(n_dec, 1, N_KV_B, HEAD_DIM), sbv.reshape(n_dec, 1, N_KV_B, HEAD_DIM)]
        for acc, arr in zip(outs, new):
            acc.append(arr)

    return (xp.reshape(batch, seq, d), xs.reshape(n_dec, 1, d)) + tuple(jnp.stack(o) for o in outs)
```

```python
import functools
import math

import jax
import jax.numpy as jnp
import numpy as np
from jax import lax
from jax.experimental import pallas as pl
from jax.experimental.pallas import tpu as pltpu

HEAD_DIM = 64
N_HEADS_A = 8
N_HEADS_B = 8
N_KV_B = 2
A_PATTERNS = ((128, 1), (512, 4), (2048, 16))
B_WINDOW = 128
BLK = 128
NUM_BUCKETS = 32
MAX_DISTANCE = 2048
LN_EPS = 1e-5
NEG = -1e30
SCALE = HEAD_DIM ** -0.5

LANES = 128
SUBLANES = 8
V7X_VMEM_BYTES = 64 * 1024 * 1024
A_COLS = N_HEADS_A * HEAD_DIM
N_PAIRS = A_COLS // LANES

F32 = jnp.float32
BF16 = jnp.bfloat16


SPILL_ALLOWANCE_BYTES = 4 << 20
VMEM_RESERVED_BYTES = 6 << 20


def _vmem_limit(pipelined_bytes, scratch_bytes=0, temp_bytes=0):
    want = 2 * pipelined_bytes + scratch_bytes + temp_bytes + SPILL_ALLOWANCE_BYTES
    return int(min(want, V7X_VMEM_BYTES - VMEM_RESERVED_BYTES))


def _resident_spec(shape):
    return pl.BlockSpec(shape, lambda *_: (0,) * len(shape), pipeline_mode=pl.Buffered(1))


def _nbytes(shape, dtype):
    return math.prod(shape) * jnp.dtype(dtype).itemsize


def _layer_norm(x, g, b):
    mu = jnp.mean(x, axis=-1, keepdims=True)
    xc = x - mu
    var = jnp.mean(xc * xc, axis=-1, keepdims=True)
    return xc * lax.rsqrt(var + LN_EPS) * g + b


def _silu(x):
    return x * jax.nn.sigmoid(x)


def _seq_mod_spec(seq0, tiles_per_seq, k, d):
    return pl.BlockSpec((SUBLANES, d), lambda i: ((seq0 + i // tiles_per_seq) // SUBLANES, k))


def _seq_mod(ref, mod_row):
    if mod_row is None:
        return ref[...]
    seq0, tiles_per_seq = mod_row
    return ref[pl.ds((seq0 + pl.program_id(0) // tiles_per_seq) % SUBLANES, 1), :]


def _ada_kernel(c_ref, w_ref, b_ref, o_ref):
    a = _silu(c_ref[...]).astype(BF16)
    o_ref[...] = jnp.dot(a, w_ref[...].astype(BF16), preferred_element_type=F32) + b_ref[...]


def _ada_call(c_all, w_ada, b_ada):
    m, d = c_all.shape
    n = w_ada.shape[1]
    tn = n // 8
    assert tn % LANES == 0
    blocks = _nbytes((m, d), F32) + _nbytes((d, tn), F32) + _nbytes((m, tn), F32)
    return pl.pallas_call(
        _ada_kernel,
        grid=(n // tn,),
        in_specs=[pl.BlockSpec((m, d), lambda j: (0, 0)),
                  pl.BlockSpec((d, tn), lambda j: (0, j)),
                  pl.BlockSpec((1, tn), lambda j: (0, j))],
        out_specs=pl.BlockSpec((m, tn), lambda j: (0, j)),
        out_shape=jax.ShapeDtypeStruct((m, n), F32),
        compiler_params=pltpu.CompilerParams(
            dimension_semantics=("arbitrary",),
            vmem_limit_bytes=_vmem_limit(blocks, temp_bytes=_nbytes((d, tn), BF16))),
        name="ada_mod",
    )(c_all, w_ada, b_ada.reshape(1, n))


def _t5_bucket(n):
    max_exact = NUM_BUCKETS // 2
    nf = np.maximum(n, 1).astype(np.float32)
    large = max_exact + (np.log(nf / max_exact) / math.log(MAX_DISTANCE / max_exact)
                         * (NUM_BUCKETS - max_exact)).astype(np.int32)
    return np.where(n < max_exact, n, np.minimum(large, NUM_BUCKETS - 1))


def _bucket_maps(a_rows, b_rows):
    steps = [d for _, d in A_PATTERNS] + [1]
    q = np.arange(BLK)[:, None]
    k = np.arange(2 * BLK)[None, :]
    dist = q + BLK - k
    valid = (dist >= 0) & (dist <= BLK)
    band = np.stack([np.where(valid, _t5_bucket(np.maximum(dist, 0) * s), -1) for s in steps])

    def cached(rows, step):
        back = rows - np.arange(rows)
        hit = (back % step == 0) & (back // step <= BLK)
        return np.where(hit, _t5_bucket(back), -1)[None, :].astype(np.int32)

    cached_a = np.stack([cached(a_rows, s) for s in steps[:-1]])
    return band.astype(np.int32), cached_a, cached(b_rows, 1)


def _band_bias_kernel(rel_ref, bmap_ref, o_ref):
    head0 = jnp.where(pl.program_id(0) == len(A_PATTERNS), N_HEADS_A, 0)
    rows = 2 * SUBLANES
    for r0 in range(0, BLK, rows):
        bm = bmap_ref[r0:r0 + rows, :]
        accs = [jnp.full(bm.shape, NEG, F32)] * (2 * N_PAIRS)
        for b in range(NUM_BUCKETS):
            hit = bm == b
            accs = [jnp.where(hit, rel_ref[b, head0 + h], acc) for h, acc in enumerate(accs)]
        for h, acc in enumerate(accs):
            o_ref[h // 2, (h % 2) * BLK + r0:(h % 2) * BLK + r0 + rows, :] = acc


def _band_bias_call(rel_bias, band_map):
    n_tab = band_map.shape[0]
    return pl.pallas_call(
        _band_bias_kernel,
        grid=(n_tab,),
        in_specs=[pl.BlockSpec(memory_space=pltpu.SMEM),
                  pl.BlockSpec((None, BLK, 2 * BLK), lambda t: (t, 0, 0))],
        out_specs=pl.BlockSpec((None, N_PAIRS, 2 * BLK, 2 * BLK), lambda t: (t, 0, 0, 0)),
        out_shape=jax.ShapeDtypeStruct((n_tab, N_PAIRS, 2 * BLK, 2 * BLK), F32),
        compiler_params=pltpu.CompilerParams(dimension_semantics=("arbitrary",)),
        name="band_bias",
    )(rel_bias, band_map)


def _cached_bias_kernel(relt_ref, amap_ref, bmap_ref, oa_ref, ob_ref):
    relt = relt_ref[...]

    def table(bm, heads):
        acc = jnp.full(bm.shape, NEG, F32)
        for b in range(NUM_BUCKETS):
            acc = jnp.where(bm == b, heads[:, b:b + 1], acc)
        return acc

    for t in range(amap_ref.shape[0]):
        oa_ref[t] = table(jnp.broadcast_to(amap_ref[t], oa_ref.shape[1:]), relt[:N_HEADS_A])
    ob_ref[...] = table(jnp.broadcast_to(bmap_ref[...], ob_ref.shape), relt[N_HEADS_A:])


def _cached_bias_call(rel_bias, cached_a, cached_b):
    return pl.pallas_call(
        _cached_bias_kernel,
        out_shape=(jax.ShapeDtypeStruct((cached_a.shape[0], N_HEADS_A, cached_a.shape[2]), F32),
                   jax.ShapeDtypeStruct((N_HEADS_B, cached_b.shape[1]), F32)),
        name="cached_bias",
    )(rel_bias.T, cached_a, cached_b)


def _ffn_kernel(*refs, alpha, pre_mix, mix_slots, n_chunks, fc, n_side, sample0, mod_row, proj=False):
    refs = list(refs)
    x_ref = refs.pop(0)
    if pre_mix:
        mix_refs = [refs.pop(0) for _ in range(2 if mix_slots else 1)]
        wo_ref, gm_ref, ln1g_ref, ln1b_ref = (refs.pop(0) for _ in range(4))
    sh_ref, sc_ref, gt_ref, wg_ref, wu_ref, wd_ref, lng_ref, lnb_ref = (refs.pop(0) for _ in range(8))
    if n_side:
        side_in = [refs.pop(0) for _ in range(10)]
        ak_hbm, av_hbm, o_ref, oc_ref, h_ref, acc_ref, kbuf, vbuf, sem = (refs.pop(0) for _ in range(9))
        side_scratch = refs
        end = sample0 + pl.num_programs(0) * n_side
        first = sample0 + pl.program_id(0) * n_side

        def cache_copies(g, slot):
            return (pltpu.make_async_copy(ak_hbm.at[g], kbuf.at[slot], sem.at[0, slot]),
                    pltpu.make_async_copy(av_hbm.at[g], vbuf.at[slot], sem.at[1, slot]))

        @pl.when(pl.program_id(0) == 0)
        def _():
            for cp in cache_copies(sample0, 0):
                cp.start()

        def side(j):
            g, slot = first + j, j % 2
            for cp in cache_copies(g, slot):
                cp.wait()

            @pl.when(g + 1 < end)
            def _():
                for cp in cache_copies(g + 1, 1 - slot):
                    cp.start()

            _cached_sample(j, kbuf.at[slot], vbuf.at[slot], *side_in, oc_ref, *side_scratch)

        assert n_side % 2 == 0
        side_at = {(j * n_chunks) // n_side if j else -1: j for j in range(n_side)}
    else:
        if proj:
            sh2_ref, sc2_ref, wp_ref, o_ref, p_ref, h_ref, acc_ref = refs
        else:
            o_ref, h_ref, acc_ref = refs
        side_at = {}

    if -1 in side_at:
        side(side_at[-1])
    x = x_ref[...]
    if pre_mix:
        if mix_slots:
            mixed = jnp.concatenate([r[j] for r in mix_refs for j in range(N_PAIRS)], axis=-1)
        else:
            mixed = mix_refs[0][...].astype(BF16)
        y = jnp.dot(mixed, wo_ref[...], preferred_element_type=F32)
        x = _layer_norm(alpha * x + _seq_mod(gm_ref, mod_row) * y, ln1g_ref[...], ln1b_ref[...])
    h_ref[...] = (x * (1.0 + _seq_mod(sc_ref, mod_row)) + _seq_mod(sh_ref, mod_row)).astype(BF16)
    for c in range(n_chunks):
        if c in side_at:
            side(side_at[c])
        cols = slice(c * fc, (c + 1) * fc)
        g = jnp.dot(h_ref[...], wg_ref[:, cols], preferred_element_type=F32)
        u = jnp.dot(h_ref[...], wu_ref[:, cols], preferred_element_type=F32)
        y = jnp.dot((_silu(g) * u).astype(BF16), wd_ref[cols, :], preferred_element_type=F32)
        if c == 0:
            acc_ref[...] = y
        else:
            acc_ref[...] += y
    out = _layer_norm(alpha * x + 0.5 * _seq_mod(gt_ref, mod_row) * acc_ref[...], lng_ref[...], lnb_ref[...])
    o_ref[...] = out
    if proj:
        h2 = (out * (1.0 + _seq_mod(sc2_ref, mod_row)) + _seq_mod(sh2_ref, mod_row)).astype(BF16)
        p_ref[...] = jnp.dot(h2, wp_ref[...], preferred_element_type=F32)


def _ffn_call(x, mod2, mod_ks, seq0, rows_per_seq, wg, wu, wd, ln_g, ln_b, alpha, tm, fc,
              mix=None, w_out=None, gate_k=None, ln1_g=None, ln1_b=None, side=None, proj=None):
    n, d = x.shape
    d_ff = wg.shape[1]
    n_chunks = d_ff // fc
    assert n_chunks * fc == d_ff
    per_row = rows_per_seq == 1
    tiles_per_seq = max(rows_per_seq // tm, 1)

    def mod_spec(k):
        if per_row:
            return pl.BlockSpec((tm, d), lambda i, k=k: (seq0 // tm + i, k))
        return _seq_mod_spec(seq0, tiles_per_seq, k, d)

    mod_arr = mod2
    row_spec = pl.BlockSpec((tm, d), lambda i: (i, 0))
    vec_spec = pl.BlockSpec((1, d), lambda i: (0, 0))
    args, specs = [x], [row_spec]
    pre_mix = mix is not None
    mix_slots = pre_mix and isinstance(mix, tuple)
    blocks = 2 * _nbytes((tm, d), F32) + (3 * _nbytes((tm, d), F32) if per_row else 0)
    resident = 3 * _nbytes(wg.shape, BF16)
    if pre_mix:
        if mix_slots:
            for mm in mix:
                args.append(mm)
                specs.append(pl.BlockSpec((N_PAIRS, tm, LANES), lambda i: (0, i, 0)))
        else:
            args.append(mix)
            specs.append(row_spec)
        args += [w_out, mod_arr, ln1_g.reshape(1, d), ln1_b.reshape(1, d)]
        specs += [_resident_spec((d, d)), mod_spec(gate_k), vec_spec, vec_spec]
        blocks += _nbytes((tm, d), F32)
        resident += _nbytes((d, d), BF16)
    args += [mod_arr, mod_arr, mod_arr, wg, wu, wd, ln_g.reshape(1, d), ln_b.reshape(1, d)]
    specs += [mod_spec(mod_ks[0]), mod_spec(mod_ks[1]), mod_spec(mod_ks[2]),
              _resident_spec(wg.shape), _resident_spec(wu.shape), _resident_spec(wd.shape), vec_spec, vec_spec]
    scratch = _nbytes((tm, d), BF16) + _nbytes((tm, d), F32) + resident
    temps = 6 * _nbytes((tm, fc), F32) + 3 * _nbytes((tm, d), F32)
    out_specs, out_shape = row_spec, jax.ShapeDtypeStruct((n, d), F32)
    scratch_shapes = [pltpu.VMEM((tm, d), BF16), pltpu.VMEM((tm, d), F32)]
    n_side = sample0 = 0
    if proj is not None:
        assert side is None
        w_ext, shift_k, scale_k = proj
        cols = w_ext.shape[1]
        args += [mod_arr, mod_arr, w_ext]
        specs += [mod_spec(shift_k), mod_spec(scale_k), _resident_spec(w_ext.shape)]
        blocks += 2 * _nbytes((tm, d), F32) + 2 * _nbytes((tm, cols), F32)
        scratch += _nbytes(w_ext.shape, BF16)
        temps += _nbytes((tm, cols), F32)
        out_specs = [row_spec, pl.BlockSpec((tm, cols), lambda i: (i, 0))]
        out_shape = [out_shape, jax.ShapeDtypeStruct((n, cols), F32)]
    if side is not None:
        (*small, ak, av), sample0, count = side
        n_steps = n // tm
        n_side = count // n_steps
        assert n_side * n_steps == count and sample0 % n_side == 0
        blk0 = sample0 // n_side
        for a in small[:6]:
            args.append(a)
            specs.append(pl.BlockSpec((n_side,) + a.shape[1:],
                                      lambda i, nd=a.ndim: (blk0 + i,) + (0,) * (nd - 1)))
            blocks += n_side * _nbytes(a.shape[1:-1] + (max(a.shape[-1], LANES),), F32)
        for a in small[6:]:
            args.append(a)
            specs.append(pl.BlockSpec(a.shape, lambda i, nd=a.ndim: (0,) * nd))
            blocks += _nbytes(a.shape, F32)
        args += [ak, av]
        specs += [pl.BlockSpec(memory_space=pl.ANY)] * 2
        q3 = small[0]
        out_specs = [row_spec, pl.BlockSpec((n_side,) + q3.shape[1:], lambda i: (i, 0, 0))]
        out_shape = [out_shape, jax.ShapeDtypeStruct((count,) + q3.shape[1:], F32)]
        sa_shape = (n_side, N_HEADS_A, ak.shape[3])
        sb_shape = (n_side, N_HEADS_B, small[4].shape[3])
        scratch_shapes += [pltpu.VMEM((2,) + ak.shape[1:], F32), pltpu.VMEM((2,) + av.shape[1:], F32),
                           pltpu.SemaphoreType.DMA((2, 2)),
                           pltpu.VMEM(sa_shape, F32), pltpu.VMEM(sb_shape, F32),
                           pltpu.VMEM(sa_shape, F32), pltpu.VMEM(sb_shape, F32)]
        scratch += 4 * _nbytes(ak.shape[1:], F32) + 2 * _nbytes(sa_shape, F32) + 2 * _nbytes(sb_shape, F32)
    return pl.pallas_call(
        functools.partial(_ffn_kernel, alpha=alpha, pre_mix=pre_mix, mix_slots=mix_slots,
                          n_chunks=n_chunks, fc=fc, n_side=n_side, sample0=sample0,
                          mod_row=None if per_row else (seq0, tiles_per_seq), proj=proj is not None),
        grid=(n // tm,),
        in_specs=specs,
        out_specs=out_specs,
        out_shape=out_shape,
        scratch_shapes=scratch_shapes,
        compiler_params=pltpu.CompilerParams(
            dimension_semantics=("arbitrary",),
            vmem_limit_bytes=_vmem_limit(blocks, scratch, temps)),
        name=("ffn_mix" if pre_mix else "ffn") + ("_cached" if n_side else "") + ("_proj" if proj is not None else ""),
    )(*args)


def _proj_prompt_kernel(x_ref, sh_ref, sc_ref, w_ref, p_ref, p4_ref, p16_ref, ak_ref, av_ref, bk_ref, bv_ref,
                        stage_ref, stage4_ref, *, tm, b_rows, mod_row):
    h = (x_ref[...] * (1.0 + _seq_mod(sc_ref, mod_row)) + _seq_mod(sh_ref, mod_row)).astype(BF16)
    lane_lo = lax.broadcasted_iota(jnp.int32, (tm, LANES), 1) < HEAD_DIM
    for g in range(4):
        res = jnp.dot(h, w_ref[:, g * A_COLS:(g + 1) * A_COLS], preferred_element_type=F32)
        scaled = res * SCALE if g in (0, 3) else res
        for s in range(N_PAIRS):
            slab = scaled[:, s * LANES:(s + 1) * LANES]
            p_ref[g * N_PAIRS + s] = slab.astype(BF16)
            if g < 3:
                slot = g * N_PAIRS + s
                d4, d16 = p4_ref.shape[1], p16_ref.shape[1]
                stage_ref[slot] = slab
                for r in range(d4):
                    rows = stage_ref[slot, pl.ds(r, tm // d4, stride=d4), :]
                    p4_ref[slot, r] = rows.astype(BF16)
                    stage4_ref[slot, r] = rows
                for r in range(d16):
                    rows = stage4_ref[slot, r % d4, pl.ds(r // d4, tm // d16, stride=d16 // d4), :]
                    p16_ref[slot, r] = rows.astype(BF16)
        if g in (1, 2):
            out_ref = ak_ref if g == 1 else av_ref
            out_ref[...] = res
    res = jnp.dot(h, w_ref[:, 4 * A_COLS:], preferred_element_type=F32)
    for j, out_ref in enumerate((bk_ref, bv_ref)):
        slab = res[:, j * LANES:(j + 1) * LANES]
        swapped = pltpu.roll(slab, HEAD_DIM, axis=1)
        p_ref[4 * N_PAIRS + N_KV_B * j] = jnp.where(lane_lo, slab, swapped).astype(BF16)
        p_ref[4 * N_PAIRS + N_KV_B * j + 1] = jnp.where(lane_lo, swapped, slab).astype(BF16)

        out_ref[...] = slab[tm - b_rows:, :]


def _proj_prompt_call(x, mod2, seq0, seq, w_ext, a_rows, b_rows, tm):
    n, d = x.shape
    batch = n // seq
    tiles_per_seq = seq // tm
    a_tiles = a_rows // tm
    n_slots = 4 * N_PAIRS + 2 * N_KV_B
    assert w_ext.shape[1] == 4 * A_COLS + 2 * LANES and N_KV_B * HEAD_DIM == LANES

    def mod_spec(k):
        return _seq_mod_spec(seq0, tiles_per_seq, k, d)

    def a_map(i):
        return (i // tiles_per_seq, jnp.maximum(i % tiles_per_seq - (tiles_per_seq - a_tiles), 0), 0)

    a_blk = (None, tm, A_COLS)
    b_blk = (None, b_rows, LANES)
    a_slots = 3 * N_PAIRS
    d4, d16 = A_PATTERNS[1][1], A_PATTERNS[2][1]
    assert tm % (16 * d16) == 0

    def dil_spec(dil):
        return pl.BlockSpec((a_slots, None, dil, tm // dil, LANES),
                            lambda i: (0, i // tiles_per_seq, 0, i % tiles_per_seq, 0))

    blocks = (_nbytes((tm, d), F32) + _nbytes(w_ext.shape, BF16) + _nbytes((n_slots + 2 * a_slots, tm, LANES), BF16)
              + 2 * _nbytes((tm, A_COLS), F32) + 2 * _nbytes((b_rows, LANES), F32))
    return pl.pallas_call(
        functools.partial(_proj_prompt_kernel, tm=tm, b_rows=b_rows, mod_row=(seq0, tiles_per_seq)),
        grid=(n // tm,),
        in_specs=[pl.BlockSpec((tm, d), lambda i: (i, 0)), mod_spec(3), mod_spec(4),
                  pl.BlockSpec(w_ext.shape, lambda i: (0, 0))],
        out_specs=[pl.BlockSpec((n_slots, tm, LANES), lambda i: (0, i, 0)),
                   dil_spec(d4), dil_spec(d16),
                   pl.BlockSpec(a_blk, a_map),
                   pl.BlockSpec(a_blk, a_map),
                   pl.BlockSpec(b_blk, lambda i: (i // tiles_per_seq, 0, 0)),
                   pl.BlockSpec(b_blk, lambda i: (i // tiles_per_seq, 0, 0))],
        out_shape=[jax.ShapeDtypeStruct((n_slots, n, LANES), BF16),
                   jax.ShapeDtypeStruct((a_slots, batch, d4, seq // d4, LANES), BF16),
                   jax.ShapeDtypeStruct((a_slots, batch, d16, seq // d16, LANES), BF16),
                   jax.ShapeDtypeStruct((batch, a_rows, A_COLS), F32),
                   jax.ShapeDtypeStruct((batch, a_rows, A_COLS), F32),
                   jax.ShapeDtypeStruct((batch, b_rows, LANES), F32),
                   jax.ShapeDtypeStruct((batch, b_rows, LANES), F32)],
        scratch_shapes=[pltpu.VMEM((a_slots, tm, LANES), F32), pltpu.VMEM((a_slots, d4, tm // d4, LANES), F32)],
        compiler_params=pltpu.CompilerParams(
            dimension_semantics=("arbitrary",),
            vmem_limit_bytes=_vmem_limit(blocks, 2 * _nbytes((a_slots, tm, LANES), F32),
                                         4 * _nbytes((tm, A_COLS), F32))),
        name="proj_prompt",
    )(x, mod2, mod2, w_ext)


def _pair_scores(q, kk, bias, lane_lo):
    zero = jnp.zeros_like(q)
    q2 = jnp.concatenate([jnp.where(lane_lo, q, zero), jnp.where(lane_lo, zero, q)], axis=0)
    s = lax.dot_general(q2, kk, (((1,), (1,)), ((), ())), preferred_element_type=F32)
    return s + bias


def _pair_select(a, lane_lo):
    return jnp.where(lane_lo, a[:BLK], a[BLK:])


GROUP = 16


def _band_sequence(q_ref, k_ref, v_ref, n_blocks, load_bias, block):
    def one(jb, first):
        if first:
            block(jb, q_ref[0:BLK, :], k_ref[0:BLK, :], v_ref[0:BLK, :], load_bias(True))
        else:
            r0 = jb * BLK
            if not isinstance(jb, int):
                r0 = pl.multiple_of(r0, BLK)
            block(jb, q_ref[pl.ds(r0, BLK), :], k_ref[pl.ds(r0 - BLK, 2 * BLK), :],
                  v_ref[pl.ds(r0 - BLK, 2 * BLK), :], load_bias(False))

    for jb in range(min(GROUP, n_blocks)):
        one(jb, jb == 0)
    if n_blocks > GROUP:
        assert n_blocks % GROUP == 0

        def group(g, carry):
            for j in range(GROUP):
                one(g * GROUP + j, False)
            return carry

        lax.fori_loop(1, n_blocks // GROUP, group, 0)


def _attn_a_kernel(q16_ref, k16_ref, v16_ref, q4_ref, k4_ref, v4_ref, q1_ref, k1_ref, v1_ref,
                   bias_ref, o_ref, acc16_ref, m16_ref, den16_ref, acc4_ref, m4_ref, den4_ref, *, seq):
    lane_lo = lax.broadcasted_iota(jnp.int32, (BLK, LANES), 1) < HEAD_DIM
    state16 = (acc16_ref, m16_ref, den16_ref)
    state4 = (acc4_ref, m4_ref, den4_ref)

    def update(q, kk, vv, bias, rows, old, new):
        s = _pair_scores(q, kk, bias, lane_lo)
        part = s[:, :LANES] if s.shape[1] == LANES else jnp.maximum(s[:, :LANES], s[:, LANES:])
        if old is not None:
            acc_old, m_old, den_old = (ref[rows, :] for ref in old)
            part = jnp.maximum(part, jnp.concatenate([jnp.where(lane_lo, m_old, NEG),
                                                      jnp.where(lane_lo, NEG, m_old)], axis=0))
        m2 = jnp.max(part, axis=-1, keepdims=True)
        p = jnp.exp(s - m2)
        l2 = jnp.sum(p, axis=-1, keepdims=True)
        pv = _pair_select(jnp.dot(p.astype(BF16), vv, preferred_element_type=F32), lane_lo)
        m_new = _pair_select(m2, lane_lo)
        den = _pair_select(l2, lane_lo)
        if old is None:
            acc = pv
        else:
            a = jnp.exp(m_old - m_new)
            acc = a * acc_old + pv
            den = a * den_old + den
        if new is None:
            o_ref[rows, :] = (acc / den).astype(o_ref.dtype)
        else:
            new[0][rows, :] = acc
            new[1][rows, :] = m_new
            new[2][rows, :] = den

    def branch(q_ref, k_ref, v_ref, table, dil, r, old, new):
        def block(jb, q, kk, vv, bias):
            start = r + jb * (BLK * dil)
            rows = pl.ds(start, BLK, stride=dil) if dil > 1 else pl.ds(start, BLK)
            update(q, kk, vv, bias, rows, old, new)

        def load_bias(first):
            return bias_ref[table, :, BLK:2 * BLK] if first else bias_ref[table]

        _band_sequence(q_ref, k_ref, v_ref, q_ref.shape[0] // BLK, load_bias, block)

    def dilated(q_ref, k_ref, v_ref, table, old, new):
        dil = q_ref.shape[0]
        per_group = min(max(GROUP // (q_ref.shape[1] // BLK), 1), dil)
        assert dil % per_group == 0

        def group(g, carry):
            for rr in range(per_group):
                r = g * per_group + rr
                branch(q_ref.at[r], k_ref.at[r], v_ref.at[r], table, dil, r, old, new)
            return carry

        lax.fori_loop(0, dil // per_group, group, 0)

    dilated(q16_ref, k16_ref, v16_ref, 2, None, state16)
    dilated(q4_ref, k4_ref, v4_ref, 1, state16, state4)
    branch(q1_ref, k1_ref, v1_ref, 0, 1, 0, state4, None)


def _attn_a_call(p_slots, p4, p16, bias_a, batch, seq):
    n_slots = p_slots.shape[0]
    dils = [d for _, d in A_PATTERNS]
    assert dils == [1, p4.shape[2], p16.shape[2]] and seq % (dils[2] * BLK) == 0
    view1 = p_slots.reshape(n_slots, batch, seq, LANES)

    in_specs, args = [], []
    for arr in (p16, p4, view1):
        for slot0 in (0, N_PAIRS, 2 * N_PAIRS):
            in_specs.append(pl.BlockSpec((None, None) + arr.shape[2:],
                                         lambda b, hp, slot0=slot0, nd=arr.ndim: (slot0 + hp, b) + (0,) * (nd - 2)))
            args.append(arr)
    in_specs.append(pl.BlockSpec((len(A_PATTERNS), None, 2 * BLK, 2 * BLK), lambda b, hp: (0, hp, 0, 0)))
    args.append(bias_a)
    blocks = 10 * _nbytes((seq, LANES), BF16) + _nbytes((3, 2 * BLK, 2 * BLK), F32)
    scratch = 6 * _nbytes((seq, LANES), F32)
    return pl.pallas_call(
        functools.partial(_attn_a_kernel, seq=seq),
        grid=(batch, N_PAIRS),
        in_specs=in_specs,
        out_specs=pl.BlockSpec((None, seq, LANES), lambda b, hp: (hp, b, 0)),
        out_shape=jax.ShapeDtypeStruct((N_PAIRS, batch * seq, LANES), BF16),
        scratch_shapes=[pltpu.VMEM((seq, LANES), F32)] * 6,
        compiler_params=pltpu.CompilerParams(
            dimension_semantics=("arbitrary", "arbitrary"),
            vmem_limit_bytes=_vmem_limit(blocks, scratch, 4 * GROUP * _nbytes((2 * BLK, 2 * BLK), F32))),
        name="attn_dilated",
    )(*args)


def _attn_b_kernel(sink_ref, q_ref, k_ref, v_ref, bias_ref, o_ref, *, seq):
    hp = pl.program_id(1)
    lane_lo = lax.broadcasted_iota(jnp.int32, (BLK, LANES), 1) < HEAD_DIM
    row_lo = lax.broadcasted_iota(jnp.int32, (2 * BLK, 1), 0) < BLK
    sink2 = jnp.where(row_lo, sink_ref[2 * hp], sink_ref[2 * hp + 1])

    def block(jb, q, kk, vv, bias):
        s = _pair_scores(q, kk, bias, lane_lo)
        m2 = jnp.maximum(jnp.max(s, axis=-1, keepdims=True), sink2)
        p = jnp.exp(s - m2)
        l2 = jnp.sum(p, axis=-1, keepdims=True) + jnp.exp(sink2 - m2)
        pv = _pair_select(jnp.dot(p.astype(BF16), vv, preferred_element_type=F32), lane_lo)
        r0 = jb * BLK if isinstance(jb, int) else pl.multiple_of(jb * BLK, BLK)
        o_ref[pl.ds(r0, BLK), :] = (pv / _pair_select(l2, lane_lo)).astype(o_ref.dtype)

    def load_bias(first):
        return bias_ref[:, BLK:2 * BLK] if first else bias_ref[...]

    _band_sequence(q_ref, k_ref, v_ref, seq // BLK, load_bias, block)


def _attn_b_call(p_slots, bias_b, sinks, batch, seq):
    n_slots = p_slots.shape[0]
    view = p_slots.reshape(n_slots, batch, seq, LANES)
    q0, k0, v0 = 3 * N_PAIRS, 4 * N_PAIRS, 4 * N_PAIRS + N_KV_B
    pairs_per_kv = N_PAIRS // N_KV_B
    blk = (None, None, seq, LANES)
    blocks = 4 * _nbytes((seq, LANES), BF16) + _nbytes((2 * BLK, 2 * BLK), F32)
    return pl.pallas_call(
        functools.partial(_attn_b_kernel, seq=seq),
        grid=(batch, N_PAIRS),
        in_specs=[pl.BlockSpec(memory_space=pltpu.SMEM),
                  pl.BlockSpec(blk, lambda b, hp: (q0 + hp, b, 0, 0)),
                  pl.BlockSpec(blk, lambda b, hp: (k0 + hp // pairs_per_kv, b, 0, 0)),
                  pl.BlockSpec(blk, lambda b, hp: (v0 + hp // pairs_per_kv, b, 0, 0)),
                  pl.BlockSpec((None, 2 * BLK, 2 * BLK), lambda b, hp: (hp, 0, 0))],
        out_specs=pl.BlockSpec((None, seq, LANES), lambda b, hp: (hp, b, 0)),
        out_shape=jax.ShapeDtypeStruct((N_PAIRS, batch * seq, LANES), BF16),
        compiler_params=pltpu.CompilerParams(
            dimension_semantics=("arbitrary", "arbitrary"),
            vmem_limit_bytes=_vmem_limit(blocks, temp_bytes=8 * _nbytes((2 * BLK, 2 * BLK), F32))),
        name="attn_window",
    )(sinks, view, view, view, bias_b)


def _cached_sample(i, ak_ref, av_ref, q_ref, qt_ref, knt_ref, vn_ref, bk_ref, bv_ref, bias_a_ref, bias_b_ref,
                   relt_ref, sink_ref, o_ref, sa_ref, sb_ref, wa_ref, wb_ref):
    n_branches = bias_a_ref.shape[0]
    heads_per_kv = N_HEADS_B // N_KV_B
    bias0 = relt_ref[:, 0:1]
    sink = sink_ref[...]
    q = q_ref[i] * SCALE
    vn = vn_ref[i]
    s0 = jnp.sum(qt_ref[i] * SCALE * knt_ref[i], axis=-1, keepdims=True) + bias0
    for h in range(N_HEADS_A):
        sa_ref[i, h:h + 1, :] = jnp.sum(ak_ref[h] * q[:, h:h + 1], axis=0, keepdims=True)
    for hb in range(N_HEADS_B):
        h = N_HEADS_A + hb
        sb_ref[i, hb:hb + 1, :] = jnp.sum(bk_ref[i, hb // heads_per_kv] * q[:, h:h + 1], axis=0, keepdims=True)

    s0a = s0[:N_HEADS_A]
    ts = [sa_ref[i] + bias_a_ref[t] for t in range(n_branches)]
    m = s0a
    for t_ in ts:
        m = jnp.maximum(m, jnp.max(t_, axis=-1, keepdims=True))
    w = jnp.exp(ts[0] - m)
    for t_ in ts[1:]:
        w = w + jnp.exp(t_ - m)
    p0a = n_branches * jnp.exp(s0a - m)
    inv = 1.0 / (jnp.sum(w, axis=-1, keepdims=True) + p0a)
    wa_ref[i] = w * inv
    p0a = p0a * inv

    s0b = s0[N_HEADS_A:]
    tb = sb_ref[i] + bias_b_ref[...]
    mb = jnp.maximum(jnp.maximum(s0b, sink), jnp.max(tb, axis=-1, keepdims=True))
    wb = jnp.exp(tb - mb)
    p0b = jnp.exp(s0b - mb)
    invb = 1.0 / (jnp.sum(wb, axis=-1, keepdims=True) + p0b + jnp.exp(sink - mb))
    wb_ref[i] = wb * invb
    p0b = p0b * invb

    cols = []
    for h in range(N_HEADS_A):
        acc = jnp.sum(av_ref[h] * wa_ref[i, h:h + 1, :], axis=-1, keepdims=True)
        cols.append(acc + p0a[h:h + 1, :] * vn[:, h:h + 1])
    for hb in range(N_HEADS_B):
        h = N_HEADS_A + hb
        acc = jnp.sum(bv_ref[i, hb // heads_per_kv] * wb_ref[i, hb:hb + 1, :], axis=-1, keepdims=True)
        cols.append(acc + p0b[hb:hb + 1, :] * vn[:, h:h + 1])
    o_ref[i] = jnp.concatenate(cols, axis=1)


def kernel(x_prompt, x_sample, cache_a_k, cache_a_v, cache_b_k, cache_b_v, c_prompt, c_sample, rel_bias, w_ada, b_ada, ffn1_wg, ffn1_wu, ffn1_wd, w_in, w_out, sinks, ffn2_wg, ffn2_wu, ffn2_wd, ln_g, ln_b):
    batch, seq, d = x_prompt.shape
    n_dec, dec_seq, _ = x_sample.shape
    depth = w_ada.shape[0]
    assert dec_seq == 1 and d == 2 * A_COLS and all(w // dl == BLK for w, dl in A_PATTERNS)
    alpha = (2 * depth) ** 0.25
    fc = 256
    tm = 512
    tm_proj = 1024 if seq % 1024 == 0 else tm
    a_rows, b_rows = min(A_PATTERNS[-1][0], seq), min(B_WINDOW, seq)
    q_cols = 4 * A_COLS
    kvb = N_KV_B * HEAD_DIM
    heads_per_kv = N_HEADS_B // N_KV_B

    band_map, cached_a, cached_b = _bucket_maps(cache_a_k.shape[2], cache_b_k.shape[2])
    band_bias = _band_bias_call(rel_bias, band_map)
    bias_a, bias_b = band_bias[:len(A_PATTERNS)], band_bias[len(A_PATTERNS)]
    bias_sa, bias_sb = _cached_bias_call(rel_bias, cached_a, cached_b)

    pad = (-(n_dec + batch)) % 16
    c_all = jnp.concatenate([c_sample, c_prompt, jnp.zeros((pad, d), F32)], axis=0)
    xp = x_prompt.reshape(batch * seq, d)
    xs = x_sample.reshape(n_dec, d)
    outs = [[] for _ in range(8)]

    def head_rows(x, n_heads):
        return x.reshape(n_dec, n_heads, HEAD_DIM)

    for l in range(depth):
        mod2 = _ada_call(c_all, w_ada[l], b_ada[l])
        ffn1 = [w[l].astype(BF16) for w in (ffn1_wg, ffn1_wu, ffn1_wd)]
        ffn2 = [w[l].astype(BF16) for w in (ffn2_wg, ffn2_wu, ffn2_wd)]
        w_in_s = w_in[l].astype(BF16)
        wo = w_out[l].astype(BF16)

        s1, proj_s = _ffn_call(xs, mod2, (0, 1, 2), 0, 1, *ffn1, ln_g[l, 0], ln_b[l, 0], alpha, n_dec, fc,
                               proj=(w_in_s, 3, 4))
        c = A_COLS
        sak, sav = proj_s[:, c:2 * c], proj_s[:, 2 * c:3 * c]
        sbk, sbv = proj_s[:, q_cols:q_cols + kvb], proj_s[:, q_cols + kvb:q_cols + 2 * kvb]
        qt3 = jnp.concatenate([head_rows(proj_s[:, :c], N_HEADS_A),
                               head_rows(proj_s[:, 3 * c:q_cols], N_HEADS_B)], axis=1)
        knt3 = jnp.concatenate([head_rows(sak, N_HEADS_A),
                                jnp.repeat(head_rows(sbk, N_KV_B), heads_per_kv, axis=1)], axis=1)
        vnt3 = jnp.concatenate([head_rows(sav, N_HEADS_A),
                                jnp.repeat(head_rows(sbv, N_KV_B), heads_per_kv, axis=1)], axis=1)
        to_cols = (0, 2, 3, 1)
        decode_side = (qt3.transpose(0, 2, 1), qt3, knt3, vnt3.transpose(0, 2, 1),
                       cache_b_k[l].transpose(to_cols), cache_b_v[l].transpose(to_cols),
                       bias_sa, bias_sb, rel_bias.T, sinks[l].reshape(N_HEADS_B, 1),
                       cache_a_k[l].transpose(to_cols), cache_a_v[l].transpose(to_cols))

        n_first = n_dec // 2 if (n_dec // 2) % (2 * (batch * seq // tm)) == 0 else n_dec
        x1, mix3 = _ffn_call(xp, mod2, (0, 1, 2), n_dec, seq, *ffn1, ln_g[l, 0], ln_b[l, 0], alpha, tm, fc,
                             side=(decode_side, 0, n_first))
        p_slots, p4, p16, pak, pav, pbk, pbv = _proj_prompt_call(x1, mod2, n_dec, seq, w_in_s, a_rows, b_rows, tm_proj)
        mix_a = _attn_a_call(p_slots, p4, p16, bias_a, batch, seq)
        mix_b = _attn_b_call(p_slots, bias_b, sinks[l], batch, seq)
        xp = _ffn_call(x1, mod2, (6, 7, 8), n_dec, seq, *ffn2, ln_g[l, 2], ln_b[l, 2], alpha, tm, fc,
                       mix=(mix_a, mix_b), w_out=wo, gate_k=5, ln1_g=ln_g[l, 1], ln1_b=ln_b[l, 1],
                       side=(decode_side, n_first, n_dec - n_first) if n_first < n_dec else None)
        if n_first < n_dec:
            xp, mix3_rest = xp
            mix3 = jnp.concatenate([mix3, mix3_rest], axis=0)

        mix_s = mix3.transpose(0, 2, 1).reshape(n_dec, d)
        xs = _ffn_call(s1, mod2, (6, 7, 8), 0, 1, *ffn2, ln_g[l, 2], ln_b[l, 2], alpha, n_dec, fc,
                       mix=mix_s, w_out=wo, gate_k=5, ln1_g=ln_g[l, 1], ln1_b=ln_b[l, 1])

        new = [pak.reshape(batch, a_rows, N_HEADS_A, HEAD_DIM), pav.reshape(batch, a_rows, N_HEADS_A, HEAD_DIM),
               pbk.reshape(batch, b_rows, N_KV_B, HEAD_DIM), pbv.reshape(batch, b_rows, N_KV_B, HEAD_DIM),
               sak.reshape(n_dec, 1, N_HEADS_A, HEAD_DIM), sav.reshape(n_dec, 1, N_HEADS_A, HEAD_DIM),
               sbk.reshape(n_dec, 1, N_KV_B, HEAD_DIM), sbv.reshape(n_dec, 1, N_KV_B, HEAD_DIM)]
        for acc, arr in zip(outs, new):
            acc.append(arr)

    return (xp.reshape(batch, seq, d), xs.reshape(n_dec, 1, d)) + tuple(jnp.stack(o) for o in outs)
```
